```python
import math
import jax, jax.numpy as jnp
from jax import lax
import numpy as np

D_MODEL = 1024
BATCH = 8
SEQ = 2048
DEPTH = 2
DEC_BATCH = 128
DEC_SEQ = 4
PAST_LEN = 16384
PAGE_SIZE = 128

HG_HEADS = 8
HG_DK = 128
HG_DV = 128
HG_DIM = HG_HEADS * HG_DK
HG_CHUNK = 32
SSD_EXPAND = 2
SSD_INNER = SSD_EXPAND * D_MODEL
SSD_HEADDIM = 64
SSD_HEADS = SSD_INNER // SSD_HEADDIM
SSD_GROUPS = 4
SSD_HPG = SSD_HEADS // SSD_GROUPS
SSD_STATE = 128
CONV_K = 4
CONV_DIM = SSD_INNER + 2 * SSD_GROUPS * SSD_STATE
SSD_CHUNK = 64
N_EXPERTS = 32
TOP_K = 4
D_FF = D_MODEL
SWIGLU_LIMIT = 7.0
SWIGLU_ALPHA = 1.702
MOE_BLOCK = 128
ALPHA = (2.0 * DEPTH) ** 0.25
BETA = (8.0 * DEPTH) ** -0.25
LN_EPS = 1e-5
RMS_EPS = 1e-6
IN_SPLITS = (HG_DIM, HG_DIM, HG_DIM, HG_DIM, SSD_INNER, CONV_DIM, SSD_HEADS, D_MODEL, D_MODEL)
N_IN = sum(IN_SPLITS)

kernel_name = 'hgrn2_mamba2_moe_deepnorm_step'


def _split_cols(a, sizes):
    idx = np.cumsum(sizes)[:-1].tolist()
    return jnp.split(a, idx, axis=-1)


def layer_norm(x, g, b):
    xf = x.astype(jnp.float32)
    mu = jnp.mean(xf, axis=-1, keepdims=True)
    var = jnp.mean(jnp.square(xf - mu), axis=-1, keepdims=True)
    return ((xf - mu) * lax.rsqrt(var + LN_EPS) * g + b).astype(x.dtype)


def rms_norm(x, w):
    xf = x.astype(jnp.float32)
    return xf * lax.rsqrt(jnp.mean(xf * xf, axis=-1, keepdims=True) + RMS_EPS) * w


def _to_chunks(a, c):
    b, l = a.shape[:2]
    return jnp.swapaxes(a.reshape((b, l // c, c) + a.shape[2:]), 0, 1)


def _from_chunks(a):
    n, b, c = a.shape[:3]
    return jnp.swapaxes(a, 0, 1).reshape((b, n * c) + a.shape[3:])


def hgrn2_recurrence(q, k, v, log_f, s0):
    c = math.gcd(q.shape[1], HG_CHUNK)
    causal = jnp.tril(jnp.ones((c, c), bool))[None, :, :, None, None]

    def step(s, inp):
        qc, kc, vc, lf = inp
        bcum = jnp.cumsum(lf, axis=1)
        o_inter = jnp.einsum('bthk,bhkv->bthv', qc * jnp.exp(bcum), s)
        decay = jnp.exp(jnp.where(causal, bcum[:, :, None] - bcum[:, None], -jnp.inf))
        scores = jnp.einsum('bthk,btshk->bths', qc, decay * kc[:, None])
        o_intra = jnp.einsum('bths,bshv->bthv', scores, vc)
        blast = bcum[:, -1]
        s_new = jnp.exp(blast)[..., None] * s + jnp.einsum('bshk,bshv->bhkv', kc * jnp.exp(blast[:, None] - bcum), vc)
        return s_new, o_inter + o_intra

    xs = (_to_chunks(q, c), _to_chunks(k, c), _to_chunks(v, c), _to_chunks(log_f, c))
    s_fin, o = lax.scan(step, s0, xs)
    return _from_chunks(o), s_fin


def ssd_recurrence(x, dt, a, bm, cm, h0):
    c = math.gcd(x.shape[1], SSD_CHUNK)
    causal = jnp.tril(jnp.ones((c, c), bool))[None, :, :, None, None]

    def step(h, inp):
        xc, dtc, bc, cc = inp
        acum = jnp.cumsum(dtc * a, axis=1)
        seg = jnp.exp(jnp.where(causal, acum[:, :, None] - acum[:, None], -jnp.inf))
        cb = jnp.einsum('btgn,bsgn->btsg', cc, bc)
        w = cb[..., None] * seg * dtc[:, None]
        y_intra = jnp.einsum('btsgr,bsgrp->btgrp', w, xc)
        y_inter = jnp.einsum('btgn,bgrpn->btgrp', cc, h) * jnp.exp(acum)[..., None]
        alast = acum[:, -1]
        wdec = jnp.exp(alast[:, None] - acum) * dtc
        h_new = jnp.exp(alast)[..., None, None] * h + jnp.einsum('bsgr,bsgn,bsgrp->bgrpn', wdec, bc, xc)
        return h_new, y_intra + y_inter

    xs = (_to_chunks(x, c), _to_chunks(dt, c), _to_chunks(bm, c), _to_chunks(cm, c))
    h_fin, y = lax.scan(step, h0, xs)
    return _from_chunks(y), h_fin


def causal_dwconv(u, prev, w, b):
    L = u.shape[1]
    full = jnp.concatenate([prev.astype(u.dtype), u], axis=1)
    out = b + w[0] * full[:, 0:L]
    for j in range(1, CONV_K):
        out = out + w[j] * full[:, j:j + L]
    return out, full[:, -(CONV_K - 1):]


def mixer_block(x, s_hg, h_ssm, conv_prev, lb, w_in, hg_norm_w, hg_proj, conv_w, conv_b,
                dt_bias, a_log, d_skip, ssd_norm_w, ssd_proj, w_out):
    B, L, _ = x.shape
    f32 = jnp.float32
    q, fr, iv, g, z, xbc, dt, gate_a, gate_b = _split_cols(x @ w_in, IN_SPLITS)

    q = jax.nn.silu(q.astype(f32)).reshape(B, L, HG_HEADS, HG_DK)
    lbh = lb.astype(f32).reshape(HG_HEADS, HG_DK)
    log_f = jnp.logaddexp(jnp.log(lbh), jnp.log1p(-lbh) + jax.nn.log_sigmoid(fr.astype(f32).reshape(B, L, HG_HEADS, HG_DK)))
    k = -jnp.expm1(log_f)
    v = iv.astype(f32).reshape(B, L, HG_HEADS, HG_DV)
    o, s_hg_new = hgrn2_recurrence(q, k, v, log_f, s_hg.astype(f32))
    o = rms_norm(o, hg_norm_w.reshape(HG_HEADS, HG_DV)).reshape(B, L, HG_DIM) * jax.nn.sigmoid(g.astype(f32))
    out_a = o.astype(x.dtype) @ hg_proj

    xbc, conv_new = causal_dwconv(xbc, conv_prev, conv_w, conv_b)
    xbc = jax.nn.silu(xbc.astype(f32))
    xs, bm, cm = _split_cols(xbc, (SSD_INNER, SSD_GROUPS * SSD_STATE, SSD_GROUPS * SSD_STATE))
    xs = xs.reshape(B, L, SSD_GROUPS, SSD_HPG, SSD_HEADDIM)
    bm = bm.reshape(B, L, SSD_GROUPS, SSD_STATE)
    cm = cm.reshape(B, L, SSD_GROUPS, SSD_STATE)
    dt = jax.nn.softplus(dt.astype(f32) + dt_bias).reshape(B, L, SSD_GROUPS, SSD_HPG)
    a = -jnp.exp(a_log.astype(f32)).reshape(SSD_GROUPS, SSD_HPG)
    h0 = h_ssm.astype(f32).reshape(B, SSD_GROUPS, SSD_HPG, SSD_HEADDIM, SSD_STATE)
    y, h_new = ssd_recurrence(xs, dt, a, bm, cm, h0)
    y = y + d_skip.reshape(SSD_GROUPS, SSD_HPG)[..., None] * xs
    y = y.reshape(B, L, SSD_GROUPS, SSD_INNER // SSD_GROUPS) * jax.nn.silu(z.astype(f32)).reshape(B, L, SSD_GROUPS, SSD_INNER // SSD_GROUPS)
    y = rms_norm(y, ssd_norm_w.reshape(SSD_GROUPS, SSD_INNER // SSD_GROUPS)).reshape(B, L, SSD_INNER)
    out_b = y.astype(x.dtype) @ ssd_proj

    merged = jax.nn.sigmoid(gate_a) * out_a + jax.nn.sigmoid(gate_b) * out_b
    h_new = h_new.reshape(B, SSD_HEADS, SSD_HEADDIM, SSD_STATE)
    return merged @ w_out, s_hg_new.astype(s_hg.dtype), h_new.astype(h_ssm.dtype), conv_new


def moe_ffn(x, router_w, router_b, w_gu, b_gu, w_down, b_down):
    B, L, D = x.shape
    T = B * L
    S = T * TOP_K
    f32 = jnp.float32
    xt = x.reshape(T, D)
    logits = (xt @ router_w + router_b).astype(f32)
    top_v, top_e = lax.top_k(logits, TOP_K)
    gates = jax.nn.softmax(top_v, axis=-1)
    e_slot = top_e.reshape(S).astype(jnp.int32)
    tok_slot = jnp.arange(S, dtype=jnp.int32) // TOP_K
    order = jnp.argsort(e_slot)
    e_sorted = e_slot[order]
    counts = jnp.bincount(e_slot, length=N_EXPERTS).astype(jnp.int32)
    start = jnp.cumsum(counts) - counts
    padded = (counts + MOE_BLOCK - 1) // MOE_BLOCK * MOE_BLOCK
    pend = jnp.cumsum(padded)
    pstart = pend - padded
    dest_sorted = (pstart[e_sorted] + jnp.arange(S, dtype=jnp.int32) - start[e_sorted]).astype(jnp.int32)
    dest = jnp.zeros((S,), jnp.int32).at[order].set(dest_sorted)
    n_blocks = (S + N_EXPERTS * (MOE_BLOCK - 1) + MOE_BLOCK - 1) // MOE_BLOCK
    buf = jnp.zeros((n_blocks * MOE_BLOCK, D), x.dtype).at[dest].set(xt[tok_slot])
    block_e = jnp.minimum(jnp.searchsorted(pend, jnp.arange(n_blocks, dtype=jnp.int32) * MOE_BLOCK, side='right'), N_EXPERTS - 1)

    def expert(args):
        xb, e = args
        gu = (xb @ w_gu[e] + b_gu[e]).astype(f32)
        gate, up = jnp.split(gu, 2, axis=-1)
        gate = jnp.minimum(gate, SWIGLU_LIMIT)
        up = jnp.clip(up, -SWIGLU_LIMIT, SWIGLU_LIMIT)
        act = (up + 1.0) * gate * jax.nn.sigmoid(SWIGLU_ALPHA * gate)
        return act.astype(x.dtype) @ w_down[e] + b_down[e]

    out = lax.map(expert, (buf.reshape(n_blocks, MOE_BLOCK, D), block_e))
    y_slot = out.reshape(n_blocks * MOE_BLOCK, D)[dest].reshape(T, TOP_K, D)
    y = jnp.einsum('tk,tkd->td', gates.astype(x.dtype), y_slot)
    return y.reshape(B, L, D)


def trunk(x, s_hg, s_ssm, s_conv, params):
    (hg_lower_bounds, w_in, hg_norm_w, hg_proj, conv_w, conv_b, dt_bias, a_log, d_skip,
     ssd_norm_w, ssd_proj, w_out, ln1_g, ln1_b, router_w, router_b, w_gu, b_gu, w_down, b_down,
     ln2_g, ln2_b) = params
    lb_all = jnp.cumsum(jax.nn.softmax(hg_lower_bounds.astype(jnp.float32), axis=0), axis=0)
    lb_all = lb_all - lb_all[0]
    new_hg, new_ssm, new_conv = [], [], []
    for l in range(DEPTH):
        mix, sh, ss, sc = mixer_block(x, s_hg[l], s_ssm[l], s_conv[l], lb_all[l], w_in[l], hg_norm_w[l], hg_proj[l],
                                      conv_w[l], conv_b[l], dt_bias[l], a_log[l], d_skip[l], ssd_norm_w[l],
                                      ssd_proj[l], w_out[l])
        h = layer_norm(ALPHA * x + mix, ln1_g[l], ln1_b[l])
        x = layer_norm(ALPHA * h + moe_ffn(h, router_w[l], router_b[l], w_gu[l], b_gu[l], w_down[l], b_down[l]),
                       ln2_g[l], ln2_b[l])
        new_hg.append(sh)
        new_ssm.append(ss)
        new_conv.append(sc)
    return x, jnp.stack(new_hg), jnp.stack(new_ssm), jnp.stack(new_conv)


def setup_inputs(seed: int = 0) -> dict:
    key = jax.random.key(seed)
    ks = jax.random.split(key, 32)
    f32 = jnp.float32

    def nrm(i, shape, scale):
        return scale * jax.random.normal(ks[i], shape, f32)

    dt0 = jnp.exp(jax.random.uniform(ks[9], (DEPTH, SSD_HEADS), f32, math.log(1e-3), math.log(1e-1)))
    return {
        'x_prompt': nrm(0, (BATCH, SEQ, D_MODEL), 1.0),
        'x_sample': nrm(1, (DEC_BATCH, DEC_SEQ, D_MODEL), 1.0),
        'state_hgrn': nrm(2, (DEPTH, DEC_BATCH, HG_HEADS, HG_DK, HG_DV), 0.5),
        'state_ssm': nrm(3, (DEPTH, DEC_BATCH, SSD_HEADS, SSD_HEADDIM, SSD_STATE), 0.1),
        'state_conv': nrm(4, (DEPTH, DEC_BATCH, CONV_K - 1, CONV_DIM), 1.0),
        'hg_lower_bounds': nrm(5, (DEPTH, HG_DIM), 0.5),
        'w_in': nrm(6, (DEPTH, D_MODEL, N_IN), D_MODEL ** -0.5),
        'hg_norm_w': 1.0 + nrm(7, (DEPTH, HG_DIM), 0.02),
        'hg_proj': nrm(8, (DEPTH, HG_DIM, D_MODEL), BETA * HG_DIM ** -0.5),
        'conv_w': nrm(10, (DEPTH, CONV_K, CONV_DIM), CONV_K ** -0.5),
        'conv_b': nrm(11, (DEPTH, CONV_DIM), 0.01),
        'dt_bias': dt0 + jnp.log(-jnp.expm1(-dt0)),
        'a_log': jnp.log(jax.random.uniform(ks[12], (DEPTH, SSD_HEADS), f32, 1.0, 16.0)),
        'd_skip': 1.0 + nrm(13, (DEPTH, SSD_HEADS), 0.1),
        'ssd_norm_w': 1.0 + nrm(14, (DEPTH, SSD_INNER), 0.02),
        'ssd_proj': nrm(15, (DEPTH, SSD_INNER, D_MODEL), BETA * SSD_INNER ** -0.5),
        'w_out': nrm(16, (DEPTH, D_MODEL, D_MODEL), BETA * D_MODEL ** -0.5),
        'ln1_g': 1.0 + nrm(17, (DEPTH, D_MODEL), 0.02),
        'ln1_b': nrm(18, (DEPTH, D_MODEL), 0.02),
        'router_w': nrm(19, (DEPTH, D_MODEL, N_EXPERTS), D_MODEL ** -0.5),
        'router_b': nrm(20, (DEPTH, N_EXPERTS), 0.01),
        'w_gu': nrm(21, (DEPTH, N_EXPERTS, D_MODEL, 2 * D_FF), D_MODEL ** -0.5),
        'b_gu': nrm(22, (DEPTH, N_EXPERTS, 2 * D_FF), 0.02),
        'w_down': nrm(23, (DEPTH, N_EXPERTS, D_FF, D_MODEL), BETA * D_FF ** -0.5),
        'b_down': nrm(24, (DEPTH, N_EXPERTS, D_MODEL), 0.02),
        'ln2_g': 1.0 + nrm(25, (DEPTH, D_MODEL), 0.02),
        'ln2_b': nrm(26, (DEPTH, D_MODEL), 0.02),
    }


def reference(x_prompt, x_sample, state_hgrn, state_ssm, state_conv, hg_lower_bounds, w_in, hg_norm_w, hg_proj,
              conv_w, conv_b, dt_bias, a_log, d_skip, ssd_norm_w, ssd_proj, w_out, ln1_g, ln1_b, router_w,
              router_b, w_gu, b_gu, w_down, b_down, ln2_g, ln2_b):
    params = (hg_lower_bounds, w_in, hg_norm_w, hg_proj, conv_w, conv_b, dt_bias, a_log, d_skip,
              ssd_norm_w, ssd_proj, w_out, ln1_g, ln1_b, router_w, router_b, w_gu, b_gu, w_down, b_down,
              ln2_g, ln2_b)
    bp = x_prompt.shape[0]
    zero_hg = jnp.zeros((DEPTH, bp) + state_hgrn.shape[2:], state_hgrn.dtype)
    zero_ssm = jnp.zeros((DEPTH, bp) + state_ssm.shape[2:], state_ssm.dtype)
    zero_conv = jnp.zeros((DEPTH, bp) + state_conv.shape[2:], x_prompt.dtype)
    y_prompt, hg_p, ssm_p, conv_p = trunk(x_prompt, zero_hg, zero_ssm, zero_conv, params)
    y_sample, hg_s, ssm_s, conv_s = trunk(x_sample, state_hgrn, state_ssm, state_conv, params)
    return (y_prompt, y_sample, hg_p, ssm_p, conv_p, hg_s, ssm_s, conv_s)
```

```python
import functools

import jax
import jax.numpy as jnp
from jax import lax
from jax.experimental import pallas as pl
from jax.experimental.pallas import tpu as pltpu

F32 = jnp.float32
BF16 = jnp.bfloat16

D_MODEL = 1024
DEPTH = 2
HG_HEADS = 8
HG_DK = 128
HG_DV = 128
HG_DIM = HG_HEADS * HG_DK
SSD_INNER = 2 * D_MODEL
SSD_HEADDIM = 64
SSD_HEADS = SSD_INNER // SSD_HEADDIM
SSD_GROUPS = 4
SSD_HPG = SSD_HEADS // SSD_GROUPS
SSD_STATE = 128
SSD_PAIRS = SSD_HEADS // 2
PAIRS_PER_GROUP = SSD_PAIRS // SSD_GROUPS
GROUP_W = SSD_INNER // SSD_GROUPS
CONV_K = 4
CONV_DIM = SSD_INNER + 2 * SSD_GROUPS * SSD_STATE
N_EXPERTS = 32
TOP_K = 4
D_FF = D_MODEL
SWIGLU_LIMIT = 7.0
SWIGLU_ALPHA = 1.702
ALPHA = (2.0 * DEPTH) ** 0.25
LN_EPS = 1e-5
RMS_EPS = 1e-6

LANES = 128
SUBLANES = 8

DT_W = SSD_GROUPS * LANES
OFF_Q = 0
OFF_F = HG_DIM
OFF_V = 2 * HG_DIM
OFF_G = 3 * HG_DIM
OFF_Z = 4 * HG_DIM
OFF_XBC = OFF_Z + SSD_INNER
OFF_GA = OFF_XBC + CONV_DIM
OFF_GB = OFF_GA + D_MODEL
OFF_DT = OFF_GB + D_MODEL
N_W = OFF_DT + DT_W
IN_SPLITS = (HG_DIM, HG_DIM, HG_DIM, HG_DIM, SSD_INNER, CONV_DIM, SSD_HEADS, D_MODEL, D_MODEL)

ROW_TILE = 128
MOE_BM = 256
VMEM_LIMIT = 48 * 1024 * 1024


def _cparams(sem):
    return pltpu.CompilerParams(dimension_semantics=sem, vmem_limit_bytes=VMEM_LIMIT)


def _sigmoid(x):
    return jax.nn.sigmoid(x)


def _dot(a, b):
    return jnp.dot(a, b, preferred_element_type=F32)


def _dot_nt(a, b):
    return lax.dot_general(a, b, (((1,), (1,)), ((), ())), preferred_element_type=F32)


def _dot_tn(a, b):
    return lax.dot_general(a, b, (((0,), (0,)), ((), ())), preferred_element_type=F32)


def _mm_kernel(x_ref, w_ref, o_ref):
    o_ref[...] = _dot(x_ref[...], w_ref[...])


def _matmul(x, w, tm, tn):
    m, k = x.shape
    n = w.shape[1]
    return pl.pallas_call(
        _mm_kernel,
        grid=(n // tn, m // tm),
        in_specs=[pl.BlockSpec((tm, k), lambda j, i: (i, 0)),
                  pl.BlockSpec((k, tn), lambda j, i: (0, j))],
        out_specs=pl.BlockSpec((tm, tn), lambda j, i: (i, j)),
        out_shape=jax.ShapeDtypeStruct((m, n), F32),
        compiler_params=_cparams(("arbitrary", "arbitrary")),
        name="in_proj",
    )(x, w)


def _conv_kernel(x_ref, pb_ref, p0_ref, w_ref, b_ref, o_ref):
    i = pl.program_id(1)
    x = x_ref[0]
    prev = jnp.where(i == 0, p0_ref[0], pb_ref[0])
    full = jnp.concatenate([prev, x], axis=0)
    w = w_ref[...]
    acc = b_ref[...] + w[CONV_K - 1:CONV_K] * x
    for s in range(1, CONV_K):
        xs = pltpu.roll(full, s, 0)[SUBLANES:]
        acc = acc + w[CONV_K - 1 - s:CONV_K - s] * xs
    o_ref[0] = acc * _sigmoid(acc)


def _conv(xw3, prev8, conv_w, conv_b, r):
    b, l, _ = xw3.shape
    cb = OFF_XBC // CONV_DIM
    rb = r // SUBLANES
    return pl.pallas_call(
        _conv_kernel,
        grid=(b, l // r),
        in_specs=[pl.BlockSpec((1, r, CONV_DIM), lambda bi, i: (bi, i, cb)),
                  pl.BlockSpec((1, SUBLANES, CONV_DIM), lambda bi, i: (bi, jnp.maximum(i * rb - 1, 0), cb)),
                  pl.BlockSpec((1, SUBLANES, CONV_DIM), lambda bi, i: (bi, 0, 0)),
                  pl.BlockSpec((CONV_K, CONV_DIM), lambda bi, i: (0, 0)),
                  pl.BlockSpec((1, CONV_DIM), lambda bi, i: (0, 0))],
        out_specs=pl.BlockSpec((1, r, CONV_DIM), lambda bi, i: (bi, i, 0)),
        out_shape=jax.ShapeDtypeStruct((b, l, CONV_DIM), F32),
        compiler_params=_cparams(("arbitrary", "arbitrary")),
        name="conv_silu",
    )(xw3, xw3, prev8, conv_w, conv_b)


def _seg_cumsum(x, rc, seg):
    s = 1
    while s < seg:
        x = x + jnp.where(rc >= s, pltpu.roll(x, s, 0), 0.0)
        s *= 2
    return x


def _pad_rows(x, rows):
    if x.shape[0] == rows:
        return x
    return jnp.concatenate([x, jnp.zeros((rows - x.shape[0], x.shape[1]), x.dtype)], axis=0)


def _hgrn_kernel(q_ref, f_ref, v_ref, g_ref, la_ref, l1m_ref, oml_ref, nw_ref, s0_ref,
                 o_ref, sout_ref, st_ref, acc_ref, *, ch, sub, nseq, valid, nc):
    r = ROW_TILE
    c = pl.program_id(2)

    @pl.when(c == 0)
    def _():
        for i in range(nseq):
            st_ref[i] = s0_ref[i, 0].T

    q = q_ref[...]
    fr = f_ref[...]
    v = v_ref[...]
    q = q * _sigmoid(q)
    la = la_ref[0]
    lsig = jnp.minimum(fr, 0.0) - jnp.log1p(jnp.exp(-jnp.abs(fr)))
    cc = l1m_ref[0] + lsig
    logf = jnp.maximum(la, cc) + jnp.log1p(jnp.exp(-jnp.abs(la - cc)))
    k = oml_ref[0] * _sigmoid(-fr)

    row = lax.broadcasted_iota(jnp.int32, (r, LANES), 0)
    rc = row & (ch - 1)
    if valid < ch:
        pad = rc >= valid
        logf = jnp.where(pad, 0.0, logf)
        k = jnp.where(pad, 0.0, k)
    b = _seg_cumsum(logf, rc, ch)

    rs = row & (sub - 1)
    o = jnp.sum(q * k, axis=-1, keepdims=True) * v
    for d in range(1, sub):
        kd = pltpu.roll(k, d, 0)
        bd = pltpu.roll(b, d, 0)
        vd = pltpu.roll(v, d, 0)
        p = jnp.where(rs >= d, q * kd * jnp.exp(b - bd), 0.0)
        o = o + jnp.sum(p, axis=-1, keepdims=True) * vd
    acc_ref[...] = o

    if ch > sub:
        col = lax.broadcasted_iota(jnp.int32, (sub, LANES), 1)
        vb = v.astype(BF16)
        for j in range(1, ch // sub):
            lo, hi = j * sub, (j + 1) * sub
            ref_b = b[lo - 1:lo]
            qj = q[lo:hi] * jnp.exp(b[lo:hi] - ref_b)
            kk = k * jnp.exp(jnp.minimum(ref_b - b, 0.0))
            sc = _dot_nt(qj.astype(BF16), kk.astype(BF16))
            sc = jnp.where(col < lo, sc, 0.0)
            acc_ref[lo:hi] += _dot(sc.astype(BF16), vb)

    for i in range(nseq):
        lo, hi = i * ch, (i + 1) * ch
        bi = b[lo:hi]
        bl = bi[ch - 1:ch]
        st = st_ref[i]
        qt = q[lo:hi] * jnp.exp(bi)
        acc_ref[lo:hi] += _dot_nt(qt.astype(BF16), st.astype(BF16))
        kh = _pad_rows(k[lo:hi] * jnp.exp(bl - bi), r)
        vi = _pad_rows(v[lo:hi], r)
        st_ref[i] = st * jnp.exp(bl) + _dot_tn(vi.astype(BF16), kh.astype(BF16))

    o = acc_ref[...]
    ms = jnp.mean(o * o, axis=-1, keepdims=True)
    o_ref[...] = o * lax.rsqrt(ms + RMS_EPS) * nw_ref[0] * _sigmoid(g_ref[...])

    @pl.when(c == nc - 1)
    def _():
        for i in range(nseq):
            sout_ref[i, 0] = st_ref[i].T


def _seq_tiling(bsz, l):
    r = ROW_TILE
    ch = min(l, r)
    nseq = r // ch
    nc = max(l // r, 1)
    nb = bsz * l // (r * nc)
    return ch, nseq, nc, nb


def _hgrn(xw, s0, la, l1m, oml, nw, bsz, l, valid):
    r = ROW_TILE
    ch, nseq, nc, nb = _seq_tiling(bsz, l)
    sub = min(ch, 16)
    kern = functools.partial(_hgrn_kernel, ch=ch, sub=sub, nseq=nseq, valid=valid, nc=nc)

    def xspec(off):
        return pl.BlockSpec((r, LANES), lambda bb, h, c: (bb * nc + c, off // LANES + h))

    def pspec():
        return pl.BlockSpec((1, 1, LANES), lambda bb, h, c: (h, 0, 0))

    sspec = pl.BlockSpec((nseq, 1, HG_DK, HG_DV), lambda bb, h, c: (bb, h, 0, 0))
    return pl.pallas_call(
        kern,
        grid=(nb, HG_HEADS, nc),
        in_specs=[xspec(OFF_Q), xspec(OFF_F), xspec(OFF_V), xspec(OFF_G),
                  pspec(), pspec(), pspec(), pspec(), sspec],
        out_specs=[pl.BlockSpec((r, LANES), lambda bb, h, c: (bb * nc + c, h)), sspec],
        out_shape=[jax.ShapeDtypeStruct((bsz * l, HG_DIM), F32),
                   jax.ShapeDtypeStruct(s0.shape, F32)],
        scratch_shapes=[pltpu.VMEM((nseq, HG_DV, HG_DK), F32), pltpu.VMEM((r, LANES), F32)],
        compiler_params=_cparams(("arbitrary", "arbitrary", "arbitrary")),
        name="hgrn2",
    )(xw, xw, xw, xw, la, l1m, oml, nw, s0)


def _ssd_kernel(xs_ref, bm_ref, cm_ref, z_ref, dt_ref, dtb_ref, a_ref, dsk_ref, nw_ref, h0_ref,
                y_ref, hout_ref, h_ref, *, ch, nseq, valid, nc):
    r = ROW_TILE
    c = pl.program_id(2)

    @pl.when(c == 0)
    def _():
        h_ref[...] = h0_ref[...]

    shift = ch.bit_length() - 1
    rowl = lax.broadcasted_iota(jnp.int32, (r, LANES), 0)
    lane = lax.broadcasted_iota(jnp.int32, (r, LANES), 1)
    rc = rowl & (ch - 1)
    dtr = dt_ref[...] + dtb_ref[...]
    dt = jnp.maximum(dtr, 0.0) + jnp.log1p(jnp.exp(-jnp.abs(dtr)))
    if valid < ch:
        dt = jnp.where(rc < valid, dt, 0.0)
    acum = _seg_cumsum(dt * a_ref[...], rc, ch)
    acum_t = acum.T
    dt_t = dt.T

    mask = (lane <= rowl) & ((rowl >> shift) == (lane >> shift))
    even_lane = lane < SSD_HEADDIM
    even_row = rowl < SSD_HEADDIM
    last_row = rc == ch - 1

    bmb = bm_ref[0].astype(BF16)
    cmb = cm_ref[0].astype(BF16)
    cb = _dot_nt(cmb, bmb)
    xs = xs_ref[0]
    z = z_ref[...]
    for pp in range(PAIRS_PER_GROUP):
        xp = xs[:, pp * LANES:(pp + 1) * LANES]
        acc = dsk_ref[:, pp * LANES:(pp + 1) * LANES] * xp
        for e in range(2):
            hd = 2 * pp + e
            w = cb * jnp.exp(jnp.where(mask, acum[:, hd:hd + 1] - acum_t[hd:hd + 1, :], -jnp.inf)) * dt_t[hd:hd + 1, :]
            xm = jnp.where(even_lane if e == 0 else jnp.logical_not(even_lane), xp, 0.0)
            acc = acc + _dot(w.astype(BF16), xm.astype(BF16))
        a_lane = jnp.where(even_lane, acum[:, 2 * pp:2 * pp + 1], acum[:, 2 * pp + 1:2 * pp + 2])
        dt_lane = jnp.where(even_lane, dt[:, 2 * pp:2 * pp + 1], dt[:, 2 * pp + 1:2 * pp + 2])
        ea = jnp.exp(a_lane)

        def seq_body(i, acc, pp=pp, xp=xp, a_lane=a_lane, dt_lane=dt_lane, ea=ea):
            in_seq = (rowl >> shift) == i
            al_lane = jnp.sum(jnp.where(in_seq & last_row, a_lane, 0.0), axis=0, keepdims=True)
            hp = h_ref[i, pp]
            yi = _dot_nt(cmb, hp.astype(BF16)) * ea
            acc = acc + jnp.where(in_seq, yi, 0.0)
            xw = jnp.where(in_seq, xp * (jnp.exp(al_lane - a_lane) * dt_lane), 0.0)
            dec = jnp.where(even_row, jnp.exp(al_lane[:, 0:1]), jnp.exp(al_lane[:, LANES - 1:LANES]))
            h_ref[i, pp] = dec * hp + _dot_tn(xw.astype(BF16), bmb)
            return acc

        if nseq == 1:
            acc = seq_body(0, acc)
        else:
            acc = lax.fori_loop(0, nseq, seq_body, acc)
        zz = z[:, pp * LANES:(pp + 1) * LANES]
        y_ref[:, pp * LANES:(pp + 1) * LANES] = acc * (zz * _sigmoid(zz))

    y = y_ref[...]
    ms = jnp.mean(y * y, axis=-1, keepdims=True)
    y_ref[...] = y * lax.rsqrt(ms + RMS_EPS) * nw_ref[...]

    @pl.when(c == nc - 1)
    def _():
        hout_ref[...] = h_ref[...]


def _ssd(xc3, xw, h0p, dtb, a, dsk, nw, bsz, l, valid):
    r = ROW_TILE
    ch, nseq, nc, nb = _seq_tiling(bsz, l)
    kern = functools.partial(_ssd_kernel, ch=ch, nseq=nseq, valid=valid, nc=nc)
    hspec = pl.BlockSpec((nseq, PAIRS_PER_GROUP, LANES, SSD_STATE), lambda bb, g, c: (bb, g, 0, 0))
    b_off = SSD_INNER // SSD_STATE
    c_off = b_off + SSD_GROUPS

    def vec(w):
        return pl.BlockSpec((1, w), lambda bb, g, c: (0, g))

    return pl.pallas_call(
        kern,
        grid=(nb, SSD_GROUPS, nc),
        in_specs=[pl.BlockSpec((1, r, GROUP_W), lambda bb, g, c: (bb * nc + c, 0, g)),
                  pl.BlockSpec((1, r, SSD_STATE), lambda bb, g, c: (bb * nc + c, 0, b_off + g)),
                  pl.BlockSpec((1, r, SSD_STATE), lambda bb, g, c: (bb * nc + c, 0, c_off + g)),
                  pl.BlockSpec((r, GROUP_W), lambda bb, g, c: (bb * nc + c, OFF_Z // GROUP_W + g)),
                  pl.BlockSpec((r, LANES), lambda bb, g, c: (bb * nc + c, OFF_DT // LANES + g)),
                  vec(LANES), vec(LANES), vec(GROUP_W), vec(GROUP_W), hspec],
        out_specs=[pl.BlockSpec((r, GROUP_W), lambda bb, g, c: (bb * nc + c, g)), hspec],
        out_shape=[jax.ShapeDtypeStruct((bsz * l, SSD_INNER), F32),
                   jax.ShapeDtypeStruct(h0p.shape, F32)],
        scratch_shapes=[pltpu.VMEM((nseq, PAIRS_PER_GROUP, LANES, SSD_STATE), F32)],
        compiler_params=_cparams(("arbitrary", "arbitrary", "arbitrary")),
        name="ssd",
    )(xc3, xc3, xc3, xw, xw, dtb, a, dsk, nw, h0p)


def _layer_norm(x, g, b):
    mu = jnp.mean(x, axis=-1, keepdims=True)
    xc = x - mu
    var = jnp.mean(xc * xc, axis=-1, keepdims=True)
    return xc * lax.rsqrt(var + LN_EPS) * g + b


def _postmix_kernel(o_ref, y_ref, ga_ref, gb_ref, x_ref, hgp_ref, ssp_ref, wo_ref, g_ref, b_ref,
                    rwh_ref, rwl_ref, rb_ref, h_ref, hb_ref, lg_ref):
    out_a = _dot(o_ref[...].astype(BF16), hgp_ref[...])
    out_b = _dot(y_ref[...].astype(BF16), ssp_ref[...])
    merged = _sigmoid(ga_ref[...]) * out_a + _sigmoid(gb_ref[...]) * out_b
    mix = _dot(merged.astype(BF16), wo_ref[...])
    h = _layer_norm(ALPHA * x_ref[...] + mix, g_ref[...], b_ref[...])
    h_ref[...] = h
    hh = h.astype(BF16)
    hb_ref[...] = hh
    hl = (h - hh.astype(F32)).astype(BF16)
    lg_ref[...] = (_dot(hh, rwh_ref[...]) + _dot(hl, rwh_ref[...]) + _dot(hh, rwl_ref[...])) + rb_ref[...]


def _postmix(o, y, xw, x, hgp, ssp, wo, g, b, rwh, rwl, rb, tm):
    t = x.shape[0]

    def full(a):
        return pl.BlockSpec(a.shape, lambda i: (0, 0))

    return pl.pallas_call(
        _postmix_kernel,
        grid=(t // tm,),
        in_specs=[pl.BlockSpec((tm, HG_DIM), lambda i: (i, 0)),
                  pl.BlockSpec((tm, SSD_INNER), lambda i: (i, 0)),
                  pl.BlockSpec((tm, D_MODEL), lambda i: (i, OFF_GA // D_MODEL)),
                  pl.BlockSpec((tm, D_MODEL), lambda i: (i, OFF_GB // D_MODEL)),
                  pl.BlockSpec((tm, D_MODEL), lambda i: (i, 0)),
                  full(hgp), full(ssp), full(wo), full(g), full(b), full(rwh), full(rwl), full(rb)],
        out_specs=[pl.BlockSpec((tm, D_MODEL), lambda i: (i, 0)),
                   pl.BlockSpec((tm, D_MODEL), lambda i: (i, 0)),
                   pl.BlockSpec((tm, LANES), lambda i: (i, 0))],
        out_shape=[jax.ShapeDtypeStruct((t, D_MODEL), F32),
                   jax.ShapeDtypeStruct((t, D_MODEL), BF16),
                   jax.ShapeDtypeStruct((t, LANES), F32)],
        compiler_params=_cparams(("arbitrary",)),
        name="postmix",
    )(o, y, xw, xw, x, hgp, ssp, wo, g, b, rwh, rwl, rb)


def _moe_kernel(be_ref, nu_ref, xb_ref, wgu_ref, bgu_ref, wd_ref, bd_ref, o_ref):
    j = pl.program_id(0)

    @pl.when(j < nu_ref[0])
    def _():
        gu = _dot(xb_ref[...], wgu_ref[0]) + bgu_ref[0]
        gate = jnp.minimum(gu[:, :D_FF], SWIGLU_LIMIT)
        up = jnp.clip(gu[:, D_FF:], -SWIGLU_LIMIT, SWIGLU_LIMIT)
        act = (up + 1.0) * gate * _sigmoid(SWIGLU_ALPHA * gate)
        o_ref[...] = _dot(act.astype(BF16), wd_ref[0]) + bd_ref[0]

    @pl.when(j >= nu_ref[0])
    def _():
        o_ref[...] = jnp.zeros_like(o_ref)


def _moe_experts(block_e, n_used, buf, wgu, bgu, wd, bd):
    rows = buf.shape[0]
    nblk = rows // MOE_BM
    grid_spec = pltpu.PrefetchScalarGridSpec(
        num_scalar_prefetch=2,
        grid=(nblk,),
        in_specs=[pl.BlockSpec((MOE_BM, D_MODEL), lambda j, be, nu: (j, 0)),
                  pl.BlockSpec((1, D_MODEL, 2 * D_FF), lambda j, be, nu: (be[j], 0, 0)),
                  pl.BlockSpec((1, 1, 2 * D_FF), lambda j, be, nu: (be[j], 0, 0)),
                  pl.BlockSpec((1, D_FF, D_MODEL), lambda j, be, nu: (be[j], 0, 0)),
                  pl.BlockSpec((1, 1, D_MODEL), lambda j, be, nu: (be[j], 0, 0))],
        out_specs=pl.BlockSpec((MOE_BM, D_MODEL), lambda j, be, nu: (j, 0)),
    )
    return pl.pallas_call(
        _moe_kernel,
        grid_spec=grid_spec,
        out_shape=jax.ShapeDtypeStruct((rows, D_MODEL), F32),
        compiler_params=_cparams(("arbitrary",)),
        name="moe_experts",
    )(block_e, n_used, buf, wgu, bgu, wd, bd)


def _combine_kernel(h_ref, ys_ref, gt_ref, g_ref, b_ref, x_ref, xb_ref):
    gt = gt_ref[...]
    y = gt[:, 0:1] * ys_ref[:, 0:D_MODEL]
    for kk in range(1, TOP_K):
        y = y + gt[:, kk:kk + 1] * ys_ref[:, kk * D_MODEL:(kk + 1) * D_MODEL]
    x = _layer_norm(ALPHA * h_ref[...] + y, g_ref[...], b_ref[...])
    x_ref[...] = x
    xb_ref[...] = x.astype(BF16)


def _combine(h, ys, gates, g, b, tm):
    t = h.shape[0]
    return pl.pallas_call(
        _combine_kernel,
        grid=(t // tm,),
        in_specs=[pl.BlockSpec((tm, D_MODEL), lambda i: (i, 0)),
                  pl.BlockSpec((tm, TOP_K * D_MODEL), lambda i: (i, 0)),
                  pl.BlockSpec((tm, TOP_K), lambda i: (i, 0)),
                  pl.BlockSpec((1, D_MODEL), lambda i: (0, 0)),
                  pl.BlockSpec((1, D_MODEL), lambda i: (0, 0))],
        out_specs=[pl.BlockSpec((tm, D_MODEL), lambda i: (i, 0)),
                   pl.BlockSpec((tm, D_MODEL), lambda i: (i, 0))],
        out_shape=[jax.ShapeDtypeStruct((t, D_MODEL), F32),
                   jax.ShapeDtypeStruct((t, D_MODEL), BF16)],
        compiler_params=_cparams(("arbitrary",)),
        name="combine_ln",
    )(h, ys, gates, g, b)


def _split_cols(a, sizes):
    out, off = [], 0
    for s in sizes:
        out.append(a[..., off:off + s])
        off += s
    return out


def _per_group(vec_heads):
    v = vec_heads.reshape(SSD_GROUPS, SSD_HPG)
    return jnp.pad(v, ((0, 0), (0, LANES - SSD_HPG))).reshape(1, DT_W)


def _prep_layer(p, l):
    wq, wf, wv, wg, wz, wxbc, wdt, wga, wgb = _split_cols(p["w_in"][l], IN_SPLITS)
    wdt = jnp.pad(wdt.reshape(D_MODEL, SSD_GROUPS, SSD_HPG), ((0, 0), (0, 0), (0, LANES - SSD_HPG)))
    w_in = jnp.concatenate([wq, wf, wv, wg, wz, wxbc, wga, wgb, wdt.reshape(D_MODEL, DT_W)], axis=1).astype(BF16)
    lb = p["lb_all"][l].reshape(HG_HEADS, 1, HG_DK)
    rw = jnp.pad(p["router_w"][l], ((0, 0), (0, LANES - N_EXPERTS)))
    rwh = rw.astype(BF16)
    rwl = (rw - rwh.astype(F32)).astype(BF16)
    return dict(
        w_in=w_in,
        la=jnp.log(lb), l1m=jnp.log1p(-lb), oml=1.0 - lb,
        hg_nw=p["hg_norm_w"][l].reshape(HG_HEADS, 1, HG_DV),
        hgp=p["hg_proj"][l].astype(BF16),
        conv_w=p["conv_w"][l], conv_b=p["conv_b"][l].reshape(1, CONV_DIM),
        dtb=_per_group(p["dt_bias"][l]),
        a=_per_group(-jnp.exp(p["a_log"][l].astype(F32))),
        dsk=jnp.repeat(p["d_skip"][l], SSD_HEADDIM).reshape(1, SSD_INNER),
        ssd_nw=p["ssd_norm_w"][l].reshape(1, SSD_INNER),
        ssp=p["ssd_proj"][l].astype(BF16),
        wo=p["w_out"][l].astype(BF16),
        ln1_g=p["ln1_g"][l].reshape(1, D_MODEL), ln1_b=p["ln1_b"][l].reshape(1, D_MODEL),
        rwh=rwh, rwl=rwl,
        rb=jnp.pad(p["router_b"][l], (0, LANES - N_EXPERTS)).reshape(1, LANES),
        wgu=p["w_gu"][l].astype(BF16), bgu=p["b_gu"][l].reshape(N_EXPERTS, 1, 2 * D_FF),
        wd=p["w_down"][l].astype(BF16), bd=p["b_down"][l].reshape(N_EXPERTS, 1, D_MODEL),
        ln2_g=p["ln2_g"][l].reshape(1, D_MODEL), ln2_b=p["ln2_b"][l].reshape(1, D_MODEL),
    )


def _moe(h, hb, logits, lp, tm):
    t = h.shape[0]
    s = t * TOP_K
    top_v, top_e = lax.top_k(logits[:, :N_EXPERTS], TOP_K)
    gates = jax.nn.softmax(top_v, axis=-1)
    e_slot = top_e.reshape(s).astype(jnp.int32)
    tok_slot = jnp.arange(s, dtype=jnp.int32) // TOP_K
    order = jnp.argsort(e_slot)
    e_sorted = e_slot[order]
    counts = jnp.bincount(e_slot, length=N_EXPERTS).astype(jnp.int32)
    start = jnp.cumsum(counts) - counts
    padded = (counts + MOE_BM - 1) // MOE_BM * MOE_BM
    pend = jnp.cumsum(padded)
    pstart = pend - padded
    dest_sorted = (pstart[e_sorted] + jnp.arange(s, dtype=jnp.int32) - start[e_sorted]).astype(jnp.int32)
    dest = jnp.zeros((s,), jnp.int32).at[order].set(dest_sorted)
    n_blocks = (s + N_EXPERTS * (MOE_BM - 1) + MOE_BM - 1) // MOE_BM
    buf = jnp.zeros((n_blocks * MOE_BM, D_MODEL), BF16).at[dest].set(hb[tok_slot])
    block_e = jnp.minimum(jnp.searchsorted(pend, jnp.arange(n_blocks, dtype=jnp.int32) * MOE_BM, side="right"),
                          N_EXPERTS - 1).astype(jnp.int32)
    n_used = (pend[-1:] // MOE_BM).astype(jnp.int32)
    out = _moe_experts(block_e, n_used, buf, lp["wgu"], lp["bgu"], lp["wd"], lp["bd"])
    ys = out[dest].reshape(t, TOP_K * D_MODEL)
    return _combine(h, ys, gates, lp["ln2_g"], lp["ln2_b"], tm)


def _trunk(x3, s_hg, s_ssm, s_conv, layers, valid):
    bsz, l, _ = x3.shape
    t = bsz * l
    tm = 256
    x = x3.reshape(t, D_MODEL)
    xb = x.astype(BF16)
    new_hg, new_ssm, new_conv = [], [], []
    for li, lp in enumerate(layers):
        xw = _matmul(xb, lp["w_in"], min(512, t), 512)
        xw3 = xw.reshape(bsz, l, N_W)
        prev8 = jnp.pad(s_conv[li], ((0, 0), (SUBLANES - (CONV_K - 1), 0), (0, 0)))
        xc = _conv(xw3, prev8, lp["conv_w"], lp["conv_b"], min(l, 256))
        new_conv.append(xw3[:, valid - (CONV_K - 1):valid, OFF_XBC:OFF_XBC + CONV_DIM])
        o, shg = _hgrn(xw, s_hg[li], lp["la"], lp["l1m"], lp["oml"], lp["hg_nw"], bsz, l, valid)
        h0p = s_ssm[li].reshape(bsz, SSD_PAIRS, LANES, SSD_STATE)
        y, hss = _ssd(xc.reshape(t // ROW_TILE, ROW_TILE, CONV_DIM), xw, h0p, lp["dtb"], lp["a"], lp["dsk"],
                      lp["ssd_nw"], bsz, l, valid)
        h, hb, logits = _postmix(o, y, xw, x, lp["hgp"], lp["ssp"], lp["wo"], lp["ln1_g"], lp["ln1_b"],
                                 lp["rwh"], lp["rwl"], lp["rb"], tm)
        x, xb = _moe(h, hb, logits, lp, tm)
        new_hg.append(shg)
        new_ssm.append(hss.reshape(s_ssm[li].shape))
    return x.reshape(bsz, l, D_MODEL), jnp.stack(new_hg), jnp.stack(new_ssm), jnp.stack(new_conv)


def kernel(x_prompt, x_sample, state_hgrn, state_ssm, state_conv, hg_lower_bounds, w_in, hg_norm_w, hg_proj,
           conv_w, conv_b, dt_bias, a_log, d_skip, ssd_norm_w, ssd_proj, w_out, ln1_g, ln1_b, router_w,
           router_b, w_gu, b_gu, w_down, b_down, ln2_g, ln2_b):
    lb_all = jnp.cumsum(jax.nn.softmax(hg_lower_bounds.astype(F32), axis=0), axis=0)
    lb_all = lb_all - lb_all[0]
    p = dict(lb_all=lb_all, w_in=w_in, hg_norm_w=hg_norm_w, hg_proj=hg_proj, conv_w=conv_w, conv_b=conv_b,
             dt_bias=dt_bias, a_log=a_log, d_skip=d_skip, ssd_norm_w=ssd_norm_w, ssd_proj=ssd_proj, w_out=w_out,
             ln1_g=ln1_g, ln1_b=ln1_b, router_w=router_w, router_b=router_b, w_gu=w_gu, b_gu=b_gu,
             w_down=w_down, b_down=b_down, ln2_g=ln2_g, ln2_b=ln2_b)
    layers = [_prep_layer(p, l) for l in range(DEPTH)]

    bp, lprompt, _ = x_prompt.shape
    ls = x_sample.shape[1]
    zero_hg = jnp.zeros((DEPTH, bp) + state_hgrn.shape[2:], F32)
    zero_ssm = jnp.zeros((DEPTH, bp) + state_ssm.shape[2:], F32)
    zero_conv = jnp.zeros((DEPTH, bp) + state_conv.shape[2:], F32)
    y_p, hg_p, ssm_p, conv_p = _trunk(x_prompt, zero_hg, zero_ssm, zero_conv, layers, lprompt)

    xs = jnp.pad(x_sample, ((0, 0), (0, SUBLANES - ls), (0, 0)))
    y_s, hg_s, ssm_s, conv_s = _trunk(xs, state_hgrn, state_ssm, state_conv, layers, ls)
    return (y_p, y_s[:, :ls], hg_p, ssm_p, conv_p, hg_s, ssm_s, conv_s)
```

```python
import functools

import jax
import jax.numpy as jnp
from jax import lax
from jax.experimental import pallas as pl
from jax.experimental.pallas import tpu as pltpu

F32 = jnp.float32
BF16 = jnp.bfloat16

D_MODEL = 1024
DEPTH = 2
HG_HEADS = 8
HG_DK = 128
HG_DV = 128
HG_DIM = HG_HEADS * HG_DK
SSD_INNER = 2 * D_MODEL
SSD_HEADDIM = 64
SSD_HEADS = SSD_INNER // SSD_HEADDIM
SSD_GROUPS = 4
SSD_HPG = SSD_HEADS // SSD_GROUPS
SSD_STATE = 128
SSD_PAIRS = SSD_HEADS // 2
PAIRS_PER_GROUP = SSD_PAIRS // SSD_GROUPS
GROUP_W = SSD_INNER // SSD_GROUPS
CONV_K = 4
CONV_DIM = SSD_INNER + 2 * SSD_GROUPS * SSD_STATE
N_EXPERTS = 32
TOP_K = 4
D_FF = D_MODEL
SWIGLU_LIMIT = 7.0
SWIGLU_ALPHA = 1.702
ALPHA = (2.0 * DEPTH) ** 0.25
LN_EPS = 1e-5
RMS_EPS = 1e-6

LANES = 128
SUBLANES = 8

DT_W = SSD_GROUPS * LANES
OFF_Q = 0
OFF_F = HG_DIM
OFF_V = 2 * HG_DIM
OFF_G = 3 * HG_DIM
OFF_Z = 4 * HG_DIM
OFF_XBC = OFF_Z + SSD_INNER
OFF_GA = OFF_XBC + CONV_DIM
OFF_GB = OFF_GA + D_MODEL
OFF_DT = OFF_GB + D_MODEL
N_W = OFF_DT + DT_W
IN_SPLITS = (HG_DIM, HG_DIM, HG_DIM, HG_DIM, SSD_INNER, CONV_DIM, SSD_HEADS, D_MODEL, D_MODEL)

ROW_TILE = 128
MOE_BM = 256
VMEM_LIMIT = 48 * 1024 * 1024
LOG2E = 1.4426950408889634
HALO = 16
HGRN_HEADS_PER_STEP = 4


def _cparams(sem):
    return pltpu.CompilerParams(dimension_semantics=sem, vmem_limit_bytes=VMEM_LIMIT)


def _sigmoid(x):
    return jax.nn.sigmoid(x)


def _dot(a, b):
    return jnp.dot(a, b, preferred_element_type=F32)


def _dot_nt(a, b):
    return lax.dot_general(a, b, (((1,), (1,)), ((), ())), preferred_element_type=F32)


def _dot_tn(a, b):
    return lax.dot_general(a, b, (((0,), (0,)), ((), ())), preferred_element_type=F32)


def _mm_kernel(x_ref, w_ref, o_ref):
    o_ref[...] = _dot(x_ref[...], w_ref[...])


def _matmul(x, w, tm, tn):
    m, k = x.shape
    n = w.shape[1]
    return pl.pallas_call(
        _mm_kernel,
        grid=(n // tn, m // tm),
        in_specs=[pl.BlockSpec((tm, k), lambda j, i: (i, 0)),
                  pl.BlockSpec((k, tn), lambda j, i: (0, j))],
        out_specs=pl.BlockSpec((tm, tn), lambda j, i: (i, j)),
        out_shape=jax.ShapeDtypeStruct((m, n), F32),
        compiler_params=_cparams(("arbitrary", "arbitrary")),
        name="in_proj",
    )(x, w)


def _conv_kernel(x_ref, pb_ref, p0_ref, w_ref, b_ref, o_ref):
    i = pl.program_id(1)
    x = x_ref[0]
    prev = jnp.where(i == 0, p0_ref[0], pb_ref[0])
    full = jnp.concatenate([prev, x], axis=0)
    w = w_ref[...]
    acc = b_ref[...] + w[CONV_K - 1:CONV_K] * x
    for s in range(1, CONV_K):
        xs = pltpu.roll(full, s, 0)[SUBLANES:]
        acc = acc + w[CONV_K - 1 - s:CONV_K - s] * xs
    o_ref[0] = acc * _sigmoid(acc)


def _conv(xw3, prev8, conv_w, conv_b, r):
    b, l, _ = xw3.shape
    cb = OFF_XBC // CONV_DIM
    rb = r // SUBLANES
    return pl.pallas_call(
        _conv_kernel,
        grid=(b, l // r),
        in_specs=[pl.BlockSpec((1, r, CONV_DIM), lambda bi, i: (bi, i, cb)),
                  pl.BlockSpec((1, SUBLANES, CONV_DIM), lambda bi, i: (bi, jnp.maximum(i * rb - 1, 0), cb)),
                  pl.BlockSpec((1, SUBLANES, CONV_DIM), lambda bi, i: (bi, 0, 0)),
                  pl.BlockSpec((CONV_K, CONV_DIM), lambda bi, i: (0, 0)),
                  pl.BlockSpec((1, CONV_DIM), lambda bi, i: (0, 0))],
        out_specs=pl.BlockSpec((1, r, CONV_DIM), lambda bi, i: (bi, i, 0)),
        out_shape=jax.ShapeDtypeStruct((b, l, CONV_DIM), F32),
        compiler_params=_cparams(("arbitrary", "arbitrary")),
        name="conv_silu",
    )(xw3, xw3, prev8, conv_w, conv_b)


def _seg_cumsum(x, rc, seg):
    s = 1
    while s < seg:
        x = x + jnp.where(rc >= s, pltpu.roll(x, s, 0), 0.0)
        s *= 2
    return x


def _pad_rows(x, rows):
    if x.shape[0] == rows:
        return x
    return jnp.concatenate([x, jnp.zeros((rows - x.shape[0], x.shape[1]), x.dtype)], axis=0)


def _split3(x):
    hi = x.astype(BF16)
    r1 = x - hi.astype(F32)
    mid = r1.astype(BF16)
    lo = (r1 - mid.astype(F32)).astype(BF16)
    return hi, mid, lo


def _hgrn_kernel(*refs, hps, nseq, nc, **kw):
    s0_ref, sout_ref, st_ref = refs[8], refs[10], refs[11]
    c = pl.program_id(2)

    @pl.when(c == 0)
    def _():
        for hh in range(hps):
            for i in range(nseq):
                st_ref[hh, i] = s0_ref[0, i, hh].T

    for hh in range(hps):
        _hgrn_head(hh, *refs, nseq=nseq, **kw)

    @pl.when(c == nc - 1)
    def _():
        for hh in range(hps):
            for i in range(nseq):
                sout_ref[i, hh] = st_ref[hh, i].T


def _hgrn_head(hh, q_ref, f_ref, v_ref, g_ref, la_ref, l1m_ref, oml_ref, nw_ref, s0_ref,
               o_ref, sout_ref, st_ref, k_scr, b_scr, *, ch, sub, nseq, valid):
    r = ROW_TILE
    cols = slice(hh * LANES, (hh + 1) * LANES)
    q = q_ref[:, cols]
    fr = f_ref[:, cols]
    v = v_ref[:, cols]
    q = q * _sigmoid(q)
    la = la_ref[hh]
    lsig = jnp.minimum(fr, 0.0) - jnp.log1p(jnp.exp(-jnp.abs(fr)))
    cc = l1m_ref[hh] + lsig
    logf = jnp.maximum(la, cc) + jnp.log1p(jnp.exp(-jnp.abs(la - cc)))
    k = oml_ref[hh] * _sigmoid(-fr)

    row = lax.broadcasted_iota(jnp.int32, (r, LANES), 0)
    lane = lax.broadcasted_iota(jnp.int32, (r, LANES), 1)
    if valid < ch:
        pad = (row & (ch - 1)) >= valid
        logf = jnp.where(pad, 0.0, logf)
        k = jnp.where(pad, 0.0, k)

    tri = lane <= row
    if ch < r:
        shift = ch.bit_length() - 1
        tri = tri & ((row >> shift) == (lane >> shift))
    tri = jnp.where(tri, 1.0, 0.0).astype(BF16)
    hi, mid, lo = _split3(logf)
    b2 = (_dot(tri, hi) + _dot(tri, mid) + _dot(tri, lo)) * LOG2E

    k_scr[hh, 0:HALO] = jnp.zeros((HALO, LANES), F32)
    b_scr[hh, 0:HALO] = jnp.zeros((HALO, LANES), F32)
    k_scr[hh, HALO:HALO + r] = k
    b_scr[hh, HALO:HALO + r] = b2
    rs = row & (sub - 1)
    scl = jnp.where(lane == rs, jnp.sum(q * k, axis=-1, keepdims=True), 0.0)
    for d in range(1, sub):
        kd = k_scr[hh, pl.ds(HALO - d, r), :]
        bd = b_scr[hh, pl.ds(HALO - d, r), :]
        sd = jnp.sum(q * kd * jnp.exp2(b2 - bd), axis=-1, keepdims=True)
        scl = jnp.where(lane == rs - d, sd, scl)

    spc = ch // sub
    pieces = []
    for m in range(r // sub):
        lo_, hi_ = m * sub, (m + 1) * sub
        piece = scl[lo_:hi_]
        if m:
            piece = pltpu.roll(piece, lo_, 1)
        j = m % spc
        if j:
            cs = (m - j) * sub
            ref_b = b2[lo_ - 1:lo_]
            qj = q[lo_:hi_] * jnp.exp2(b2[lo_:hi_] - ref_b)
            kk = k[cs:lo_] * jnp.exp2(ref_b - b2[cs:lo_])
            parts = ([jnp.zeros((cs, LANES), F32)] if cs else []) + [kk, jnp.zeros((r - lo_, LANES), F32)]
            piece = piece + _dot_nt(qj.astype(BF16), jnp.concatenate(parts, axis=0).astype(BF16))
        pieces.append(piece)
    sc = jnp.concatenate(pieces, axis=0)
    o = _dot(sc.astype(BF16), v.astype(BF16))

    inter = []
    for i in range(nseq):
        lo_, hi_ = i * ch, (i + 1) * ch
        bi = b2[lo_:hi_]
        bl = bi[ch - 1:ch]
        st = st_ref[hh, i]
        inter.append(_dot_nt((q[lo_:hi_] * jnp.exp2(bi)).astype(BF16), st.astype(BF16)))
        kh = _pad_rows(k[lo_:hi_] * jnp.exp2(bl - bi), r)
        vi = _pad_rows(v[lo_:hi_], r)
        st_ref[hh, i] = st * jnp.exp2(bl) + _dot_tn(vi.astype(BF16), kh.astype(BF16))
    o = o + (inter[0] if nseq == 1 else jnp.concatenate(inter, axis=0))

    ms = jnp.mean(o * o, axis=-1, keepdims=True)
    o_ref[:, cols] = o * lax.rsqrt(ms + RMS_EPS) * nw_ref[hh] * _sigmoid(g_ref[:, cols])


def _seq_tiling(bsz, l):
    r = ROW_TILE
    ch = min(l, r)
    nseq = r // ch
    nc = max(l // r, 1)
    nb = bsz * l // (r * nc)
    return ch, nseq, nc, nb


def _hgrn(xw, s0_all, li, la, l1m, oml, nw, bsz, l, valid):
    r = ROW_TILE
    ch, nseq, nc, nb = _seq_tiling(bsz, l)
    sub = min(ch, HALO)
    hps = HGRN_HEADS_PER_STEP
    w = hps * LANES
    kern = functools.partial(_hgrn_kernel, hps=hps, ch=ch, sub=sub, nseq=nseq, valid=valid, nc=nc)

    def xspec(off):
        return pl.BlockSpec((r, w), lambda bb, h, c: (bb * nc + c, off // w + h))

    def pspec():
        return pl.BlockSpec((hps, 1, LANES), lambda bb, h, c: (h, 0, 0))

    return pl.pallas_call(
        kern,
        grid=(nb, HG_HEADS // hps, nc),
        in_specs=[xspec(OFF_Q), xspec(OFF_F), xspec(OFF_V), xspec(OFF_G),
                  pspec(), pspec(), pspec(), pspec(),
                  pl.BlockSpec((1, nseq, hps, HG_DK, HG_DV), lambda bb, h, c: (li, bb, h, 0, 0))],
        out_specs=[pl.BlockSpec((r, w), lambda bb, h, c: (bb * nc + c, h)),
                   pl.BlockSpec((nseq, hps, HG_DK, HG_DV), lambda bb, h, c: (bb, h, 0, 0))],
        out_shape=[jax.ShapeDtypeStruct((bsz * l, HG_DIM), F32),
                   jax.ShapeDtypeStruct(s0_all.shape[1:], F32)],
        scratch_shapes=[pltpu.VMEM((hps, nseq, HG_DV, HG_DK), F32),
                        pltpu.VMEM((hps, HALO + r, LANES), F32), pltpu.VMEM((hps, HALO + r, LANES), F32)],
        compiler_params=_cparams(("arbitrary", "arbitrary", "arbitrary")),
        name="hgrn2",
    )(xw, xw, xw, xw, la, l1m, oml, nw, s0_all)


def _ssd_kernel(xs_ref, bm_ref, cm_ref, z_ref, dt_ref, dtb_ref, a_ref, dsk_ref, nw_ref, h0_ref,
                y_ref, hout_ref, h_ref, *, ch, nseq, valid, nc):
    r = ROW_TILE
    c = pl.program_id(2)

    @pl.when(c == 0)
    def _():
        h_ref[...] = h0_ref[0]

    shift = ch.bit_length() - 1
    rowl = lax.broadcasted_iota(jnp.int32, (r, LANES), 0)
    lane = lax.broadcasted_iota(jnp.int32, (r, LANES), 1)
    rc = rowl & (ch - 1)
    dtr = dt_ref[...] + dtb_ref[...]
    dt = jnp.maximum(dtr, 0.0) + jnp.log1p(jnp.exp(-jnp.abs(dtr)))
    if valid < ch:
        dt = jnp.where(rc < valid, dt, 0.0)
    acum = _seg_cumsum(dt * a_ref[...], rc, ch)
    acum_t = acum.T
    dt_t = dt.T

    mask = (lane <= rowl) & ((rowl >> shift) == (lane >> shift))
    even_lane = lane < SSD_HEADDIM
    even_row = rowl < SSD_HEADDIM
    last_row = rc == ch - 1

    bmb = bm_ref[0].astype(BF16)
    cmb = cm_ref[0].astype(BF16)
    cb = _dot_nt(cmb, bmb)
    xs = xs_ref[0]
    z = z_ref[...]
    for pp in range(PAIRS_PER_GROUP):
        xp = xs[:, pp * LANES:(pp + 1) * LANES]
        acc = dsk_ref[:, pp * LANES:(pp + 1) * LANES] * xp
        for e in range(2):
            hd = 2 * pp + e
            w = cb * jnp.exp(jnp.where(mask, acum[:, hd:hd + 1] - acum_t[hd:hd + 1, :], -jnp.inf)) * dt_t[hd:hd + 1, :]
            xm = jnp.where(even_lane if e == 0 else jnp.logical_not(even_lane), xp, 0.0)
            acc = acc + _dot(w.astype(BF16), xm.astype(BF16))
        a_lane = jnp.where(even_lane, acum[:, 2 * pp:2 * pp + 1], acum[:, 2 * pp + 1:2 * pp + 2])
        dt_lane = jnp.where(even_lane, dt[:, 2 * pp:2 * pp + 1], dt[:, 2 * pp + 1:2 * pp + 2])
        ea = jnp.exp(a_lane)

        def seq_body(i, acc, pp=pp, xp=xp, a_lane=a_lane, dt_lane=dt_lane, ea=ea):
            in_seq = (rowl >> shift) == i
            al_lane = jnp.sum(jnp.where(in_seq & last_row, a_lane, 0.0), axis=0, keepdims=True)
            hp = h_ref[i, pp]
            yi = _dot_nt(cmb, hp.astype(BF16)) * ea
            acc = acc + jnp.where(in_seq, yi, 0.0)
            xw = jnp.where(in_seq, xp * (jnp.exp(al_lane - a_lane) * dt_lane), 0.0)
            dec = jnp.where(even_row, jnp.exp(al_lane[:, 0:1]), jnp.exp(al_lane[:, LANES - 1:LANES]))
            h_ref[i, pp] = dec * hp + _dot_tn(xw.astype(BF16), bmb)
            return acc

        if nseq == 1:
            acc = seq_body(0, acc)
        else:
            acc = lax.fori_loop(0, nseq, seq_body, acc)
        zz = z[:, pp * LANES:(pp + 1) * LANES]
        y_ref[:, pp * LANES:(pp + 1) * LANES] = acc * (zz * _sigmoid(zz))

    y = y_ref[...]
    ms = jnp.mean(y * y, axis=-1, keepdims=True)
    y_ref[...] = y * lax.rsqrt(ms + RMS_EPS) * nw_ref[...]

    @pl.when(c == nc - 1)
    def _():
        hout_ref[...] = h_ref[...]


def _ssd(xc3, xw, h0_all, li, dtb, a, dsk, nw, bsz, l, valid):
    r = ROW_TILE
    ch, nseq, nc, nb = _seq_tiling(bsz, l)
    kern = functools.partial(_ssd_kernel, ch=ch, nseq=nseq, valid=valid, nc=nc)
    hspec = pl.BlockSpec((nseq, PAIRS_PER_GROUP, LANES, SSD_STATE), lambda bb, g, c: (bb, g, 0, 0))
    h0spec = pl.BlockSpec((1, nseq, PAIRS_PER_GROUP, LANES, SSD_STATE), lambda bb, g, c: (li, bb, g, 0, 0))
    b_off = SSD_INNER // SSD_STATE
    c_off = b_off + SSD_GROUPS

    def vec(w):
        return pl.BlockSpec((1, w), lambda bb, g, c: (0, g))

    return pl.pallas_call(
        kern,
        grid=(nb, SSD_GROUPS, nc),
        in_specs=[pl.BlockSpec((1, r, GROUP_W), lambda bb, g, c: (bb * nc + c, 0, g)),
                  pl.BlockSpec((1, r, SSD_STATE), lambda bb, g, c: (bb * nc + c, 0, b_off + g)),
                  pl.BlockSpec((1, r, SSD_STATE), lambda bb, g, c: (bb * nc + c, 0, c_off + g)),
                  pl.BlockSpec((r, GROUP_W), lambda bb, g, c: (bb * nc + c, OFF_Z // GROUP_W + g)),
                  pl.BlockSpec((r, LANES), lambda bb, g, c: (bb * nc + c, OFF_DT // LANES + g)),
                  vec(LANES), vec(LANES), vec(GROUP_W), vec(GROUP_W), h0spec],
        out_specs=[pl.BlockSpec((r, GROUP_W), lambda bb, g, c: (bb * nc + c, g)), hspec],
        out_shape=[jax.ShapeDtypeStruct((bsz * l, SSD_INNER), F32),
                   jax.ShapeDtypeStruct(h0_all.shape[1:], F32)],
        scratch_shapes=[pltpu.VMEM((nseq, PAIRS_PER_GROUP, LANES, SSD_STATE), F32)],
        compiler_params=_cparams(("arbitrary", "arbitrary", "arbitrary")),
        name="ssd",
    )(xc3, xc3, xc3, xw, xw, dtb, a, dsk, nw, h0_all)


def _layer_norm(x, g, b):
    mu = jnp.mean(x, axis=-1, keepdims=True)
    xc = x - mu
    var = jnp.mean(xc * xc, axis=-1, keepdims=True)
    return xc * lax.rsqrt(var + LN_EPS) * g + b


def _postmix_kernel(o_ref, y_ref, ga_ref, gb_ref, x_ref, hgp_ref, ssp_ref, wo_ref, g_ref, b_ref,
                    rwh_ref, rwl_ref, rb_ref, h_ref, lg_ref):
    out_a = _dot(o_ref[...].astype(BF16), hgp_ref[...])
    out_b = _dot(y_ref[...].astype(BF16), ssp_ref[...])
    merged = _sigmoid(ga_ref[...]) * out_a + _sigmoid(gb_ref[...]) * out_b
    mix = _dot(merged.astype(BF16), wo_ref[...])
    h = _layer_norm(ALPHA * x_ref[...] + mix, g_ref[...], b_ref[...])
    h_ref[...] = h
    hh = h.astype(BF16)
    hl = (h - hh.astype(F32)).astype(BF16)
    lg_ref[...] = (_dot(hh, rwh_ref[...]) + _dot(hl, rwh_ref[...]) + _dot(hh, rwl_ref[...])) + rb_ref[...]


def _postmix(o, y, xw, x, hgp, ssp, wo, g, b, rwh, rwl, rb, tm):
    t = x.shape[0]

    def full(a):
        return pl.BlockSpec(a.shape, lambda i: (0, 0))

    return pl.pallas_call(
        _postmix_kernel,
        grid=(t // tm,),
        in_specs=[pl.BlockSpec((tm, HG_DIM), lambda i: (i, 0)),
                  pl.BlockSpec((tm, SSD_INNER), lambda i: (i, 0)),
                  pl.BlockSpec((tm, D_MODEL), lambda i: (i, OFF_GA // D_MODEL)),
                  pl.BlockSpec((tm, D_MODEL), lambda i: (i, OFF_GB // D_MODEL)),
                  pl.BlockSpec((tm, D_MODEL), lambda i: (i, 0)),
                  full(hgp), full(ssp), full(wo), full(g), full(b), full(rwh), full(rwl), full(rb)],
        out_specs=[pl.BlockSpec((tm, D_MODEL), lambda i: (i, 0)),
                   pl.BlockSpec((tm, LANES), lambda i: (i, 0))],
        out_shape=[jax.ShapeDtypeStruct((t, D_MODEL), F32),
                   jax.ShapeDtypeStruct((t, LANES), F32)],
        compiler_params=_cparams(("arbitrary",)),
        name="postmix",
    )(o, y, xw, xw, x, hgp, ssp, wo, g, b, rwh, rwl, rb)


def _moe_kernel(be_ref, nu_ref, tokc_ref, tokn_ref, dst_ref, h_hbm, wgu_ref, bgu_ref, wd_ref, bd_ref,
                ys_hbm, xbuf, obuf, gsem, ssem, *, nblk, trash_row):
    j = pl.program_id(0)
    nu = nu_ref[0]
    slot = j & 1

    def gather(tok_ref, s):
        def body(i, carry):
            t = tok_ref[0, 0, i]
            pltpu.make_async_copy(h_hbm.at[pl.ds(t, 1)], xbuf.at[s, pl.ds(i, 1)], gsem.at[s]).start()
            return carry
        lax.fori_loop(0, MOE_BM, body, 0, unroll=8)

    def scatter_copy(s, i, row):
        return pltpu.make_async_copy(obuf.at[s, pl.ds(i, 1)], ys_hbm.at[pl.ds(row, 1)], ssem.at[s])

    def block_wait(s):
        pltpu.make_async_copy(obuf.at[s], ys_hbm.at[pl.ds(0, MOE_BM)], ssem.at[s]).wait()

    @pl.when(j == 0)
    def _():
        obuf[1] = jnp.zeros((MOE_BM, D_MODEL), F32)
        for half in range(2):
            zero_fill = pltpu.make_async_copy(obuf.at[1], ys_hbm.at[pl.ds(trash_row + half * MOE_BM, MOE_BM)],
                                              ssem.at[1])
            zero_fill.start()
            zero_fill.wait()
        gather(tokc_ref, 0)

    @pl.when(j + 1 < nu)
    def _():
        gather(tokn_ref, 1 - slot)

    @pl.when(j < nu)
    def _():
        pltpu.make_async_copy(h_hbm.at[pl.ds(0, MOE_BM)], xbuf.at[slot], gsem.at[slot]).wait()

        @pl.when(j >= 2)
        def _():
            block_wait(slot)

        gu = _dot(xbuf[slot].astype(BF16), wgu_ref[0, 0]) + bgu_ref[0, 0]
        gate = jnp.minimum(gu[:, :D_FF], SWIGLU_LIMIT)
        up = jnp.clip(gu[:, D_FF:], -SWIGLU_LIMIT, SWIGLU_LIMIT)
        act = (up + 1.0) * gate * _sigmoid(SWIGLU_ALPHA * gate)
        obuf[slot] = _dot(act.astype(BF16), wd_ref[0, 0]) + bd_ref[0, 0]

        def body(i, carry):
            scatter_copy(slot, i, dst_ref[0, 0, i]).start()
            return carry
        lax.fori_loop(0, MOE_BM, body, 0, unroll=8)

    @pl.when(j == nblk - 1)
    def _():
        block_wait((nu - 1) & 1)

        @pl.when(nu >= 2)
        def _():
            block_wait(nu & 1)


def _moe_experts(block_e, n_used, row_tok, row_dst, h, wgu, bgu, wd, bd, li):
    t = h.shape[0]
    nblk = row_tok.shape[0]
    trash_row = TOP_K * t
    kern = functools.partial(_moe_kernel, nblk=nblk, trash_row=trash_row)

    def smem(imap):
        return pl.BlockSpec((1, 1, MOE_BM), imap, memory_space=pltpu.SMEM)

    grid_spec = pltpu.PrefetchScalarGridSpec(
        num_scalar_prefetch=2,
        grid=(nblk,),
        in_specs=[smem(lambda j, be, nu: (j, 0, 0)),
                  smem(lambda j, be, nu: (jnp.minimum(j + 1, nblk - 1), 0, 0)),
                  smem(lambda j, be, nu: (j, 0, 0)),
                  pl.BlockSpec(memory_space=pl.ANY),
                  pl.BlockSpec((1, 1, D_MODEL, 2 * D_FF), lambda j, be, nu: (li, be[j], 0, 0)),
                  pl.BlockSpec((1, 1, 1, 2 * D_FF), lambda j, be, nu: (li, be[j], 0, 0)),
                  pl.BlockSpec((1, 1, D_FF, D_MODEL), lambda j, be, nu: (li, be[j], 0, 0)),
                  pl.BlockSpec((1, 1, 1, D_MODEL), lambda j, be, nu: (li, be[j], 0, 0))],
        out_specs=pl.BlockSpec(memory_space=pl.ANY),
        scratch_shapes=[pltpu.VMEM((2, MOE_BM, D_MODEL), F32), pltpu.VMEM((2, MOE_BM, D_MODEL), F32),
                        pltpu.SemaphoreType.DMA((2,)), pltpu.SemaphoreType.DMA((2,))],
    )
    return pl.pallas_call(
        kern,
        grid_spec=grid_spec,
        out_shape=jax.ShapeDtypeStruct((trash_row + 2 * MOE_BM, D_MODEL), F32),
        compiler_params=_cparams(("arbitrary",)),
        name="moe_experts",
    )(block_e, n_used, row_tok, row_tok, row_dst, h, wgu, bgu, wd, bd)


def _combine_kernel(h_ref, y0_ref, y1_ref, y2_ref, y3_ref, gt_ref, g_ref, b_ref, x_ref, xb_ref):
    gt = gt_ref[...]
    y = gt[:, 0:1] * y0_ref[...]
    for kk, y_ref in enumerate((y1_ref, y2_ref, y3_ref), start=1):
        y = y + gt[:, kk:kk + 1] * y_ref[...]
    x = _layer_norm(ALPHA * h_ref[...] + y, g_ref[...], b_ref[...])
    x_ref[...] = x
    xb_ref[...] = x.astype(BF16)


def _combine(h, ys, gates, g, b, tm):
    t = h.shape[0]
    nt = t // tm

    def yspec(kk):
        return pl.BlockSpec((tm, D_MODEL), lambda i: (kk * nt + i, 0))

    return pl.pallas_call(
        _combine_kernel,
        grid=(nt,),
        in_specs=[pl.BlockSpec((tm, D_MODEL), lambda i: (i, 0)),
                  yspec(0), yspec(1), yspec(2), yspec(3),
                  pl.BlockSpec((tm, TOP_K), lambda i: (i, 0)),
                  pl.BlockSpec((1, D_MODEL), lambda i: (0, 0)),
                  pl.BlockSpec((1, D_MODEL), lambda i: (0, 0))],
        out_specs=[pl.BlockSpec((tm, D_MODEL), lambda i: (i, 0)),
                   pl.BlockSpec((tm, D_MODEL), lambda i: (i, 0))],
        out_shape=[jax.ShapeDtypeStruct((t, D_MODEL), F32),
                   jax.ShapeDtypeStruct((t, D_MODEL), BF16)],
        compiler_params=_cparams(("arbitrary",)),
        name="combine_ln",
    )(h, ys, ys, ys, ys, gates, g, b)


def _split_cols(a, sizes):
    out, off = [], 0
    for s in sizes:
        out.append(a[..., off:off + s])
        off += s
    return out


def _per_group(vec_heads):
    v = vec_heads.reshape(SSD_GROUPS, SSD_HPG)
    return jnp.pad(v, ((0, 0), (0, LANES - SSD_HPG))).reshape(1, DT_W)


def _prep_layer(p, l):
    wq, wf, wv, wg, wz, wxbc, wdt, wga, wgb = _split_cols(p["w_in"][l], IN_SPLITS)
    wdt = jnp.pad(wdt.reshape(D_MODEL, SSD_GROUPS, SSD_HPG), ((0, 0), (0, 0), (0, LANES - SSD_HPG)))
    w_in = jnp.concatenate([wq, wf, wv, wg, wz, wxbc, wga, wgb, wdt.reshape(D_MODEL, DT_W)], axis=1).astype(BF16)
    lb = p["lb_all"][l].reshape(HG_HEADS, 1, HG_DK)
    rw = jnp.pad(p["router_w"][l], ((0, 0), (0, LANES - N_EXPERTS)))
    rwh = rw.astype(BF16)
    rwl = (rw - rwh.astype(F32)).astype(BF16)
    return dict(
        w_in=w_in,
        la=jnp.log(lb), l1m=jnp.log1p(-lb), oml=1.0 - lb,
        hg_nw=p["hg_norm_w"][l].reshape(HG_HEADS, 1, HG_DV),
        hgp=p["hg_proj"][l].astype(BF16),
        conv_w=p["conv_w"][l], conv_b=p["conv_b"][l].reshape(1, CONV_DIM),
        dtb=_per_group(p["dt_bias"][l]),
        a=_per_group(-jnp.exp(p["a_log"][l].astype(F32))),
        dsk=jnp.repeat(p["d_skip"][l], SSD_HEADDIM).reshape(1, SSD_INNER),
        ssd_nw=p["ssd_norm_w"][l].reshape(1, SSD_INNER),
        ssp=p["ssd_proj"][l].astype(BF16),
        wo=p["w_out"][l].astype(BF16),
        ln1_g=p["ln1_g"][l].reshape(1, D_MODEL), ln1_b=p["ln1_b"][l].reshape(1, D_MODEL),
        rwh=rwh, rwl=rwl,
        rb=jnp.pad(p["router_b"][l], (0, LANES - N_EXPERTS)).reshape(1, LANES),
        ln2_g=p["ln2_g"][l].reshape(1, D_MODEL), ln2_b=p["ln2_b"][l].reshape(1, D_MODEL),
    )


def _prep_experts(p):
    return dict(
        wgu=p["w_gu"].astype(BF16), bgu=p["b_gu"].reshape(DEPTH, N_EXPERTS, 1, 2 * D_FF),
        wd=p["w_down"].astype(BF16), bd=p["b_down"].reshape(DEPTH, N_EXPERTS, 1, D_MODEL),
    )


def _moe(h, logits, lp, ep, li, tm):
    t = h.shape[0]
    s = t * TOP_K
    top_v, top_e = lax.top_k(logits[:, :N_EXPERTS], TOP_K)
    gates = jax.nn.softmax(top_v, axis=-1)
    e_slot = top_e.reshape(s).astype(jnp.int32)
    order = jnp.argsort(e_slot).astype(jnp.int32)
    e_sorted = e_slot[order]
    counts = jnp.bincount(e_slot, length=N_EXPERTS).astype(jnp.int32)
    start = jnp.cumsum(counts) - counts
    padded = (counts + MOE_BM - 1) // MOE_BM * MOE_BM
    pend = jnp.cumsum(padded)
    pstart = pend - padded
    dest_sorted = (pstart[e_sorted] + jnp.arange(s, dtype=jnp.int32) - start[e_sorted]).astype(jnp.int32)
    n_blocks = (s + N_EXPERTS * (MOE_BM - 1) + MOE_BM - 1) // MOE_BM
    nrows = n_blocks * MOE_BM
    tok_sorted = order // TOP_K
    k_sorted = order - tok_sorted * TOP_K
    row_tok = jnp.zeros((nrows,), jnp.int32).at[dest_sorted].set(tok_sorted)
    pad_dst = TOP_K * t + (jnp.arange(nrows, dtype=jnp.int32) % (2 * MOE_BM))
    row_dst = pad_dst.at[dest_sorted].set(k_sorted * t + tok_sorted)
    block_e = jnp.minimum(jnp.searchsorted(pend, jnp.arange(n_blocks, dtype=jnp.int32) * MOE_BM, side="right"),
                          N_EXPERTS - 1).astype(jnp.int32)
    n_used = (pend[-1:] // MOE_BM).astype(jnp.int32)
    ys = _moe_experts(block_e, n_used, row_tok.reshape(n_blocks, 1, MOE_BM), row_dst.reshape(n_blocks, 1, MOE_BM),
                      h, ep["wgu"], ep["bgu"], ep["wd"], ep["bd"], li)
    return _combine(h, ys, gates, lp["ln2_g"], lp["ln2_b"], tm)


def _trunk(x3, s_hg, s_ssm, s_conv, layers, ep, valid):
    bsz, l, _ = x3.shape
    t = bsz * l
    tm = 256
    x = x3.reshape(t, D_MODEL)
    xb = x.astype(BF16)
    s_ssm_p = s_ssm.reshape(DEPTH, bsz, SSD_PAIRS, LANES, SSD_STATE)
    new_hg, new_ssm, new_conv = [], [], []
    for li, lp in enumerate(layers):
        xw = _matmul(xb, lp["w_in"], min(512, t), N_W // 4)
        xw3 = xw.reshape(bsz, l, N_W)
        prev8 = jnp.pad(s_conv[li], ((0, 0), (SUBLANES - (CONV_K - 1), 0), (0, 0)))
        xc = _conv(xw3, prev8, lp["conv_w"], lp["conv_b"], min(l, 256))
        new_conv.append(xw3[:, valid - (CONV_K - 1):valid, OFF_XBC:OFF_XBC + CONV_DIM])
        o, shg = _hgrn(xw, s_hg, li, lp["la"], lp["l1m"], lp["oml"], lp["hg_nw"], bsz, l, valid)
        y, hss = _ssd(xc.reshape(t // ROW_TILE, ROW_TILE, CONV_DIM), xw, s_ssm_p, li, lp["dtb"], lp["a"], lp["dsk"],
                      lp["ssd_nw"], bsz, l, valid)
        h, logits = _postmix(o, y, xw, x, lp["hgp"], lp["ssp"], lp["wo"], lp["ln1_g"], lp["ln1_b"],
                             lp["rwh"], lp["rwl"], lp["rb"], tm)
        x, xb = _moe(h, logits, lp, ep, li, tm)
        new_hg.append(shg)
        new_ssm.append(hss.reshape(s_ssm.shape[1:]))
    return x.reshape(bsz, l, D_MODEL), jnp.stack(new_hg), jnp.stack(new_ssm), jnp.stack(new_conv)


def kernel(x_prompt, x_sample, state_hgrn, state_ssm, state_conv, hg_lower_bounds, w_in, hg_norm_w, hg_proj,
           conv_w, conv_b, dt_bias, a_log, d_skip, ssd_norm_w, ssd_proj, w_out, ln1_g, ln1_b, router_w,
           router_b, w_gu, b_gu, w_down, b_down, ln2_g, ln2_b):
    lb_all = jnp.cumsum(jax.nn.softmax(hg_lower_bounds.astype(F32), axis=0), axis=0)
    lb_all = lb_all - lb_all[0]
    p = dict(lb_all=lb_all, w_in=w_in, hg_norm_w=hg_norm_w, hg_proj=hg_proj, conv_w=conv_w, conv_b=conv_b,
             dt_bias=dt_bias, a_log=a_log, d_skip=d_skip, ssd_norm_w=ssd_norm_w, ssd_proj=ssd_proj, w_out=w_out,
             ln1_g=ln1_g, ln1_b=ln1_b, router_w=router_w, router_b=router_b, w_gu=w_gu, b_gu=b_gu,
             w_down=w_down, b_down=b_down, ln2_g=ln2_g, ln2_b=ln2_b)
    layers = [_prep_layer(p, l) for l in range(DEPTH)]
    ep = _prep_experts(p)

    bp, lprompt, _ = x_prompt.shape
    ls = x_sample.shape[1]
    zero_hg = jnp.zeros((DEPTH, bp) + state_hgrn.shape[2:], F32)
    zero_ssm = jnp.zeros((DEPTH, bp) + state_ssm.shape[2:], F32)
    zero_conv = jnp.zeros((DEPTH, bp) + state_conv.shape[2:], F32)
    y_p, hg_p, ssm_p, conv_p = _trunk(x_prompt, zero_hg, zero_ssm, zero_conv, layers, ep, lprompt)

    xs = jnp.pad(x_sample, ((0, 0), (0, SUBLANES - ls), (0, 0)))
    y_s, hg_s, ssm_s, conv_s = _trunk(xs, state_hgrn, state_ssm, state_conv, layers, ep, ls)
    return (y_p, y_s[:, :ls], hg_p, ssm_p, conv_p, hg_s, ssm_s, conv_s)
```

```python
import functools

import jax
import jax.numpy as jnp
from jax import lax
from jax.experimental import pallas as pl
from jax.experimental.pallas import tpu as pltpu

F32 = jnp.float32
BF16 = jnp.bfloat16

D_MODEL = 1024
DEPTH = 2
HG_HEADS = 8
HG_DK = 128
HG_DV = 128
HG_DIM = HG_HEADS * HG_DK
SSD_INNER = 2 * D_MODEL
SSD_HEADDIM = 64
SSD_HEADS = SSD_INNER // SSD_HEADDIM
SSD_GROUPS = 4
SSD_HPG = SSD_HEADS // SSD_GROUPS
SSD_STATE = 128
SSD_PAIRS = SSD_HEADS // 2
PAIRS_PER_GROUP = SSD_PAIRS // SSD_GROUPS
GROUP_W = SSD_INNER // SSD_GROUPS
CONV_K = 4
CONV_DIM = SSD_INNER + 2 * SSD_GROUPS * SSD_STATE
N_EXPERTS = 32
TOP_K = 4
D_FF = D_MODEL
SWIGLU_LIMIT = 7.0
SWIGLU_ALPHA = 1.702
ALPHA = (2.0 * DEPTH) ** 0.25
LN_EPS = 1e-5
RMS_EPS = 1e-6

LANES = 128
SUBLANES = 8

DT_W = SSD_GROUPS * LANES
OFF_Q = 0
OFF_F = HG_DIM
OFF_V = 2 * HG_DIM
OFF_G = 3 * HG_DIM
OFF_Z = 4 * HG_DIM
OFF_XBC = OFF_Z + SSD_INNER
OFF_GA = OFF_XBC + CONV_DIM
OFF_GB = OFF_GA + D_MODEL
OFF_DT = OFF_GB + D_MODEL
N_W = OFF_DT + DT_W
IN_SPLITS = (HG_DIM, HG_DIM, HG_DIM, HG_DIM, SSD_INNER, CONV_DIM, SSD_HEADS, D_MODEL, D_MODEL)

ROW_TILE = 128
CONV_ROWS = 256
TOKEN_TILE = 256
MOE_BM = 256
VMEM_LIMIT = 48 * 1024 * 1024
LOG2E = 1.4426950408889634
HALO = 16
HGRN_HEADS_PER_STEP = 4


def _cparams(sem):
    return pltpu.CompilerParams(dimension_semantics=sem, vmem_limit_bytes=VMEM_LIMIT)


def _sigmoid(x):
    return jax.nn.sigmoid(x)


def _dot(a, b):
    return jnp.dot(a, b, preferred_element_type=F32)


def _dot_nt(a, b):
    return lax.dot_general(a, b, (((1,), (1,)), ((), ())), preferred_element_type=F32)


def _dot_tn(a, b):
    return lax.dot_general(a, b, (((0,), (0,)), ((), ())), preferred_element_type=F32)


def _mm_kernel(x_ref, w_ref, o_ref):
    o_ref[...] = _dot(x_ref[...], w_ref[...])


def _matmul(x, w, tm, tn):
    m, k = x.shape
    n = w.shape[1]
    return pl.pallas_call(
        _mm_kernel,
        grid=(n // tn, m // tm),
        in_specs=[pl.BlockSpec((tm, k), lambda j, i: (i, 0)),
                  pl.BlockSpec((k, tn), lambda j, i: (0, j))],
        out_specs=pl.BlockSpec((tm, tn), lambda j, i: (i, j)),
        out_shape=jax.ShapeDtypeStruct((m, n), F32),
        compiler_params=_cparams(("arbitrary", "arbitrary")),
        name="in_proj",
    )(x, w)


def _conv_kernel(x_ref, pb_ref, p0_ref, w_ref, b_ref, o_ref):
    i = pl.program_id(1)
    x = x_ref[...]
    prev = jnp.where(i == 0, p0_ref[0], pb_ref[...])
    full = jnp.concatenate([prev, x], axis=0)
    w = w_ref[...]
    acc = b_ref[...] + w[CONV_K - 1:CONV_K] * x
    for s in range(1, CONV_K):
        xs = pltpu.roll(full, s, 0)[SUBLANES:]
        acc = acc + w[CONV_K - 1 - s:CONV_K - s] * xs
    o_ref[...] = acc * _sigmoid(acc)


def _conv_short_kernel(x_ref, p_ref, w_ref, b_ref, o_ref):
    x = x_ref[...]
    p = p_ref[...]
    rc = lax.broadcasted_iota(jnp.int32, x.shape, 0) & (SUBLANES - 1)
    w = w_ref[...]
    acc = b_ref[...] + w[CONV_K - 1:CONV_K] * x
    for s in range(1, CONV_K):
        xs = jnp.where(rc >= s, pltpu.roll(x, s, 0), pltpu.roll(p, (s - SUBLANES) % ROW_TILE, 0))
        acc = acc + w[CONV_K - 1 - s:CONV_K - s] * xs
    o_ref[...] = acc * _sigmoid(acc)


def _conv(xw, prev8, conv_w, conv_b, bsz, l, base_rows):
    cb = OFF_XBC // CONV_DIM
    wspec = [pl.BlockSpec((CONV_K, CONV_DIM), lambda *_: (0, 0)), pl.BlockSpec((1, CONV_DIM), lambda *_: (0, 0))]
    out_shape = jax.ShapeDtypeStruct((bsz * l, CONV_DIM), F32)
    if l == SUBLANES:
        r = ROW_TILE
        return pl.pallas_call(
            _conv_short_kernel,
            grid=(bsz * l // r,),
            in_specs=[pl.BlockSpec((r, CONV_DIM), lambda i: (base_rows // r + i, cb)),
                      pl.BlockSpec((r, CONV_DIM), lambda i: (i, 0))] + wspec,
            out_specs=pl.BlockSpec((r, CONV_DIM), lambda i: (i, 0)),
            out_shape=out_shape,
            compiler_params=_cparams(("arbitrary",)),
            name="conv_silu_short",
        )(xw, prev8.reshape(bsz * SUBLANES, CONV_DIM), conv_w, conv_b)
    r = CONV_ROWS
    tps = l // r
    rb = r // SUBLANES
    return pl.pallas_call(
        _conv_kernel,
        grid=(bsz, tps),
        in_specs=[pl.BlockSpec((r, CONV_DIM), lambda bi, i: (base_rows // r + bi * tps + i, cb)),
                  pl.BlockSpec((SUBLANES, CONV_DIM),
                               lambda bi, i: (jnp.maximum(base_rows // SUBLANES + (bi * tps + i) * rb - 1, 0), cb)),
                  pl.BlockSpec((1, SUBLANES, CONV_DIM), lambda bi, i: (bi, 0, 0))] + wspec,
        out_specs=pl.BlockSpec((r, CONV_DIM), lambda bi, i: (bi * tps + i, 0)),
        out_shape=out_shape,
        compiler_params=_cparams(("arbitrary", "arbitrary")),
        name="conv_silu",
    )(xw, xw, prev8, conv_w, conv_b)


def _seg_cumsum(x, rc, seg):
    s = 1
    while s < seg:
        x = x + jnp.where(rc >= s, pltpu.roll(x, s, 0), 0.0)
        s *= 2
    return x


def _pad_rows(x, rows):
    if x.shape[0] == rows:
        return x
    return jnp.concatenate([x, jnp.zeros((rows - x.shape[0], x.shape[1]), x.dtype)], axis=0)


def _split3(x):
    hi = x.astype(BF16)
    r1 = x - hi.astype(F32)
    mid = r1.astype(BF16)
    lo = (r1 - mid.astype(F32)).astype(BF16)
    return hi, mid, lo


def _hgrn_kernel(*refs, hps, nseq, nc, **kw):
    s0_ref, sout_ref, st_ref = refs[8], refs[10], refs[11]
    c = pl.program_id(2)

    @pl.when(c == 0)
    def _():
        for hh in range(hps):
            for i in range(nseq):
                st_ref[hh, i] = s0_ref[0, i, hh].T

    for hh in range(hps):
        _hgrn_head(hh, *refs, nseq=nseq, **kw)

    @pl.when(c == nc - 1)
    def _():
        for hh in range(hps):
            for i in range(nseq):
                sout_ref[i, hh] = st_ref[hh, i].T


def _hgrn_head(hh, q_ref, f_ref, v_ref, g_ref, la_ref, l1m_ref, oml_ref, nw_ref, s0_ref,
               o_ref, sout_ref, st_ref, k_scr, b_scr, *, ch, sub, nseq, valid):
    r = ROW_TILE
    cols = slice(hh * LANES, (hh + 1) * LANES)
    q = q_ref[:, cols]
    fr = f_ref[:, cols]
    v = v_ref[:, cols]
    q = q * _sigmoid(q)
    la = la_ref[hh]
    lsig = jnp.minimum(fr, 0.0) - jnp.log1p(jnp.exp(-jnp.abs(fr)))
    cc = l1m_ref[hh] + lsig
    logf = jnp.maximum(la, cc) + jnp.log1p(jnp.exp(-jnp.abs(la - cc)))
    k = oml_ref[hh] * _sigmoid(-fr)

    row = lax.broadcasted_iota(jnp.int32, (r, LANES), 0)
    lane = lax.broadcasted_iota(jnp.int32, (r, LANES), 1)
    if valid < ch:
        pad = (row & (ch - 1)) >= valid
        logf = jnp.where(pad, 0.0, logf)
        k = jnp.where(pad, 0.0, k)

    tri = lane <= row
    if ch < r:
        shift = ch.bit_length() - 1
        tri = tri & ((row >> shift) == (lane >> shift))
    tri = jnp.where(tri, 1.0, 0.0).astype(BF16)
    hi, mid, lo = _split3(logf)
    b2 = (_dot(tri, hi) + _dot(tri, mid) + _dot(tri, lo)) * LOG2E

    k_scr[hh, 0:HALO] = jnp.zeros((HALO, LANES), F32)
    b_scr[hh, 0:HALO] = jnp.zeros((HALO, LANES), F32)
    k_scr[hh, HALO:HALO + r] = k
    b_scr[hh, HALO:HALO + r] = b2
    rs = row & (sub - 1)
    scl = jnp.where(lane == rs, jnp.sum(q * k, axis=-1, keepdims=True), 0.0)
    for d in range(1, sub):
        kd = k_scr[hh, pl.ds(HALO - d, r), :]
        bd = b_scr[hh, pl.ds(HALO - d, r), :]
        sd = jnp.sum(q * kd * jnp.exp2(b2 - bd), axis=-1, keepdims=True)
        scl = jnp.where(lane == rs - d, sd, scl)

    spc = ch // sub
    pieces = []
    for m in range(r // sub):
        lo_, hi_ = m * sub, (m + 1) * sub
        piece = scl[lo_:hi_]
        if m:
            piece = pltpu.roll(piece, lo_, 1)
        j = m % spc
        if j:
            cs = (m - j) * sub
            ref_b = b2[lo_ - 1:lo_]
            qj = q[lo_:hi_] * jnp.exp2(b2[lo_:hi_] - ref_b)
            kk = k[cs:lo_] * jnp.exp2(ref_b - b2[cs:lo_])
            parts = ([jnp.zeros((cs, LANES), F32)] if cs else []) + [kk, jnp.zeros((r - lo_, LANES), F32)]
            piece = piece + _dot_nt(qj.astype(BF16), jnp.concatenate(parts, axis=0).astype(BF16))
        pieces.append(piece)
    sc = jnp.concatenate(pieces, axis=0)
    o = _dot(sc.astype(BF16), v.astype(BF16))

    inter = []
    for i in range(nseq):
        lo_, hi_ = i * ch, (i + 1) * ch
        bi = b2[lo_:hi_]
        bl = bi[ch - 1:ch]
        st = st_ref[hh, i]
        inter.append(_dot_nt((q[lo_:hi_] * jnp.exp2(bi)).astype(BF16), st.astype(BF16)))
        kh = _pad_rows(k[lo_:hi_] * jnp.exp2(bl - bi), r)
        vi = _pad_rows(v[lo_:hi_], r)
        st_ref[hh, i] = st * jnp.exp2(bl) + _dot_tn(vi.astype(BF16), kh.astype(BF16))
    o = o + (inter[0] if nseq == 1 else jnp.concatenate(inter, axis=0))

    ms = jnp.mean(o * o, axis=-1, keepdims=True)
    o_ref[:, cols] = o * lax.rsqrt(ms + RMS_EPS) * nw_ref[hh] * _sigmoid(g_ref[:, cols])


def _seq_tiling(bsz, l):
    r = ROW_TILE
    ch = min(l, r)
    nseq = r // ch
    nc = max(l // r, 1)
    nb = bsz * l // (r * nc)
    return ch, nseq, nc, nb


def _hgrn(xw, s0_all, li, la, l1m, oml, nw, bsz, l, valid, base_tile):
    r = ROW_TILE
    ch, nseq, nc, nb = _seq_tiling(bsz, l)
    sub = min(ch, HALO)
    hps = HGRN_HEADS_PER_STEP
    w = hps * LANES
    kern = functools.partial(_hgrn_kernel, hps=hps, ch=ch, sub=sub, nseq=nseq, valid=valid, nc=nc)

    def xspec(off):
        return pl.BlockSpec((r, w), lambda bb, h, c: (base_tile + bb * nc + c, off // w + h))

    def pspec():
        return pl.BlockSpec((hps, 1, LANES), lambda bb, h, c: (h, 0, 0))

    return pl.pallas_call(
        kern,
        grid=(nb, HG_HEADS // hps, nc),
        in_specs=[xspec(OFF_Q), xspec(OFF_F), xspec(OFF_V), xspec(OFF_G),
                  pspec(), pspec(), pspec(), pspec(),
                  pl.BlockSpec((1, nseq, hps, HG_DK, HG_DV), lambda bb, h, c: (li, bb, h, 0, 0))],
        out_specs=[pl.BlockSpec((r, w), lambda bb, h, c: (bb * nc + c, h)),
                   pl.BlockSpec((nseq, hps, HG_DK, HG_DV), lambda bb, h, c: (bb, h, 0, 0))],
        out_shape=[jax.ShapeDtypeStruct((bsz * l, HG_DIM), F32),
                   jax.ShapeDtypeStruct(s0_all.shape[1:], F32)],
        scratch_shapes=[pltpu.VMEM((hps, nseq, HG_DV, HG_DK), F32),
                        pltpu.VMEM((hps, HALO + r, LANES), F32), pltpu.VMEM((hps, HALO + r, LANES), F32)],
        compiler_params=_cparams(("arbitrary", "arbitrary", "arbitrary")),
        name="hgrn2",
    )(xw, xw, xw, xw, la, l1m, oml, nw, s0_all)


def _ssd_kernel(xs_ref, bm_ref, cm_ref, z_ref, dt_ref, dtb_ref, a_ref, dsk_ref, nw_ref, h0_ref,
                y_ref, hout_ref, h_ref, *, ch, nseq, valid, nc):
    r = ROW_TILE
    c = pl.program_id(2)

    @pl.when(c == 0)
    def _():
        h_ref[...] = h0_ref[0]

    shift = ch.bit_length() - 1
    rowl = lax.broadcasted_iota(jnp.int32, (r, LANES), 0)
    lane = lax.broadcasted_iota(jnp.int32, (r, LANES), 1)
    rc = rowl & (ch - 1)
    dtr = dt_ref[...] + dtb_ref[...]
    dt = jnp.maximum(dtr, 0.0) + jnp.log1p(jnp.exp(-jnp.abs(dtr)))
    if valid < ch:
        dt = jnp.where(rc < valid, dt, 0.0)
    acum = _seg_cumsum(dt * a_ref[...], rc, ch)
    acum_t = acum.T
    dt_t = dt.T

    mask = (lane <= rowl) & ((rowl >> shift) == (lane >> shift))
    even_lane = lane < SSD_HEADDIM
    even_row = rowl < SSD_HEADDIM
    last_row = rc == ch - 1

    bmb = bm_ref[0].astype(BF16)
    cmb = cm_ref[0].astype(BF16)
    cb = _dot_nt(cmb, bmb)
    xs = xs_ref[0]
    z = z_ref[...]
    for pp in range(PAIRS_PER_GROUP):
        xp = xs[:, pp * LANES:(pp + 1) * LANES]
        acc = dsk_ref[:, pp * LANES:(pp + 1) * LANES] * xp
        for e in range(2):
            hd = 2 * pp + e
            w = cb * jnp.exp(jnp.where(mask, acum[:, hd:hd + 1] - acum_t[hd:hd + 1, :], -jnp.inf)) * dt_t[hd:hd + 1, :]
            xm = jnp.where(even_lane if e == 0 else jnp.logical_not(even_lane), xp, 0.0)
            acc = acc + _dot(w.astype(BF16), xm.astype(BF16))
        a_lane = jnp.where(even_lane, acum[:, 2 * pp:2 * pp + 1], acum[:, 2 * pp + 1:2 * pp + 2])
        dt_lane = jnp.where(even_lane, dt[:, 2 * pp:2 * pp + 1], dt[:, 2 * pp + 1:2 * pp + 2])
        ea = jnp.exp(a_lane)

        def seq_body(i, acc, pp=pp, xp=xp, a_lane=a_lane, dt_lane=dt_lane, ea=ea):
            in_seq = (rowl >> shift) == i
            al_lane = jnp.sum(jnp.where(in_seq & last_row, a_lane, 0.0), axis=0, keepdims=True)
            hp = h_ref[i, pp]
            yi = _dot_nt(cmb, hp.astype(BF16)) * ea
            acc = acc + jnp.where(in_seq, yi, 0.0)
            xw = jnp.where(in_seq, xp * (jnp.exp(al_lane - a_lane) * dt_lane), 0.0)
            dec = jnp.where(even_row, jnp.exp(al_lane[:, 0:1]), jnp.exp(al_lane[:, LANES - 1:LANES]))
            h_ref[i, pp] = dec * hp + _dot_tn(xw.astype(BF16), bmb)
            return acc

        if nseq == 1:
            acc = seq_body(0, acc)
        else:
            acc = lax.fori_loop(0, nseq, seq_body, acc)
        zz = z[:, pp * LANES:(pp + 1) * LANES]
        y_ref[:, pp * LANES:(pp + 1) * LANES] = acc * (zz * _sigmoid(zz))

    y = y_ref[...]
    ms = jnp.mean(y * y, axis=-1, keepdims=True)
    y_ref[...] = y * lax.rsqrt(ms + RMS_EPS) * nw_ref[...]

    @pl.when(c == nc - 1)
    def _():
        hout_ref[...] = h_ref[...]


def _ssd(xc3, xw, h0_all, li, dtb, a, dsk, nw, bsz, l, valid, base_tile):
    r = ROW_TILE
    ch, nseq, nc, nb = _seq_tiling(bsz, l)
    kern = functools.partial(_ssd_kernel, ch=ch, nseq=nseq, valid=valid, nc=nc)
    hspec = pl.BlockSpec((nseq, PAIRS_PER_GROUP, LANES, SSD_STATE), lambda bb, g, c: (bb, g, 0, 0))
    h0spec = pl.BlockSpec((1, nseq, PAIRS_PER_GROUP, LANES, SSD_STATE), lambda bb, g, c: (li, bb, g, 0, 0))
    b_off = SSD_INNER // SSD_STATE
    c_off = b_off + SSD_GROUPS

    def vec(w):
        return pl.BlockSpec((1, w), lambda bb, g, c: (0, g))

    return pl.pallas_call(
        kern,
        grid=(nb, SSD_GROUPS, nc),
        in_specs=[pl.BlockSpec((1, r, GROUP_W), lambda bb, g, c: (bb * nc + c, 0, g)),
                  pl.BlockSpec((1, r, SSD_STATE), lambda bb, g, c: (bb * nc + c, 0, b_off + g)),
                  pl.BlockSpec((1, r, SSD_STATE), lambda bb, g, c: (bb * nc + c, 0, c_off + g)),
                  pl.BlockSpec((r, GROUP_W), lambda bb, g, c: (base_tile + bb * nc + c, OFF_Z // GROUP_W + g)),
                  pl.BlockSpec((r, LANES), lambda bb, g, c: (base_tile + bb * nc + c, OFF_DT // LANES + g)),
                  vec(LANES), vec(LANES), vec(GROUP_W), vec(GROUP_W), h0spec],
        out_specs=[pl.BlockSpec((r, GROUP_W), lambda bb, g, c: (bb * nc + c, g)), hspec],
        out_shape=[jax.ShapeDtypeStruct((bsz * l, SSD_INNER), F32),
                   jax.ShapeDtypeStruct(h0_all.shape[1:], F32)],
        scratch_shapes=[pltpu.VMEM((nseq, PAIRS_PER_GROUP, LANES, SSD_STATE), F32)],
        compiler_params=_cparams(("arbitrary", "arbitrary", "arbitrary")),
        name="ssd",
    )(xc3, xc3, xc3, xw, xw, dtb, a, dsk, nw, h0_all)


def _layer_norm(x, g, b):
    mu = jnp.mean(x, axis=-1, keepdims=True)
    xc = x - mu
    var = jnp.mean(xc * xc, axis=-1, keepdims=True)
    return xc * lax.rsqrt(var + LN_EPS) * g + b


def _postmix_kernel(op_ref, os_ref, yp_ref, ys_ref, ga_ref, gb_ref, x_ref, hgp_ref, ssp_ref, wo_ref, g_ref, b_ref,
                    rwh_ref, rwl_ref, rb_ref, h_ref, lg_ref, *, n_p):
    first = pl.program_id(0) < n_p
    o = jnp.where(first, op_ref[...], os_ref[...])
    y = jnp.where(first, yp_ref[...], ys_ref[...])
    out_a = _dot(o.astype(BF16), hgp_ref[...])
    out_b = _dot(y.astype(BF16), ssp_ref[...])
    merged = _sigmoid(ga_ref[...]) * out_a + _sigmoid(gb_ref[...]) * out_b
    mix = _dot(merged.astype(BF16), wo_ref[...])
    h = _layer_norm(ALPHA * x_ref[...] + mix, g_ref[...], b_ref[...])
    h_ref[...] = h
    hh = h.astype(BF16)
    hl = (h - hh.astype(F32)).astype(BF16)
    lg_ref[...] = (_dot(hh, rwh_ref[...]) + _dot(hl, rwh_ref[...]) + _dot(hh, rwl_ref[...])) + rb_ref[...]


def _postmix(o_p, o_s, y_p, y_s, xw, x, hgp, ssp, wo, g, b, rwh, rwl, rb):
    t = x.shape[0]
    tm = TOKEN_TILE
    n_p = o_p.shape[0] // tm

    def full(a):
        return pl.BlockSpec(a.shape, lambda i: (0, 0))

    def first(w):
        return pl.BlockSpec((tm, w), lambda i: (jnp.minimum(i, n_p - 1), 0))

    def second(w):
        return pl.BlockSpec((tm, w), lambda i: (jnp.maximum(i - n_p, 0), 0))

    return pl.pallas_call(
        functools.partial(_postmix_kernel, n_p=n_p),
        grid=(t // tm,),
        in_specs=[first(HG_DIM), second(HG_DIM), first(SSD_INNER), second(SSD_INNER),
                  pl.BlockSpec((tm, D_MODEL), lambda i: (i, OFF_GA // D_MODEL)),
                  pl.BlockSpec((tm, D_MODEL), lambda i: (i, OFF_GB // D_MODEL)),
                  pl.BlockSpec((tm, D_MODEL), lambda i: (i, 0)),
                  full(hgp), full(ssp), full(wo), full(g), full(b), full(rwh), full(rwl), full(rb)],
        out_specs=[pl.BlockSpec((tm, D_MODEL), lambda i: (i, 0)),
                   pl.BlockSpec((tm, LANES), lambda i: (i, 0))],
        out_shape=[jax.ShapeDtypeStruct((t, D_MODEL), F32),
                   jax.ShapeDtypeStruct((t, LANES), F32)],
        compiler_params=_cparams(("arbitrary",)),
        name="postmix",
    )(o_p, o_s, y_p, y_s, xw, xw, x, hgp, ssp, wo, g, b, rwh, rwl, rb)


def _moe_kernel(be_ref, nu_ref, tokc_ref, tokn_ref, dst_ref, h_hbm, wgu_ref, bgu_ref, wd_ref, bd_ref,
                ys_hbm, xbuf, obuf, gsem, ssem, *, nblk, trash_row):
    j = pl.program_id(0)
    nu = nu_ref[0]
    slot = j & 1

    def gather(tok_ref, s):
        def body(i, carry):
            t = tok_ref[0, 0, i]
            pltpu.make_async_copy(h_hbm.at[pl.ds(t, 1)], xbuf.at[s, pl.ds(i, 1)], gsem.at[s]).start()
            return carry
        lax.fori_loop(0, MOE_BM, body, 0, unroll=8)

    def scatter_copy(s, i, row):
        return pltpu.make_async_copy(obuf.at[s, pl.ds(i, 1)], ys_hbm.at[pl.ds(row, 1)], ssem.at[s])

    def block_wait(s):
        pltpu.make_async_copy(obuf.at[s], ys_hbm.at[pl.ds(0, MOE_BM)], ssem.at[s]).wait()

    @pl.when(j == 0)
    def _():
        obuf[1] = jnp.zeros((MOE_BM, D_MODEL), F32)
        for half in range(2):
            zero_fill = pltpu.make_async_copy(obuf.at[1], ys_hbm.at[pl.ds(trash_row + half * MOE_BM, MOE_BM)],
                                              ssem.at[1])
            zero_fill.start()
            zero_fill.wait()
        gather(tokc_ref, 0)

    @pl.when(j + 1 < nu)
    def _():
        gather(tokn_ref, 1 - slot)

    @pl.when(j < nu)
    def _():
        pltpu.make_async_copy(h_hbm.at[pl.ds(0, MOE_BM)], xbuf.at[slot], gsem.at[slot]).wait()

        @pl.when(j >= 2)
        def _():
            block_wait(slot)

        gu = _dot(xbuf[slot].astype(BF16), wgu_ref[0, 0]) + bgu_ref[0, 0]
        gate = jnp.minimum(gu[:, :D_FF], SWIGLU_LIMIT)
        up = jnp.clip(gu[:, D_FF:], -SWIGLU_LIMIT, SWIGLU_LIMIT)
        act = (up + 1.0) * gate * _sigmoid(SWIGLU_ALPHA * gate)
        obuf[slot] = _dot(act.astype(BF16), wd_ref[0, 0]) + bd_ref[0, 0]

        def body(i, carry):
            scatter_copy(slot, i, dst_ref[0, 0, i]).start()
            return carry
        lax.fori_loop(0, MOE_BM, body, 0, unroll=8)

    @pl.when(j == nblk - 1)
    def _():
        block_wait((nu - 1) & 1)

        @pl.when(nu >= 2)
        def _():
            block_wait(nu & 1)


def _moe_experts(block_e, n_used, row_tok, row_dst, h, wgu, bgu, wd, bd, li):
    t = h.shape[0]
    nblk = row_tok.shape[0]
    trash_row = TOP_K * t
    kern = functools.partial(_moe_kernel, nblk=nblk, trash_row=trash_row)

    def smem(imap):
        return pl.BlockSpec((1, 1, MOE_BM), imap, memory_space=pltpu.SMEM)

    grid_spec = pltpu.PrefetchScalarGridSpec(
        num_scalar_prefetch=2,
        grid=(nblk,),
        in_specs=[smem(lambda j, be, nu: (j, 0, 0)),
                  smem(lambda j, be, nu: (jnp.minimum(j + 1, nblk - 1), 0, 0)),
                  smem(lambda j, be, nu: (j, 0, 0)),
                  pl.BlockSpec(memory_space=pl.ANY),
                  pl.BlockSpec((1, 1, D_MODEL, 2 * D_FF), lambda j, be, nu: (li, be[j], 0, 0)),
                  pl.BlockSpec((1, 1, 1, 2 * D_FF), lambda j, be, nu: (li, be[j], 0, 0)),
                  pl.BlockSpec((1, 1, D_FF, D_MODEL), lambda j, be, nu: (li, be[j], 0, 0)),
                  pl.BlockSpec((1, 1, 1, D_MODEL), lambda j, be, nu: (li, be[j], 0, 0))],
        out_specs=pl.BlockSpec(memory_space=pl.ANY),
        scratch_shapes=[pltpu.VMEM((2, MOE_BM, D_MODEL), F32), pltpu.VMEM((2, MOE_BM, D_MODEL), F32),
                        pltpu.SemaphoreType.DMA((2,)), pltpu.SemaphoreType.DMA((2,))],
    )
    return pl.pallas_call(
        kern,
        grid_spec=grid_spec,
        out_shape=jax.ShapeDtypeStruct((trash_row + 2 * MOE_BM, D_MODEL), F32),
        compiler_params=_cparams(("arbitrary",)),
        name="moe_experts",
    )(block_e, n_used, row_tok, row_tok, row_dst, h, wgu, bgu, wd, bd)


def _combine_kernel(h_ref, y0_ref, y1_ref, y2_ref, y3_ref, gt_ref, g_ref, b_ref, x_ref, xb_ref):
    gt = gt_ref[...]
    y = gt[:, 0:1] * y0_ref[...]
    for kk, y_ref in enumerate((y1_ref, y2_ref, y3_ref), start=1):
        y = y + gt[:, kk:kk + 1] * y_ref[...]
    x = _layer_norm(ALPHA * h_ref[...] + y, g_ref[...], b_ref[...])
    x_ref[...] = x
    xb_ref[...] = x.astype(BF16)


def _combine(h, ys, gates, g, b, tm):
    t = h.shape[0]
    nt = t // tm

    def yspec(kk):
        return pl.BlockSpec((tm, D_MODEL), lambda i: (kk * nt + i, 0))

    return pl.pallas_call(
        _combine_kernel,
        grid=(nt,),
        in_specs=[pl.BlockSpec((tm, D_MODEL), lambda i: (i, 0)),
                  yspec(0), yspec(1), yspec(2), yspec(3),
                  pl.BlockSpec((tm, TOP_K), lambda i: (i, 0)),
                  pl.BlockSpec((1, D_MODEL), lambda i: (0, 0)),
                  pl.BlockSpec((1, D_MODEL), lambda i: (0, 0))],
        out_specs=[pl.BlockSpec((tm, D_MODEL), lambda i: (i, 0)),
                   pl.BlockSpec((tm, D_MODEL), lambda i: (i, 0))],
        out_shape=[jax.ShapeDtypeStruct((t, D_MODEL), F32),
                   jax.ShapeDtypeStruct((t, D_MODEL), BF16)],
        compiler_params=_cparams(("arbitrary",)),
        name="combine_ln",
    )(h, ys, ys, ys, ys, gates, g, b)


def _split_cols(a, sizes):
    out, off = [], 0
    for s in sizes:
        out.append(a[..., off:off + s])
        off += s
    return out


def _per_group(vec_heads):
    v = vec_heads.reshape(SSD_GROUPS, SSD_HPG)
    return jnp.pad(v, ((0, 0), (0, LANES - SSD_HPG))).reshape(1, DT_W)


def _prep_layer(p, l):
    wq, wf, wv, wg, wz, wxbc, wdt, wga, wgb = _split_cols(p["w_in"][l], IN_SPLITS)
    wdt = jnp.pad(wdt.reshape(D_MODEL, SSD_GROUPS, SSD_HPG), ((0, 0), (0, 0), (0, LANES - SSD_HPG)))
    w_in = jnp.concatenate([wq, wf, wv, wg, wz, wxbc, wga, wgb, wdt.reshape(D_MODEL, DT_W)], axis=1).astype(BF16)
    lb = p["lb_all"][l].reshape(HG_HEADS, 1, HG_DK)
    rw = jnp.pad(p["router_w"][l], ((0, 0), (0, LANES - N_EXPERTS)))
    rwh = rw.astype(BF16)
    rwl = (rw - rwh.astype(F32)).astype(BF16)
    return dict(
        w_in=w_in,
        la=jnp.log(lb), l1m=jnp.log1p(-lb), oml=1.0 - lb,
        hg_nw=p["hg_norm_w"][l].reshape(HG_HEADS, 1, HG_DV),
        hgp=p["hg_proj"][l].astype(BF16),
        conv_w=p["conv_w"][l], conv_b=p["conv_b"][l].reshape(1, CONV_DIM),
        dtb=_per_group(p["dt_bias"][l]),
        a=_per_group(-jnp.exp(p["a_log"][l].astype(F32))),
        dsk=jnp.repeat(p["d_skip"][l], SSD_HEADDIM).reshape(1, SSD_INNER),
        ssd_nw=p["ssd_norm_w"][l].reshape(1, SSD_INNER),
        ssp=p["ssd_proj"][l].astype(BF16),
        wo=p["w_out"][l].astype(BF16),
        ln1_g=p["ln1_g"][l].reshape(1, D_MODEL), ln1_b=p["ln1_b"][l].reshape(1, D_MODEL),
        rwh=rwh, rwl=rwl,
        rb=jnp.pad(p["router_b"][l], (0, LANES - N_EXPERTS)).reshape(1, LANES),
        ln2_g=p["ln2_g"][l].reshape(1, D_MODEL), ln2_b=p["ln2_b"][l].reshape(1, D_MODEL),
    )


def _prep_experts(p):
    return dict(
        wgu=p["w_gu"].astype(BF16), bgu=p["b_gu"].reshape(DEPTH, N_EXPERTS, 1, 2 * D_FF),
        wd=p["w_down"].astype(BF16), bd=p["b_down"].reshape(DEPTH, N_EXPERTS, 1, D_MODEL),
    )


def _moe(h, logits, lp, ep, li, tm):
    t = h.shape[0]
    s = t * TOP_K
    top_v, top_e = lax.top_k(logits[:, :N_EXPERTS], TOP_K)
    gates = jax.nn.softmax(top_v, axis=-1)
    e_slot = top_e.reshape(s).astype(jnp.int32)
    order = jnp.argsort(e_slot).astype(jnp.int32)
    counts = jnp.sum((e_slot[:, None] == jnp.arange(N_EXPERTS, dtype=jnp.int32)[None, :]).astype(jnp.int32), axis=0)
    padded = (counts + MOE_BM - 1) // MOE_BM * MOE_BM
    pend = jnp.cumsum(padded)
    cend = pend - padded + counts
    n_blocks = (s + N_EXPERTS * (MOE_BM - 1) + MOE_BM - 1) // MOE_BM
    nrows = n_blocks * MOE_BM
    rows = jnp.arange(nrows, dtype=jnp.int32)[:, None]
    before = pend[None, :] <= rows
    pad_before = jnp.sum(jnp.where(before, padded - counts, 0), axis=1)
    is_pad = jnp.any((cend[None, :] <= rows) & (rows < pend[None, :]), axis=1) | (rows[:, 0] >= pend[-1])
    slot = order[jnp.clip(rows[:, 0] - pad_before, 0, s - 1)]
    tok = slot // TOP_K
    row_tok = jnp.where(is_pad, 0, tok)
    row_dst = jnp.where(is_pad, TOP_K * t + rows[:, 0] % (2 * MOE_BM), (slot - tok * TOP_K) * t + tok)
    blk_rows = jnp.arange(n_blocks, dtype=jnp.int32)[:, None] * MOE_BM
    block_e = jnp.minimum(jnp.sum((pend[None, :] <= blk_rows).astype(jnp.int32), axis=1), N_EXPERTS - 1)
    n_used = (pend[-1:] // MOE_BM).astype(jnp.int32)
    ys = _moe_experts(block_e, n_used, row_tok.reshape(n_blocks, 1, MOE_BM), row_dst.reshape(n_blocks, 1, MOE_BM),
                      h, ep["wgu"], ep["bgu"], ep["wd"], ep["bd"], li)
    return _combine(h, ys, gates, lp["ln2_g"], lp["ln2_b"], tm)


def _mixers(xw, li, lp, grp):
    bsz, l, valid, base = grp["bsz"], grp["l"], grp["valid"], grp["base"]
    t = bsz * l
    prev8 = jnp.pad(grp["s_conv"][li], ((0, 0), (SUBLANES - (CONV_K - 1), 0), (0, 0)))
    xc = _conv(xw, prev8, lp["conv_w"], lp["conv_b"], bsz, l, base)
    xbc = xw[base:base + t, OFF_XBC:OFF_XBC + CONV_DIM].reshape(bsz, l, CONV_DIM)
    conv_new = xbc[:, valid - (CONV_K - 1):valid]
    o, shg = _hgrn(xw, grp["s_hg"], li, lp["la"], lp["l1m"], lp["oml"], lp["hg_nw"], bsz, l, valid, base // ROW_TILE)
    y, hss = _ssd(xc.reshape(t // ROW_TILE, ROW_TILE, CONV_DIM), xw, grp["s_ssm"], li, lp["dtb"], lp["a"], lp["dsk"],
                  lp["ssd_nw"], bsz, l, valid, base // ROW_TILE)
    return o, y, shg, hss, conv_new


def _forward(x, groups, layers, ep):
    t = x.shape[0]
    xb = x.astype(BF16)
    new = [dict(hg=[], ssm=[], conv=[]) for _ in groups]
    for li, lp in enumerate(layers):
        xw = _matmul(xb, lp["w_in"], 512 if t % 512 == 0 else TOKEN_TILE, N_W // 4)
        mix =[_mixers(xw, li, lp, grp) for grp in groups]
        for n, m in zip(new, mix):
            n["hg"].append(m[2])
            n["ssm"].append(m[3])
            n["conv"].append(m[4])
        h, logits = _postmix(mix[0][0], mix[1][0], mix[0][1], mix[1][1], xw, x, lp["hgp"], lp["ssp"], lp["wo"],
                             lp["ln1_g"], lp["ln1_b"], lp["rwh"], lp["rwl"], lp["rb"])
        x, xb = _moe(h, logits, lp, ep, li, TOKEN_TILE)
    return x, [{k: jnp.stack(v) for k, v in n.items()} for n in new]


def kernel(x_prompt, x_sample, state_hgrn, state_ssm, state_conv, hg_lower_bounds, w_in, hg_norm_w, hg_proj,
           conv_w, conv_b, dt_bias, a_log, d_skip, ssd_norm_w, ssd_proj, w_out, ln1_g, ln1_b, router_w,
           router_b, w_gu, b_gu, w_down, b_down, ln2_g, ln2_b):
    lb_all = jnp.cumsum(jax.nn.softmax(hg_lower_bounds.astype(F32), axis=0), axis=0)
    lb_all = lb_all - lb_all[0]
    p = dict(lb_all=lb_all, w_in=w_in, hg_norm_w=hg_norm_w, hg_proj=hg_proj, conv_w=conv_w, conv_b=conv_b,
             dt_bias=dt_bias, a_log=a_log, d_skip=d_skip, ssd_norm_w=ssd_norm_w, ssd_proj=ssd_proj, w_out=w_out,
             ln1_g=ln1_g, ln1_b=ln1_b, router_w=router_w, router_b=router_b, w_gu=w_gu, b_gu=b_gu,
             w_down=w_down, b_down=b_down, ln2_g=ln2_g, ln2_b=ln2_b)
    layers = [_prep_layer(p, l) for l in range(DEPTH)]
    ep = _prep_experts(p)

    return _run(x_prompt, x_sample, state_hgrn, state_ssm, state_conv, layers, ep)


def _run(x_prompt, x_sample, state_hgrn, state_ssm, state_conv, layers, ep):
    bp, lprompt, _ = x_prompt.shape
    bs, ls, _ = x_sample.shape
    tp = bp * lprompt

    def pairs(s_ssm):
        return s_ssm.reshape(s_ssm.shape[:2] + (SSD_PAIRS, LANES, SSD_STATE))

    groups = [
        dict(bsz=bp, l=lprompt, valid=lprompt, base=0,
             s_hg=jnp.zeros((DEPTH, bp) + state_hgrn.shape[2:], F32),
             s_ssm=pairs(jnp.zeros((DEPTH, bp) + state_ssm.shape[2:], F32)),
             s_conv=jnp.zeros((DEPTH, bp) + state_conv.shape[2:], F32)),
        dict(bsz=bs, l=SUBLANES, valid=ls, base=tp, s_hg=state_hgrn, s_ssm=pairs(state_ssm), s_conv=state_conv),
    ]
    xs = jnp.pad(x_sample, ((0, 0), (0, SUBLANES - ls), (0, 0)))
    x = jnp.concatenate([x_prompt.reshape(tp, D_MODEL), xs.reshape(bs * SUBLANES, D_MODEL)], axis=0)
    x, new = _forward(x, groups, layers, ep)
    y_p = x[:tp].reshape(bp, lprompt, D_MODEL)
    y_s = x[tp:].reshape(bs, SUBLANES, D_MODEL)[:, :ls]
    (n_p, n_s) = new
    return (y_p, y_s, n_p["hg"], n_p["ssm"].reshape((DEPTH, bp) + state_ssm.shape[2:]), n_p["conv"],
            n_s["hg"], n_s["ssm"].reshape(state_ssm.shape), n_s["conv"])
```

```python
import functools

import jax
import jax.numpy as jnp
from jax import lax
from jax.experimental import pallas as pl
from jax.experimental.pallas import tpu as pltpu

F32 = jnp.float32
BF16 = jnp.bfloat16

D_MODEL = 1024
DEPTH = 2
HG_HEADS = 8
HG_DK = 128
HG_DV = 128
HG_DIM = HG_HEADS * HG_DK
SSD_INNER = 2 * D_MODEL
SSD_HEADDIM = 64
SSD_HEADS = SSD_INNER // SSD_HEADDIM
SSD_GROUPS = 4
SSD_HPG = SSD_HEADS // SSD_GROUPS
SSD_STATE = 128
SSD_PAIRS = SSD_HEADS // 2
PAIRS_PER_GROUP = SSD_PAIRS // SSD_GROUPS
GROUP_W = SSD_INNER // SSD_GROUPS
CONV_K = 4
CONV_DIM = SSD_INNER + 2 * SSD_GROUPS * SSD_STATE
N_EXPERTS = 32
TOP_K = 4
D_FF = D_MODEL
SWIGLU_LIMIT = 7.0
SWIGLU_ALPHA = 1.702
ALPHA = (2.0 * DEPTH) ** 0.25
LN_EPS = 1e-5
RMS_EPS = 1e-6

LANES = 128
SUBLANES = 8

DT_W = SSD_GROUPS * LANES
OFF_Q = 0
OFF_F = HG_DIM
OFF_V = 2 * HG_DIM
OFF_G = 3 * HG_DIM
OFF_Z = 4 * HG_DIM
OFF_XBC = OFF_Z + SSD_INNER
OFF_GA = OFF_XBC + CONV_DIM
OFF_GB = OFF_GA + D_MODEL
OFF_DT = OFF_GB + D_MODEL
N_W = OFF_DT + DT_W
IN_SPLITS = (HG_DIM, HG_DIM, HG_DIM, HG_DIM, SSD_INNER, CONV_DIM, SSD_HEADS, D_MODEL, D_MODEL)

ROW_TILE = 128
CONV_ROWS = 256
TOKEN_TILE = 256
MOE_BM = 256
VMEM_LIMIT = 48 * 1024 * 1024
MOE_VMEM_LIMIT = 56 * 1024 * 1024
LOG2E = 1.4426950408889634
HALO = 16
HGRN_HEADS_PER_STEP = 4


def _cparams(sem):
    return pltpu.CompilerParams(dimension_semantics=sem, vmem_limit_bytes=VMEM_LIMIT)


def _sigmoid(x):
    return jax.nn.sigmoid(x)


def _dot(a, b):
    return jnp.dot(a, b, preferred_element_type=F32)


def _dot_nt(a, b):
    return lax.dot_general(a, b, (((1,), (1,)), ((), ())), preferred_element_type=F32)


def _dot_tn(a, b):
    return lax.dot_general(a, b, (((0,), (0,)), ((), ())), preferred_element_type=F32)


def _mm_kernel(x_ref, w_ref, o_ref):
    o_ref[...] = _dot(x_ref[...], w_ref[...])


def _matmul(x, w, tm, tn):
    m, k = x.shape
    n = w.shape[1]
    return pl.pallas_call(
        _mm_kernel,
        grid=(n // tn, m // tm),
        in_specs=[pl.BlockSpec((tm, k), lambda j, i: (i, 0)),
                  pl.BlockSpec((k, tn), lambda j, i: (0, j))],
        out_specs=pl.BlockSpec((tm, tn), lambda j, i: (i, j)),
        out_shape=jax.ShapeDtypeStruct((m, n), F32),
        compiler_params=_cparams(("arbitrary", "arbitrary")),
        name="in_proj",
    )(x, w)


def _conv_kernel(x_ref, pb_ref, p0_ref, w_ref, b_ref, o_ref):
    i = pl.program_id(1)
    x = x_ref[...]
    prev = jnp.where(i == 0, p0_ref[0], pb_ref[...])
    full = jnp.concatenate([prev, x], axis=0)
    w = w_ref[...]
    acc = b_ref[...] + w[CONV_K - 1:CONV_K] * x
    for s in range(1, CONV_K):
        xs = pltpu.roll(full, s, 0)[SUBLANES:]
        acc = acc + w[CONV_K - 1 - s:CONV_K - s] * xs
    o_ref[...] = acc * _sigmoid(acc)


def _conv_short_kernel(x_ref, p_ref, w_ref, b_ref, o_ref):
    x = x_ref[...]
    p = p_ref[...]
    rc = lax.broadcasted_iota(jnp.int32, x.shape, 0) & (SUBLANES - 1)
    w = w_ref[...]
    acc = b_ref[...] + w[CONV_K - 1:CONV_K] * x
    for s in range(1, CONV_K):
        xs = jnp.where(rc >= s, pltpu.roll(x, s, 0), pltpu.roll(p, (s - SUBLANES) % ROW_TILE, 0))
        acc = acc + w[CONV_K - 1 - s:CONV_K - s] * xs
    o_ref[...] = acc * _sigmoid(acc)


def _conv(xw, prev8, conv_w, conv_b, bsz, l, base_rows):
    cb = OFF_XBC // CONV_DIM
    wspec = [pl.BlockSpec((CONV_K, CONV_DIM), lambda *_: (0, 0)), pl.BlockSpec((1, CONV_DIM), lambda *_: (0, 0))]
    out_shape = jax.ShapeDtypeStruct((bsz * l, CONV_DIM), F32)
    if l == SUBLANES:
        r = ROW_TILE
        return pl.pallas_call(
            _conv_short_kernel,
            grid=(bsz * l // r,),
            in_specs=[pl.BlockSpec((r, CONV_DIM), lambda i: (base_rows // r + i, cb)),
                      pl.BlockSpec((r, CONV_DIM), lambda i: (i, 0))] + wspec,
            out_specs=pl.BlockSpec((r, CONV_DIM), lambda i: (i, 0)),
            out_shape=out_shape,
            compiler_params=_cparams(("arbitrary",)),
            name="conv_silu_short",
        )(xw, prev8.reshape(bsz * SUBLANES, CONV_DIM), conv_w, conv_b)
    r = CONV_ROWS
    tps = l // r
    rb = r // SUBLANES
    return pl.pallas_call(
        _conv_kernel,
        grid=(bsz, tps),
        in_specs=[pl.BlockSpec((r, CONV_DIM), lambda bi, i: (base_rows // r + bi * tps + i, cb)),
                  pl.BlockSpec((SUBLANES, CONV_DIM),
                               lambda bi, i: (jnp.maximum(base_rows // SUBLANES + (bi * tps + i) * rb - 1, 0), cb)),
                  pl.BlockSpec((1, SUBLANES, CONV_DIM), lambda bi, i: (bi, 0, 0))] + wspec,
        out_specs=pl.BlockSpec((r, CONV_DIM), lambda bi, i: (bi * tps + i, 0)),
        out_shape=out_shape,
        compiler_params=_cparams(("arbitrary", "arbitrary")),
        name="conv_silu",
    )(xw, xw, prev8, conv_w, conv_b)


def _seg_cumsum(x, rc, seg):
    s = 1
    while s < seg:
        x = x + jnp.where(rc >= s, pltpu.roll(x, s, 0), 0.0)
        s *= 2
    return x


def _pad_rows(x, rows):
    if x.shape[0] == rows:
        return x
    return jnp.concatenate([x, jnp.zeros((rows - x.shape[0], x.shape[1]), x.dtype)], axis=0)


def _split3(x):
    hi = x.astype(BF16)
    r1 = x - hi.astype(F32)
    mid = r1.astype(BF16)
    lo = (r1 - mid.astype(F32)).astype(BF16)
    return hi, mid, lo


def _hgrn_kernel(*refs, hps, nseq, nc, **kw):
    s0_ref, sout_ref, st_ref = refs[8], refs[10], refs[11]
    c = pl.program_id(2)

    @pl.when(c == 0)
    def _():
        for hh in range(hps):
            for i in range(nseq):
                st_ref[hh, i] = s0_ref[0, i, hh].T

    for hh in range(hps):
        _hgrn_head(hh, *refs, nseq=nseq, **kw)

    @pl.when(c == nc - 1)
    def _():
        for hh in range(hps):
            for i in range(nseq):
                sout_ref[i, hh] = st_ref[hh, i].T


def _hgrn_head(hh, q_ref, f_ref, v_ref, g_ref, la_ref, l1m_ref, oml_ref, nw_ref, s0_ref,
               o_ref, sout_ref, st_ref, k_scr, b_scr, *, ch, sub, nseq, valid):
    r = ROW_TILE
    cols = slice(hh * LANES, (hh + 1) * LANES)
    q = q_ref[:, cols]
    fr = f_ref[:, cols]
    v = v_ref[:, cols]
    q = q * _sigmoid(q)
    la = la_ref[hh]
    lsig = jnp.minimum(fr, 0.0) - jnp.log1p(jnp.exp(-jnp.abs(fr)))
    cc = l1m_ref[hh] + lsig
    logf = jnp.maximum(la, cc) + jnp.log1p(jnp.exp(-jnp.abs(la - cc)))
    k = oml_ref[hh] * _sigmoid(-fr)

    row = lax.broadcasted_iota(jnp.int32, (r, LANES), 0)
    lane = lax.broadcasted_iota(jnp.int32, (r, LANES), 1)
    if valid < ch:
        pad = (row & (ch - 1)) >= valid
        logf = jnp.where(pad, 0.0, logf)
        k = jnp.where(pad, 0.0, k)

    tri = lane <= row
    if ch < r:
        shift = ch.bit_length() - 1
        tri = tri & ((row >> shift) == (lane >> shift))
    tri = jnp.where(tri, 1.0, 0.0).astype(BF16)
    hi, mid, lo = _split3(logf)
    b2 = (_dot(tri, hi) + _dot(tri, mid) + _dot(tri, lo)) * LOG2E

    k_scr[hh, 0:HALO] = jnp.zeros((HALO, LANES), F32)
    b_scr[hh, 0:HALO] = jnp.zeros((HALO, LANES), F32)
    k_scr[hh, HALO:HALO + r] = k
    b_scr[hh, HALO:HALO + r] = b2
    rs = row & (sub - 1)
    scl = jnp.where(lane == rs, jnp.sum(q * k, axis=-1, keepdims=True), 0.0)
    for d in range(1, sub):
        kd = k_scr[hh, pl.ds(HALO - d, r), :]
        bd = b_scr[hh, pl.ds(HALO - d, r), :]
        sd = jnp.sum(q * kd * jnp.exp2(b2 - bd), axis=-1, keepdims=True)
        scl = jnp.where(lane == rs - d, sd, scl)

    spc = ch // sub
    pieces = []
    for m in range(r // sub):
        lo_, hi_ = m * sub, (m + 1) * sub
        piece = scl[lo_:hi_]
        if m:
            piece = pltpu.roll(piece, lo_, 1)
        j = m % spc
        if j:
            cs = (m - j) * sub
            ref_b = b2[lo_ - 1:lo_]
            qj = q[lo_:hi_] * jnp.exp2(b2[lo_:hi_] - ref_b)
            kk = k[cs:lo_] * jnp.exp2(ref_b - b2[cs:lo_])
            parts = ([jnp.zeros((cs, LANES), F32)] if cs else []) + [kk, jnp.zeros((r - lo_, LANES), F32)]
            piece = piece + _dot_nt(qj.astype(BF16), jnp.concatenate(parts, axis=0).astype(BF16))
        pieces.append(piece)
    sc = jnp.concatenate(pieces, axis=0)
    o = _dot(sc.astype(BF16), v.astype(BF16))

    inter = []
    for i in range(nseq):
        lo_, hi_ = i * ch, (i + 1) * ch
        bi = b2[lo_:hi_]
        bl = bi[ch - 1:ch]
        st = st_ref[hh, i]
        inter.append(_dot_nt((q[lo_:hi_] * jnp.exp2(bi)).astype(BF16), st.astype(BF16)))
        kh = _pad_rows(k[lo_:hi_] * jnp.exp2(bl - bi), r)
        vi = _pad_rows(v[lo_:hi_], r)
        st_ref[hh, i] = st * jnp.exp2(bl) + _dot_tn(vi.astype(BF16), kh.astype(BF16))
    o = o + (inter[0] if nseq == 1 else jnp.concatenate(inter, axis=0))

    ms = jnp.mean(o * o, axis=-1, keepdims=True)
    o_ref[:, cols] = o * lax.rsqrt(ms + RMS_EPS) * nw_ref[hh] * _sigmoid(g_ref[:, cols])


def _seq_tiling(bsz, l):
    r = ROW_TILE
    ch = min(l, r)
    nseq = r // ch
    nc = max(l // r, 1)
    nb = bsz * l // (r * nc)
    return ch, nseq, nc, nb


def _hgrn(xw, s0_all, li, la, l1m, oml, nw, bsz, l, valid, base_tile):
    r = ROW_TILE
    ch, nseq, nc, nb = _seq_tiling(bsz, l)
    sub = min(ch, HALO)
    hps = HGRN_HEADS_PER_STEP
    w = hps * LANES
    kern = functools.partial(_hgrn_kernel, hps=hps, ch=ch, sub=sub, nseq=nseq, valid=valid, nc=nc)

    def xspec(off):
        return pl.BlockSpec((r, w), lambda bb, h, c: (base_tile + bb * nc + c, off // w + h))

    def pspec():
        return pl.BlockSpec((hps, 1, LANES), lambda bb, h, c: (h, 0, 0))

    return pl.pallas_call(
        kern,
        grid=(nb, HG_HEADS // hps, nc),
        in_specs=[xspec(OFF_Q), xspec(OFF_F), xspec(OFF_V), xspec(OFF_G),
                  pspec(), pspec(), pspec(), pspec(),
                  pl.BlockSpec((1, nseq, hps, HG_DK, HG_DV), lambda bb, h, c: (li, bb, h, 0, 0))],
        out_specs=[pl.BlockSpec((r, w), lambda bb, h, c: (bb * nc + c, h)),
                   pl.BlockSpec((nseq, hps, HG_DK, HG_DV), lambda bb, h, c: (bb, h, 0, 0))],
        out_shape=[jax.ShapeDtypeStruct((bsz * l, HG_DIM), F32),
                   jax.ShapeDtypeStruct(s0_all.shape[1:], F32)],
        scratch_shapes=[pltpu.VMEM((hps, nseq, HG_DV, HG_DK), F32),
                        pltpu.VMEM((hps, HALO + r, LANES), F32), pltpu.VMEM((hps, HALO + r, LANES), F32)],
        compiler_params=_cparams(("arbitrary", "arbitrary", "arbitrary")),
        name="hgrn2",
    )(xw, xw, xw, xw, la, l1m, oml, nw, s0_all)


def _ssd_kernel(xs_ref, bm_ref, cm_ref, z_ref, dt_ref, dtb_ref, a_ref, dsk_ref, nw_ref, h0_ref,
                y_ref, hout_ref, h_ref, *, ch, nseq, valid, nc):
    r = ROW_TILE
    c = pl.program_id(2)

    @pl.when(c == 0)
    def _():
        h_ref[...] = h0_ref[0]

    shift = ch.bit_length() - 1
    rowl = lax.broadcasted_iota(jnp.int32, (r, LANES), 0)
    lane = lax.broadcasted_iota(jnp.int32, (r, LANES), 1)
    rc = rowl & (ch - 1)
    dtr = dt_ref[...] + dtb_ref[...]
    dt = jnp.maximum(dtr, 0.0) + jnp.log1p(jnp.exp(-jnp.abs(dtr)))
    if valid < ch:
        dt = jnp.where(rc < valid, dt, 0.0)
    acum = _seg_cumsum(dt * a_ref[...], rc, ch)
    acum_t = acum.T
    dt_t = dt.T

    mask = (lane <= rowl) & ((rowl >> shift) == (lane >> shift))
    even_lane = lane < SSD_HEADDIM
    even_row = rowl < SSD_HEADDIM
    last_row = rc == ch - 1

    bmb = bm_ref[0].astype(BF16)
    cmb = cm_ref[0].astype(BF16)
    cb = _dot_nt(cmb, bmb)
    xs = xs_ref[0]
    z = z_ref[...]
    for pp in range(PAIRS_PER_GROUP):
        xp = xs[:, pp * LANES:(pp + 1) * LANES]
        acc = dsk_ref[:, pp * LANES:(pp + 1) * LANES] * xp
        for e in range(2):
            hd = 2 * pp + e
            w = cb * jnp.exp(jnp.where(mask, acum[:, hd:hd + 1] - acum_t[hd:hd + 1, :], -jnp.inf)) * dt_t[hd:hd + 1, :]
            xm = jnp.where(even_lane if e == 0 else jnp.logical_not(even_lane), xp, 0.0)
            acc = acc + _dot(w.astype(BF16), xm.astype(BF16))
        a_lane = jnp.where(even_lane, acum[:, 2 * pp:2 * pp + 1], acum[:, 2 * pp + 1:2 * pp + 2])
        dt_lane = jnp.where(even_lane, dt[:, 2 * pp:2 * pp + 1], dt[:, 2 * pp + 1:2 * pp + 2])
        ea = jnp.exp(a_lane)

        def seq_body(i, acc, pp=pp, xp=xp, a_lane=a_lane, dt_lane=dt_lane, ea=ea):
            in_seq = (rowl >> shift) == i
            al_lane = jnp.sum(jnp.where(in_seq & last_row, a_lane, 0.0), axis=0, keepdims=True)
            hp = h_ref[i, pp]
            yi = _dot_nt(cmb, hp.astype(BF16)) * ea
            acc = acc + jnp.where(in_seq, yi, 0.0)
            xw = jnp.where(in_seq, xp * (jnp.exp(al_lane - a_lane) * dt_lane), 0.0)
            dec = jnp.where(even_row, jnp.exp(al_lane[:, 0:1]), jnp.exp(al_lane[:, LANES - 1:LANES]))
            h_ref[i, pp] = dec * hp + _dot_tn(xw.astype(BF16), bmb)
            return acc

        if nseq == 1:
            acc = seq_body(0, acc)
        else:
            acc = lax.fori_loop(0, nseq, seq_body, acc)
        zz = z[:, pp * LANES:(pp + 1) * LANES]
        y_ref[:, pp * LANES:(pp + 1) * LANES] = acc * (zz * _sigmoid(zz))

    y = y_ref[...]
    ms = jnp.mean(y * y, axis=-1, keepdims=True)
    y_ref[...] = y * lax.rsqrt(ms + RMS_EPS) * nw_ref[...]

    @pl.when(c == nc - 1)
    def _():
        hout_ref[...] = h_ref[...]


def _ssd(xc3, xw, h0_all, li, dtb, a, dsk, nw, bsz, l, valid, base_tile):
    r = ROW_TILE
    ch, nseq, nc, nb = _seq_tiling(bsz, l)
    kern = functools.partial(_ssd_kernel, ch=ch, nseq=nseq, valid=valid, nc=nc)
    hspec = pl.BlockSpec((nseq, PAIRS_PER_GROUP, LANES, SSD_STATE), lambda bb, g, c: (bb, g, 0, 0))
    h0spec = pl.BlockSpec((1, nseq, PAIRS_PER_GROUP, LANES, SSD_STATE), lambda bb, g, c: (li, bb, g, 0, 0))
    b_off = SSD_INNER // SSD_STATE
    c_off = b_off + SSD_GROUPS

    def vec(w):
        return pl.BlockSpec((1, w), lambda bb, g, c: (0, g))

    return pl.pallas_call(
        kern,
        grid=(nb, SSD_GROUPS, nc),
        in_specs=[pl.BlockSpec((1, r, GROUP_W), lambda bb, g, c: (bb * nc + c, 0, g)),
                  pl.BlockSpec((1, r, SSD_STATE), lambda bb, g, c: (bb * nc + c, 0, b_off + g)),
                  pl.BlockSpec((1, r, SSD_STATE), lambda bb, g, c: (bb * nc + c, 0, c_off + g)),
                  pl.BlockSpec((r, GROUP_W), lambda bb, g, c: (base_tile + bb * nc + c, OFF_Z // GROUP_W + g)),
                  pl.BlockSpec((r, LANES), lambda bb, g, c: (base_tile + bb * nc + c, OFF_DT // LANES + g)),
                  vec(LANES), vec(LANES), vec(GROUP_W), vec(GROUP_W), h0spec],
        out_specs=[pl.BlockSpec((r, GROUP_W), lambda bb, g, c: (bb * nc + c, g)), hspec],
        out_shape=[jax.ShapeDtypeStruct((bsz * l, SSD_INNER), F32),
                   jax.ShapeDtypeStruct(h0_all.shape[1:], F32)],
        scratch_shapes=[pltpu.VMEM((nseq, PAIRS_PER_GROUP, LANES, SSD_STATE), F32)],
        compiler_params=_cparams(("arbitrary", "arbitrary", "arbitrary")),
        name="ssd",
    )(xc3, xc3, xc3, xw, xw, dtb, a, dsk, nw, h0_all)


def _layer_norm(x, g, b):
    mu = jnp.mean(x, axis=-1, keepdims=True)
    xc = x - mu
    var = jnp.mean(xc * xc, axis=-1, keepdims=True)
    return xc * lax.rsqrt(var + LN_EPS) * g + b


def _postmix_kernel(op_ref, os_ref, yp_ref, ys_ref, ga_ref, gb_ref, x_ref, hgp_ref, ssp_ref, wo_ref, g_ref, b_ref,
                    rwh_ref, rwl_ref, rb_ref, h_ref, lg_ref, *, n_p):
    first = pl.program_id(0) < n_p
    o = jnp.where(first, op_ref[...], os_ref[...])
    y = jnp.where(first, yp_ref[...], ys_ref[...])
    out_a = _dot(o.astype(BF16), hgp_ref[...])
    out_b = _dot(y.astype(BF16), ssp_ref[...])
    merged = _sigmoid(ga_ref[...]) * out_a + _sigmoid(gb_ref[...]) * out_b
    mix = _dot(merged.astype(BF16), wo_ref[...])
    h = _layer_norm(ALPHA * x_ref[...] + mix, g_ref[...], b_ref[...])
    h_ref[...] = h
    hh = h.astype(BF16)
    hl = (h - hh.astype(F32)).astype(BF16)
    lg_ref[...] = (_dot(hh, rwh_ref[...]) + _dot(hl, rwh_ref[...]) + _dot(hh, rwl_ref[...])) + rb_ref[...]


def _postmix(o_p, o_s, y_p, y_s, xw, x, hgp, ssp, wo, g, b, rwh, rwl, rb):
    t = x.shape[0]
    tm = TOKEN_TILE
    n_p = o_p.shape[0] // tm

    def full(a):
        return pl.BlockSpec(a.shape, lambda i: (0, 0))

    def first(w):
        return pl.BlockSpec((tm, w), lambda i: (jnp.minimum(i, n_p - 1), 0))

    def second(w):
        return pl.BlockSpec((tm, w), lambda i: (jnp.maximum(i - n_p, 0), 0))

    return pl.pallas_call(
        functools.partial(_postmix_kernel, n_p=n_p),
        grid=(t // tm,),
        in_specs=[first(HG_DIM), second(HG_DIM), first(SSD_INNER), second(SSD_INNER),
                  pl.BlockSpec((tm, D_MODEL), lambda i: (i, OFF_GA // D_MODEL)),
                  pl.BlockSpec((tm, D_MODEL), lambda i: (i, OFF_GB // D_MODEL)),
                  pl.BlockSpec((tm, D_MODEL), lambda i: (i, 0)),
                  full(hgp), full(ssp), full(wo), full(g), full(b), full(rwh), full(rwl), full(rb)],
        out_specs=[pl.BlockSpec((tm, D_MODEL), lambda i: (i, 0)),
                   pl.BlockSpec((tm, LANES), lambda i: (i, 0))],
        out_shape=[jax.ShapeDtypeStruct((t, D_MODEL), F32),
                   jax.ShapeDtypeStruct((t, LANES), F32)],
        compiler_params=_cparams(("arbitrary",)),
        name="postmix",
    )(o_p, o_s, y_p, y_s, xw, xw, x, hgp, ssp, wo, g, b, rwh, rwl, rb)


def _moe_kernel(be_ref, nu_ref, tokc_ref, tokn_ref, dstp_ref, dstc_ref, h_hbm, wgu_ref, bgu_ref, wd_ref, bd_ref,
                ys_hbm, xbuf, obuf, wgu_b, wd_b, gsem, ssem, *, nblk, trash_row):
    j = pl.program_id(0)
    nu = nu_ref[0]
    slot = j & 1
    nslot = 1 - slot

    def gather_start(tok_ref, s, i):
        pltpu.make_async_copy(h_hbm.at[pl.ds(tok_ref[0, 0, i], 1)], xbuf.at[s, pl.ds(i, 1)], gsem.at[s]).start()

    def scatter_start(dst_ref, s, i):
        pltpu.make_async_copy(obuf.at[s, pl.ds(i, 1)], ys_hbm.at[pl.ds(dst_ref[0, 0, i], 1)], ssem.at[s]).start()

    def gather_wait(s):
        pltpu.make_async_copy(h_hbm.at[pl.ds(0, MOE_BM)], xbuf.at[s], gsem.at[s]).wait()

    def scatter_wait(s):
        pltpu.make_async_copy(obuf.at[s], ys_hbm.at[pl.ds(0, MOE_BM)], ssem.at[s]).wait()

    @pl.when(j == 0)
    def _():
        obuf[...] = jnp.zeros(obuf.shape, F32)
        pltpu.make_async_copy(obuf.at[0], ys_hbm.at[pl.ds(trash_row, MOE_BM)], ssem.at[0]).start()

        def body(i, carry):
            gather_start(tokc_ref, 0, i)
            return carry
        lax.fori_loop(0, MOE_BM, body, 0, unroll=8)

    first_of_expert = (j == 0) | (be_ref[j] != be_ref[jnp.maximum(j - 1, 0)])

    @pl.when((j < nu) & first_of_expert)
    def _():
        rows = D_MODEL // 8
        for c in range(8):
            wgu_b[c * rows:(c + 1) * rows] = wgu_ref[0, 0, c * rows:(c + 1) * rows].astype(BF16)
            wd_b[c * rows:(c + 1) * rows] = wd_ref[0, 0, c * rows:(c + 1) * rows].astype(BF16)

    @pl.when(j < nu)
    def _():
        gather_wait(slot)
        scatter_wait(slot)
        for i in range(MOE_BM):
            gather_start(tokn_ref, nslot, i)
            scatter_start(dstp_ref, nslot, i)
        gu = _dot(xbuf[slot].astype(BF16), wgu_b[...]) + bgu_ref[0, 0]
        gate = jnp.minimum(gu[:, :D_FF], SWIGLU_LIMIT)
        up = jnp.clip(gu[:, D_FF:], -SWIGLU_LIMIT, SWIGLU_LIMIT)
        act = (up + 1.0) * gate * _sigmoid(SWIGLU_ALPHA * gate)
        obuf[slot] = _dot(act.astype(BF16), wd_b[...]) + bd_ref[0, 0]

    @pl.when(j == nu - 1)
    def _():
        def body(i, carry):
            scatter_start(dstc_ref, slot, i)
            return carry
        lax.fori_loop(0, MOE_BM, body, 0, unroll=8)

    @pl.when(j == nblk - 1)
    def _():
        scatter_wait((nu - 1) & 1)
        scatter_wait(nu & 1)
        gather_wait(nu & 1)


def _moe_experts(block_e, n_used, row_tok, row_dst, row_dst_prev, h, wgu, bgu, wd, bd, li):
    t = h.shape[0]
    nblk = row_tok.shape[0]
    trash_row = TOP_K * t
    kern = functools.partial(_moe_kernel, nblk=nblk, trash_row=trash_row)

    def smem(imap):
        return pl.BlockSpec((1, 1, MOE_BM), imap, memory_space=pltpu.SMEM)

    grid_spec = pltpu.PrefetchScalarGridSpec(
        num_scalar_prefetch=2,
        grid=(nblk,),
        in_specs=[smem(lambda j, be, nu: (j, 0, 0)),
                  smem(lambda j, be, nu: (jnp.minimum(j + 1, nblk - 1), 0, 0)),
                  smem(lambda j, be, nu: (j, 0, 0)),
                  smem(lambda j, be, nu: (j, 0, 0)),
                  pl.BlockSpec(memory_space=pl.ANY),
                  pl.BlockSpec((1, 1, D_MODEL, 2 * D_FF), lambda j, be, nu: (li, be[j], 0, 0)),
                  pl.BlockSpec((1, 1, 1, 2 * D_FF), lambda j, be, nu: (li, be[j], 0, 0)),
                  pl.BlockSpec((1, 1, D_FF, D_MODEL), lambda j, be, nu: (li, be[j], 0, 0)),
                  pl.BlockSpec((1, 1, 1, D_MODEL), lambda j, be, nu: (li, be[j], 0, 0))],
        out_specs=pl.BlockSpec(memory_space=pl.ANY),
        scratch_shapes=[pltpu.VMEM((2, MOE_BM, D_MODEL), F32), pltpu.VMEM((2, MOE_BM, D_MODEL), F32),
                        pltpu.VMEM((D_MODEL, 2 * D_FF), BF16), pltpu.VMEM((D_FF, D_MODEL), BF16),
                        pltpu.SemaphoreType.DMA((2,)), pltpu.SemaphoreType.DMA((2,))],
    )
    return pl.pallas_call(
        kern,
        grid_spec=grid_spec,
        out_shape=jax.ShapeDtypeStruct((trash_row + 2 * MOE_BM, D_MODEL), F32),
        compiler_params=pltpu.CompilerParams(dimension_semantics=("arbitrary",), vmem_limit_bytes=MOE_VMEM_LIMIT),
        name="moe_experts",
    )(block_e, n_used, row_tok, row_tok, row_dst_prev, row_dst, h, wgu, bgu, wd, bd)


def _combine_kernel(h_ref, y0_ref, y1_ref, y2_ref, y3_ref, gt_ref, g_ref, b_ref, x_ref, xb_ref):
    gt = gt_ref[...]
    y = gt[:, 0:1] * y0_ref[...]
    for kk, y_ref in enumerate((y1_ref, y2_ref, y3_ref), start=1):
        y = y + gt[:, kk:kk + 1] * y_ref[...]
    x = _layer_norm(ALPHA * h_ref[...] + y, g_ref[...], b_ref[...])
    x_ref[...] = x
    xb_ref[...] = x.astype(BF16)


def _combine(h, ys, gates, g, b, tm):
    t = h.shape[0]
    nt = t // tm

    def yspec(kk):
        return pl.BlockSpec((tm, D_MODEL), lambda i: (kk * nt + i, 0))

    return pl.pallas_call(
        _combine_kernel,
        grid=(nt,),
        in_specs=[pl.BlockSpec((tm, D_MODEL), lambda i: (i, 0)),
                  yspec(0), yspec(1), yspec(2), yspec(3),
                  pl.BlockSpec((tm, TOP_K), lambda i: (i, 0)),
                  pl.BlockSpec((1, D_MODEL), lambda i: (0, 0)),
                  pl.BlockSpec((1, D_MODEL), lambda i: (0, 0))],
        out_specs=[pl.BlockSpec((tm, D_MODEL), lambda i: (i, 0)),
                   pl.BlockSpec((tm, D_MODEL), lambda i: (i, 0))],
        out_shape=[jax.ShapeDtypeStruct((t, D_MODEL), F32),
                   jax.ShapeDtypeStruct((t, D_MODEL), BF16)],
        compiler_params=_cparams(("arbitrary",)),
        name="combine_ln",
    )(h, ys, ys, ys, ys, gates, g, b)


def _split_cols(a, sizes):
    out, off = [], 0
    for s in sizes:
        out.append(a[..., off:off + s])
        off += s
    return out


def _per_group(vec_heads):
    v = vec_heads.reshape(SSD_GROUPS, SSD_HPG)
    return jnp.pad(v, ((0, 0), (0, LANES - SSD_HPG))).reshape(1, DT_W)


def _prep_layer(p, l):
    wq, wf, wv, wg, wz, wxbc, wdt, wga, wgb = _split_cols(p["w_in"][l], IN_SPLITS)
    wdt = jnp.pad(wdt.reshape(D_MODEL, SSD_GROUPS, SSD_HPG), ((0, 0), (0, 0), (0, LANES - SSD_HPG)))
    w_in = jnp.concatenate([wq, wf, wv, wg, wz, wxbc, wga, wgb, wdt.reshape(D_MODEL, DT_W)], axis=1).astype(BF16)
    lb = p["lb_all"][l].reshape(HG_HEADS, 1, HG_DK)
    rw = jnp.pad(p["router_w"][l], ((0, 0), (0, LANES - N_EXPERTS)))
    rwh = rw.astype(BF16)
    rwl = (rw - rwh.astype(F32)).astype(BF16)
    return dict(
        w_in=w_in,
        la=jnp.log(lb), l1m=jnp.log1p(-lb), oml=1.0 - lb,
        hg_nw=p["hg_norm_w"][l].reshape(HG_HEADS, 1, HG_DV),
        hgp=p["hg_proj"][l].astype(BF16),
        conv_w=p["conv_w"][l], conv_b=p["conv_b"][l].reshape(1, CONV_DIM),
        dtb=_per_group(p["dt_bias"][l]),
        a=_per_group(-jnp.exp(p["a_log"][l].astype(F32))),
        dsk=jnp.repeat(p["d_skip"][l], SSD_HEADDIM).reshape(1, SSD_INNER),
        ssd_nw=p["ssd_norm_w"][l].reshape(1, SSD_INNER),
        ssp=p["ssd_proj"][l].astype(BF16),
        wo=p["w_out"][l].astype(BF16),
        ln1_g=p["ln1_g"][l].reshape(1, D_MODEL), ln1_b=p["ln1_b"][l].reshape(1, D_MODEL),
        rwh=rwh, rwl=rwl,
        rb=jnp.pad(p["router_b"][l], (0, LANES - N_EXPERTS)).reshape(1, LANES),
        ln2_g=p["ln2_g"][l].reshape(1, D_MODEL), ln2_b=p["ln2_b"][l].reshape(1, D_MODEL),
    )


def _prep_experts(p):
    return dict(
        wgu=p["w_gu"], bgu=p["b_gu"].reshape(DEPTH, N_EXPERTS, 1, 2 * D_FF),
        wd=p["w_down"], bd=p["b_down"].reshape(DEPTH, N_EXPERTS, 1, D_MODEL),
    )


def _moe(h, logits, lp, ep, li, tm):
    t = h.shape[0]
    s = t * TOP_K
    top_v, top_e = lax.top_k(logits[:, :N_EXPERTS], TOP_K)
    gates = jax.nn.softmax(top_v, axis=-1)
    e_slot = top_e.reshape(s).astype(jnp.int32)
    order = jnp.argsort(e_slot).astype(jnp.int32)
    counts = jnp.sum((e_slot[:, None] == jnp.arange(N_EXPERTS, dtype=jnp.int32)[None, :]).astype(jnp.int32), axis=0)
    padded = (counts + MOE_BM - 1) // MOE_BM * MOE_BM
    pend = jnp.cumsum(padded)
    cend = pend - padded + counts
    n_blocks = (s + N_EXPERTS * (MOE_BM - 1) + MOE_BM - 1) // MOE_BM
    nrows = n_blocks * MOE_BM
    rows = jnp.arange(nrows, dtype=jnp.int32)[:, None]
    before = pend[None, :] <= rows
    pad_before = jnp.sum(jnp.where(before, padded - counts, 0), axis=1)
    is_pad = jnp.any((cend[None, :] <= rows) & (rows < pend[None, :]), axis=1) | (rows[:, 0] >= pend[-1])
    slot = order[jnp.clip(rows[:, 0] - pad_before, 0, s - 1)]
    tok = slot // TOP_K
    row_tok = jnp.where(is_pad, 0, tok)
    row_dst = jnp.where(is_pad, TOP_K * t + rows[:, 0] % (2 * MOE_BM), (slot - tok * TOP_K) * t + tok)
    blk_rows = jnp.arange(n_blocks, dtype=jnp.int32)[:, None] * MOE_BM
    block_e = jnp.minimum(jnp.sum((pend[None, :] <= blk_rows).astype(jnp.int32), axis=1), N_EXPERTS - 1)
    n_used = (pend[-1:] // MOE_BM).astype(jnp.int32)
    first_prev = TOP_K * t + MOE_BM + jnp.arange(MOE_BM, dtype=jnp.int32)
    row_dst_prev = jnp.concatenate([first_prev, row_dst[:-MOE_BM]])
    ys = _moe_experts(block_e, n_used, row_tok.reshape(n_blocks, 1, MOE_BM), row_dst.reshape(n_blocks, 1, MOE_BM),
                      row_dst_prev.reshape(n_blocks, 1, MOE_BM), h, ep["wgu"], ep["bgu"], ep["wd"], ep["bd"], li)
    return _combine(h, ys, gates, lp["ln2_g"], lp["ln2_b"], tm)


def _mixers(xw, li, lp, grp):
    bsz, l, valid, base = grp["bsz"], grp["l"], grp["valid"], grp["base"]
    t = bsz * l
    prev8 = jnp.pad(grp["s_conv"][li], ((0, 0), (SUBLANES - (CONV_K - 1), 0), (0, 0)))
    xc = _conv(xw, prev8, lp["conv_w"], lp["conv_b"], bsz, l, base)
    tail = base + jnp.arange(bsz, dtype=jnp.int32)[:, None] * l + jnp.arange(valid - (CONV_K - 1), valid)[None, :]
    conv_new = xw[tail.reshape(-1)][:, OFF_XBC:OFF_XBC + CONV_DIM].reshape(bsz, CONV_K - 1, CONV_DIM)
    o, shg = _hgrn(xw, grp["s_hg"], li, lp["la"], lp["l1m"], lp["oml"], lp["hg_nw"], bsz, l, valid, base // ROW_TILE)
    y, hss = _ssd(xc.reshape(t // ROW_TILE, ROW_TILE, CONV_DIM), xw, grp["s_ssm"], li, lp["dtb"], lp["a"], lp["dsk"],
                  lp["ssd_nw"], bsz, l, valid, base // ROW_TILE)
    return o, y, shg, hss, conv_new


def _forward(x, groups, layers, ep):
    t = x.shape[0]
    xb = x.astype(BF16)
    new = [dict(hg=[], ssm=[], conv=[]) for _ in groups]
    for li, lp in enumerate(layers):
        xw = _matmul(xb, lp["w_in"], 512 if t % 512 == 0 else TOKEN_TILE, N_W // 4)
        mix =[_mixers(xw, li, lp, grp) for grp in groups]
        for n, m in zip(new, mix):
            n["hg"].append(m[2])
            n["ssm"].append(m[3])
            n["conv"].append(m[4])
        h, logits = _postmix(mix[0][0], mix[1][0], mix[0][1], mix[1][1], xw, x, lp["hgp"], lp["ssp"], lp["wo"],
                             lp["ln1_g"], lp["ln1_b"], lp["rwh"], lp["rwl"], lp["rb"])
        x, xb = _moe(h, logits, lp, ep, li, TOKEN_TILE)
    return x, [{k: jnp.stack(v) for k, v in n.items()} for n in new]


def kernel(x_prompt, x_sample, state_hgrn, state_ssm, state_conv, hg_lower_bounds, w_in, hg_norm_w, hg_proj,
           conv_w, conv_b, dt_bias, a_log, d_skip, ssd_norm_w, ssd_proj, w_out, ln1_g, ln1_b, router_w,
           router_b, w_gu, b_gu, w_down, b_down, ln2_g, ln2_b):
    lb_all = jnp.cumsum(jax.nn.softmax(hg_lower_bounds.astype(F32), axis=0), axis=0)
    lb_all = lb_all - lb_all[0]
    p = dict(lb_all=lb_all, w_in=w_in, hg_norm_w=hg_norm_w, hg_proj=hg_proj, conv_w=conv_w, conv_b=conv_b,
             dt_bias=dt_bias, a_log=a_log, d_skip=d_skip, ssd_norm_w=ssd_norm_w, ssd_proj=ssd_proj, w_out=w_out,
             ln1_g=ln1_g, ln1_b=ln1_b, router_w=router_w, router_b=router_b, w_gu=w_gu, b_gu=b_gu,
             w_down=w_down, b_down=b_down, ln2_g=ln2_g, ln2_b=ln2_b)
    layers = [_prep_layer(p, l) for l in range(DEPTH)]
    ep = _prep_experts(p)

    return _run(x_prompt, x_sample, state_hgrn, state_ssm, state_conv, layers, ep)


def _run(x_prompt, x_sample, state_hgrn, state_ssm, state_conv, layers, ep):
    bp, lprompt, _ = x_prompt.shape
    bs, ls, _ = x_sample.shape
    tp = bp * lprompt

    def pairs(s_ssm):
        return s_ssm.reshape(s_ssm.shape[:2] + (SSD_PAIRS, LANES, SSD_STATE))

    groups = [
        dict(bsz=bp, l=lprompt, valid=lprompt, base=0,
             s_hg=jnp.zeros((DEPTH, bp) + state_hgrn.shape[2:], F32),
             s_ssm=pairs(jnp.zeros((DEPTH, bp) + state_ssm.shape[2:], F32)),
             s_conv=jnp.zeros((DEPTH, bp) + state_conv.shape[2:], F32)),
        dict(bsz=bs, l=SUBLANES, valid=ls, base=tp, s_hg=state_hgrn, s_ssm=pairs(state_ssm), s_conv=state_conv),
    ]
    xs = jnp.pad(x_sample, ((0, 0), (0, SUBLANES - ls), (0, 0)))
    x = jnp.concatenate([x_prompt.reshape(tp, D_MODEL), xs.reshape(bs * SUBLANES, D_MODEL)], axis=0)
    x, new = _forward(x, groups, layers, ep)
    y_p = x[:tp].reshape(bp, lprompt, D_MODEL)
    y_s = x[tp:].reshape(bs, SUBLANES, D_MODEL)[:, :ls]
    (n_p, n_s) = new
    return (y_p, y_s, n_p["hg"], n_p["ssm"].reshape((DEPTH, bp) + state_ssm.shape[2:]), n_p["conv"],
            n_s["hg"], n_s["ssm"].reshape(state_ssm.shape), n_s["conv"])
```

```python
import functools

import jax
import jax.numpy as jnp
from jax import lax
from jax.experimental import pallas as pl
from jax.experimental.pallas import tpu as pltpu

F32 = jnp.float32
BF16 = jnp.bfloat16

D_MODEL = 1024
DEPTH = 2
HG_HEADS = 8
HG_DK = 128
HG_DV = 128
HG_DIM = HG_HEADS * HG_DK
SSD_INNER = 2 * D_MODEL
SSD_HEADDIM = 64
SSD_HEADS = SSD_INNER // SSD_HEADDIM
SSD_GROUPS = 4
SSD_HPG = SSD_HEADS // SSD_GROUPS
SSD_STATE = 128
SSD_PAIRS = SSD_HEADS // 2
PAIRS_PER_GROUP = SSD_PAIRS // SSD_GROUPS
GROUP_W = SSD_INNER // SSD_GROUPS
CONV_K = 4
CONV_DIM = SSD_INNER + 2 * SSD_GROUPS * SSD_STATE
N_EXPERTS = 32
TOP_K = 4
D_FF = D_MODEL
SWIGLU_LIMIT = 7.0
SWIGLU_ALPHA = 1.702
ALPHA = (2.0 * DEPTH) ** 0.25
LN_EPS = 1e-5
RMS_EPS = 1e-6

LANES = 128
SUBLANES = 8
ROW_SUB = D_MODEL // LANES

DT_W = SSD_GROUPS * LANES
OFF_Q = 0
OFF_F = HG_DIM
OFF_V = 2 * HG_DIM
OFF_G = 3 * HG_DIM
OFF_Z = 4 * HG_DIM
OFF_XBC = OFF_Z + SSD_INNER
OFF_GA = OFF_XBC + CONV_DIM
OFF_GB = OFF_GA + D_MODEL
OFF_DT = OFF_GB + D_MODEL
N_W = OFF_DT + DT_W
IN_SPLITS = (HG_DIM, HG_DIM, HG_DIM, HG_DIM, SSD_INNER, CONV_DIM, SSD_HEADS, D_MODEL, D_MODEL)

ROW_TILE = 128
CONV_ROWS = 256
TOKEN_TILE = 256
MOE_BM = 256
VMEM_LIMIT = 48 * 1024 * 1024
MOE_VMEM_LIMIT = 56 * 1024 * 1024
LOG2E = 1.4426950408889634
HALO = 16
HGRN_HEADS_PER_STEP = 4


def _cparams(sem):
    return pltpu.CompilerParams(dimension_semantics=sem, vmem_limit_bytes=VMEM_LIMIT)


def _sigmoid(x):
    return jax.nn.sigmoid(x)


def _dot(a, b):
    return jnp.dot(a, b, preferred_element_type=F32)


def _dot_nt(a, b):
    return lax.dot_general(a, b, (((1,), (1,)), ((), ())), preferred_element_type=F32)


def _dot_tn(a, b):
    return lax.dot_general(a, b, (((0,), (0,)), ((), ())), preferred_element_type=F32)


def _load_rows(ref3):
    return jnp.concatenate([ref3[:, s, :] for s in range(ROW_SUB)], axis=1)


def _store_rows(ref3, x):
    for s in range(ROW_SUB):
        ref3[:, s, :] = x[:, s * LANES:(s + 1) * LANES]


def _mm_kernel(x_ref, w_ref, o_ref):
    o_ref[...] = _dot(x_ref[...], w_ref[...])


def _matmul(x, w, tm, tn):
    m, k = x.shape
    n = w.shape[1]
    return pl.pallas_call(
        _mm_kernel,
        grid=(n // tn, m // tm),
        in_specs=[pl.BlockSpec((tm, k), lambda j, i: (i, 0)),
                  pl.BlockSpec((k, tn), lambda j, i: (0, j))],
        out_specs=pl.BlockSpec((tm, tn), lambda j, i: (i, j)),
        out_shape=jax.ShapeDtypeStruct((m, n), F32),
        compiler_params=_cparams(("arbitrary", "arbitrary")),
        name="in_proj",
    )(x, w)


def _conv_kernel(x_ref, pb_ref, p0_ref, w_ref, b_ref, o_ref):
    i = pl.program_id(1)
    x = x_ref[...]
    prev = jnp.where(i == 0, p0_ref[0], pb_ref[...])
    full = jnp.concatenate([prev, x], axis=0)
    w = w_ref[...]
    acc = b_ref[...] + w[CONV_K - 1:CONV_K] * x
    for s in range(1, CONV_K):
        xs = pltpu.roll(full, s, 0)[SUBLANES:]
        acc = acc + w[CONV_K - 1 - s:CONV_K - s] * xs
    o_ref[...] = acc * _sigmoid(acc)


def _conv_short_kernel(x_ref, p_ref, w_ref, b_ref, o_ref):
    x = x_ref[...]
    p = p_ref[...]
    rc = lax.broadcasted_iota(jnp.int32, x.shape, 0) & (SUBLANES - 1)
    w = w_ref[...]
    acc = b_ref[...] + w[CONV_K - 1:CONV_K] * x
    for s in range(1, CONV_K):
        xs = jnp.where(rc >= s, pltpu.roll(x, s, 0), pltpu.roll(p, (s - SUBLANES) % ROW_TILE, 0))
        acc = acc + w[CONV_K - 1 - s:CONV_K - s] * xs
    o_ref[...] = acc * _sigmoid(acc)


def _conv(xw, prev8, conv_w, conv_b, bsz, l, base_rows):
    cb = OFF_XBC // CONV_DIM
    wspec = [pl.BlockSpec((CONV_K, CONV_DIM), lambda *_: (0, 0)), pl.BlockSpec((1, CONV_DIM), lambda *_: (0, 0))]
    out_shape = jax.ShapeDtypeStruct((bsz * l, CONV_DIM), F32)
    if l == SUBLANES:
        r = ROW_TILE
        return pl.pallas_call(
            _conv_short_kernel,
            grid=(bsz * l // r,),
            in_specs=[pl.BlockSpec((r, CONV_DIM), lambda i: (base_rows // r + i, cb)),
                      pl.BlockSpec((r, CONV_DIM), lambda i: (i, 0))] + wspec,
            out_specs=pl.BlockSpec((r, CONV_DIM), lambda i: (i, 0)),
            out_shape=out_shape,
            compiler_params=_cparams(("arbitrary",)),
            name="conv_silu_short",
        )(xw, prev8.reshape(bsz * SUBLANES, CONV_DIM), conv_w, conv_b)
    r = CONV_ROWS
    tps = l // r
    rb = r // SUBLANES
    return pl.pallas_call(
        _conv_kernel,
        grid=(bsz, tps),
        in_specs=[pl.BlockSpec((r, CONV_DIM), lambda bi, i: (base_rows // r + bi * tps + i, cb)),
                  pl.BlockSpec((SUBLANES, CONV_DIM),
                               lambda bi, i: (jnp.maximum(base_rows // SUBLANES + (bi * tps + i) * rb - 1, 0), cb)),
                  pl.BlockSpec((1, SUBLANES, CONV_DIM), lambda bi, i: (bi, 0, 0))] + wspec,
        out_specs=pl.BlockSpec((r, CONV_DIM), lambda bi, i: (bi * tps + i, 0)),
        out_shape=out_shape,
        compiler_params=_cparams(("arbitrary", "arbitrary")),
        name="conv_silu",
    )(xw, xw, prev8, conv_w, conv_b)


def _seg_cumsum(x, rc, seg):
    s = 1
    while s < seg:
        x = x + jnp.where(rc >= s, pltpu.roll(x, s, 0), 0.0)
        s *= 2
    return x


def _pad_rows(x, rows):
    if x.shape[0] == rows:
        return x
    return jnp.concatenate([x, jnp.zeros((rows - x.shape[0], x.shape[1]), x.dtype)], axis=0)


def _split3(x):
    hi = x.astype(BF16)
    r1 = x - hi.astype(F32)
    mid = r1.astype(BF16)
    lo = (r1 - mid.astype(F32)).astype(BF16)
    return hi, mid, lo


def _hgrn_kernel(*refs, hps, nseq, nc, **kw):
    s0_ref, sout_ref, st_ref = refs[8], refs[10], refs[11]
    c = pl.program_id(2)

    @pl.when(c == 0)
    def _():
        for hh in range(hps):
            for i in range(nseq):
                st_ref[hh, i] = s0_ref[0, i, hh].T

    for hh in range(hps):
        _hgrn_head(hh, *refs, nseq=nseq, **kw)

    @pl.when(c == nc - 1)
    def _():
        for hh in range(hps):
            for i in range(nseq):
                sout_ref[i, hh] = st_ref[hh, i].T


def _hgrn_head(hh, q_ref, f_ref, v_ref, g_ref, la_ref, l1m_ref, oml_ref, nw_ref, s0_ref,
               o_ref, sout_ref, st_ref, k_scr, b_scr, *, ch, sub, nseq, valid):
    r = ROW_TILE
    cols = slice(hh * LANES, (hh + 1) * LANES)
    q = q_ref[:, cols]
    fr = f_ref[:, cols]
    v = v_ref[:, cols]
    q = q * _sigmoid(q)
    la = la_ref[hh]
    lsig = jnp.minimum(fr, 0.0) - jnp.log1p(jnp.exp(-jnp.abs(fr)))
    cc = l1m_ref[hh] + lsig
    logf = jnp.maximum(la, cc) + jnp.log1p(jnp.exp(-jnp.abs(la - cc)))
    k = oml_ref[hh] * _sigmoid(-fr)

    row = lax.broadcasted_iota(jnp.int32, (r, LANES), 0)
    lane = lax.broadcasted_iota(jnp.int32, (r, LANES), 1)
    if valid < ch:
        pad = (row & (ch - 1)) >= valid
        logf = jnp.where(pad, 0.0, logf)
        k = jnp.where(pad, 0.0, k)

    tri = lane <= row
    if ch < r:
        shift = ch.bit_length() - 1
        tri = tri & ((row >> shift) == (lane >> shift))
    tri = jnp.where(tri, 1.0, 0.0).astype(BF16)
    hi, mid, lo = _split3(logf)
    b2 = (_dot(tri, hi) + _dot(tri, mid) + _dot(tri, lo)) * LOG2E

    k_scr[hh, 0:HALO] = jnp.zeros((HALO, LANES), F32)
    b_scr[hh, 0:HALO] = jnp.zeros((HALO, LANES), F32)
    k_scr[hh, HALO:HALO + r] = k
    b_scr[hh, HALO:HALO + r] = b2
    rs = row & (sub - 1)
    scl = jnp.where(lane == rs, jnp.sum(q * k, axis=-1, keepdims=True), 0.0)
    for d in range(1, sub):
        kd = k_scr[hh, pl.ds(HALO - d, r), :]
        bd = b_scr[hh, pl.ds(HALO - d, r), :]
        sd = jnp.sum(q * kd * jnp.exp2(b2 - bd), axis=-1, keepdims=True)
        scl = jnp.where(lane == rs - d, sd, scl)

    spc = ch // sub
    pieces = []
    for m in range(r // sub):
        lo_, hi_ = m * sub, (m + 1) * sub
        piece = scl[lo_:hi_]
        if m:
            piece = pltpu.roll(piece, lo_, 1)
        j = m % spc
        if j:
            cs = (m - j) * sub
            ref_b = b2[lo_ - 1:lo_]
            qj = q[lo_:hi_] * jnp.exp2(b2[lo_:hi_] - ref_b)
            kk = k[cs:lo_] * jnp.exp2(ref_b - b2[cs:lo_])
            parts = ([jnp.zeros((cs, LANES), F32)] if cs else []) + [kk, jnp.zeros((r - lo_, LANES), F32)]
            piece = piece + _dot_nt(qj.astype(BF16), jnp.concatenate(parts, axis=0).astype(BF16))
        pieces.append(piece)
    sc = jnp.concatenate(pieces, axis=0)
    o = _dot(sc.astype(BF16), v.astype(BF16))

    inter = []
    for i in range(nseq):
        lo_, hi_ = i * ch, (i + 1) * ch
        bi = b2[lo_:hi_]
        bl = bi[ch - 1:ch]
        st = st_ref[hh, i]
        inter.append(_dot_nt((q[lo_:hi_] * jnp.exp2(bi)).astype(BF16), st.astype(BF16)))
        kh = _pad_rows(k[lo_:hi_] * jnp.exp2(bl - bi), r)
        vi = _pad_rows(v[lo_:hi_], r)
        st_ref[hh, i] = st * jnp.exp2(bl) + _dot_tn(vi.astype(BF16), kh.astype(BF16))
    o = o + (inter[0] if nseq == 1 else jnp.concatenate(inter, axis=0))

    ms = jnp.mean(o * o, axis=-1, keepdims=True)
    o_ref[:, cols] = o * lax.rsqrt(ms + RMS_EPS) * nw_ref[hh] * _sigmoid(g_ref[:, cols])


def _seq_tiling(bsz, l):
    r = ROW_TILE
    ch = min(l, r)
    nseq = r // ch
    nc = max(l // r, 1)
    nb = bsz * l // (r * nc)
    return ch, nseq, nc, nb


def _hgrn(xw, s0_all, li, la, l1m, oml, nw, bsz, l, valid, base_tile):
    r = ROW_TILE
    ch, nseq, nc, nb = _seq_tiling(bsz, l)
    sub = min(ch, HALO)
    hps = HGRN_HEADS_PER_STEP
    w = hps * LANES
    kern = functools.partial(_hgrn_kernel, hps=hps, ch=ch, sub=sub, nseq=nseq, valid=valid, nc=nc)

    def xspec(off):
        return pl.BlockSpec((r, w), lambda bb, h, c: (base_tile + bb * nc + c, off // w + h))

    def pspec():
        return pl.BlockSpec((hps, 1, LANES), lambda bb, h, c: (h, 0, 0))

    return pl.pallas_call(
        kern,
        grid=(nb, HG_HEADS // hps, nc),
        in_specs=[xspec(OFF_Q), xspec(OFF_F), xspec(OFF_V), xspec(OFF_G),
                  pspec(), pspec(), pspec(), pspec(),
                  pl.BlockSpec((1, nseq, hps, HG_DK, HG_DV), lambda bb, h, c: (li, bb, h, 0, 0))],
        out_specs=[pl.BlockSpec((r, w), lambda bb, h, c: (bb * nc + c, h)),
                   pl.BlockSpec((nseq, hps, HG_DK, HG_DV), lambda bb, h, c: (bb, h, 0, 0))],
        out_shape=[jax.ShapeDtypeStruct((bsz * l, HG_DIM), F32),
                   jax.ShapeDtypeStruct(s0_all.shape[1:], F32)],
        scratch_shapes=[pltpu.VMEM((hps, nseq, HG_DV, HG_DK), F32),
                        pltpu.VMEM((hps, HALO + r, LANES), F32), pltpu.VMEM((hps, HALO + r, LANES), F32)],
        compiler_params=_cparams(("arbitrary", "arbitrary", "arbitrary")),
        name="hgrn2",
    )(xw, xw, xw, xw, la, l1m, oml, nw, s0_all)


def _ssd_kernel(xs_ref, bm_ref, cm_ref, z_ref, dt_ref, dtb_ref, a_ref, dsk_ref, nw_ref, h0_ref,
                y_ref, hout_ref, h_ref, al_scr, dl_scr, *, ch, nseq, valid, nc):
    r = ROW_TILE
    c = pl.program_id(2)

    @pl.when(c == 0)
    def _():
        h_ref[...] = h0_ref[0]

    shift = ch.bit_length() - 1
    rowl = lax.broadcasted_iota(jnp.int32, (r, LANES), 0)
    lane = lax.broadcasted_iota(jnp.int32, (r, LANES), 1)
    rc = rowl & (ch - 1)
    dtr = dt_ref[...] + dtb_ref[...]
    dt = jnp.maximum(dtr, 0.0) + jnp.log1p(jnp.exp(-jnp.abs(dtr)))
    if valid < ch:
        dt = jnp.where(rc < valid, dt, 0.0)
    acum = _seg_cumsum(dt * a_ref[...], rc, ch)
    acum_t = acum.T
    dt_t = dt.T

    mask = (lane <= rowl) & ((rowl >> shift) == (lane >> shift))
    even_lane = lane < SSD_HEADDIM
    even_row = rowl < SSD_HEADDIM
    last_row = rc == ch - 1

    bmb = bm_ref[0].astype(BF16)
    cmb = cm_ref[0].astype(BF16)
    cb = _dot_nt(cmb, bmb)
    xs = xs_ref[0]
    z = z_ref[...]
    for pp in range(PAIRS_PER_GROUP):
        xp = xs[:, pp * LANES:(pp + 1) * LANES]
        acc = dsk_ref[:, pp * LANES:(pp + 1) * LANES] * xp
        for e in range(2):
            hd = 2 * pp + e
            w = cb * jnp.exp(jnp.where(mask, acum[:, hd:hd + 1] - acum_t[hd:hd + 1, :], -jnp.inf)) * dt_t[hd:hd + 1, :]
            xm = jnp.where(even_lane if e == 0 else jnp.logical_not(even_lane), xp, 0.0)
            acc = acc + _dot(w.astype(BF16), xm.astype(BF16))
        a_lane = jnp.where(even_lane, acum[:, 2 * pp:2 * pp + 1], acum[:, 2 * pp + 1:2 * pp + 2])
        dt_lane = jnp.where(even_lane, dt[:, 2 * pp:2 * pp + 1], dt[:, 2 * pp + 1:2 * pp + 2])
        ea = jnp.exp(a_lane)

        def seq_body(i, acc, pp=pp, xp=xp, a_lane=a_lane, dt_lane=dt_lane, ea=ea):
            in_seq = (rowl >> shift) == i
            al_lane = jnp.sum(jnp.where(in_seq & last_row, a_lane, 0.0), axis=0, keepdims=True)
            hp = h_ref[i, pp]
            yi = _dot_nt(cmb, hp.astype(BF16)) * ea
            acc = acc + jnp.where(in_seq, yi, 0.0)
            xw = jnp.where(in_seq, xp * (jnp.exp(al_lane - a_lane) * dt_lane), 0.0)
            dec = jnp.where(even_row, jnp.exp(al_lane[:, 0:1]), jnp.exp(al_lane[:, LANES - 1:LANES]))
            h_ref[i, pp] = dec * hp + _dot_tn(xw.astype(BF16), bmb)
            return acc

        if nseq == 1:
            acc = seq_body(0, acc)
        else:
            al_scr[pp] = a_lane
            dl_scr[pp] = dt_lane
        y_ref[:, pp * LANES:(pp + 1) * LANES] = acc

    if nseq > 1:
        def short_body(i, carry):
            rows = pl.ds(pl.multiple_of(i * ch, ch), ch)
            cm_i = _pad_rows(cm_ref[0, rows, :], 2 * SUBLANES).astype(BF16)
            bm_i = _pad_rows(bm_ref[0, rows, :], 2 * SUBLANES).astype(BF16)
            for pp in range(PAIRS_PER_GROUP):
                cols = slice(pp * LANES, (pp + 1) * LANES)
                a_i = al_scr[pp, rows, :]
                al = a_i[ch - 1:ch]
                hp = h_ref[i, pp]
                yi = _dot_nt(cm_i, hp.astype(BF16))[:ch] * jnp.exp(a_i)
                y_ref[rows, cols] = y_ref[rows, cols] + yi
                xw = xs_ref[0, rows, cols] * (jnp.exp(al - a_i) * dl_scr[pp, rows, :])
                dec = jnp.where(even_row, jnp.exp(al[:, 0:1]), jnp.exp(al[:, LANES - 1:LANES]))
                h_ref[i, pp] = dec * hp + _dot_tn(_pad_rows(xw, 2 * SUBLANES).astype(BF16), bm_i)
            return carry
        lax.fori_loop(0, nseq, short_body, 0)

    y = y_ref[...] * (z * _sigmoid(z))
    ms = jnp.mean(y * y, axis=-1, keepdims=True)
    y_ref[...] = y * lax.rsqrt(ms + RMS_EPS) * nw_ref[...]

    @pl.when(c == nc - 1)
    def _():
        hout_ref[...] = h_ref[...]


def _ssd(xc3, xw, h0_all, li, dtb, a, dsk, nw, bsz, l, valid, base_tile):
    r = ROW_TILE
    ch, nseq, nc, nb = _seq_tiling(bsz, l)
    kern = functools.partial(_ssd_kernel, ch=ch, nseq=nseq, valid=valid, nc=nc)
    hspec = pl.BlockSpec((nseq, PAIRS_PER_GROUP, LANES, SSD_STATE), lambda bb, g, c: (bb, g, 0, 0))
    h0spec = pl.BlockSpec((1, nseq, PAIRS_PER_GROUP, LANES, SSD_STATE), lambda bb, g, c: (li, bb, g, 0, 0))
    b_off = SSD_INNER // SSD_STATE
    c_off = b_off + SSD_GROUPS

    def vec(w):
        return pl.BlockSpec((1, w), lambda bb, g, c: (0, g))

    return pl.pallas_call(
        kern,
        grid=(nb, SSD_GROUPS, nc),
        in_specs=[pl.BlockSpec((1, r, GROUP_W), lambda bb, g, c: (bb * nc + c, 0, g)),
                  pl.BlockSpec((1, r, SSD_STATE), lambda bb, g, c: (bb * nc + c, 0, b_off + g)),
                  pl.BlockSpec((1, r, SSD_STATE), lambda bb, g, c: (bb * nc + c, 0, c_off + g)),
                  pl.BlockSpec((r, GROUP_W), lambda bb, g, c: (base_tile + bb * nc + c, OFF_Z // GROUP_W + g)),
                  pl.BlockSpec((r, LANES), lambda bb, g, c: (base_tile + bb * nc + c, OFF_DT // LANES + g)),
                  vec(LANES), vec(LANES), vec(GROUP_W), vec(GROUP_W), h0spec],
        out_specs=[pl.BlockSpec((r, GROUP_W), lambda bb, g, c: (bb * nc + c, g)), hspec],
        out_shape=[jax.ShapeDtypeStruct((bsz * l, SSD_INNER), F32),
                   jax.ShapeDtypeStruct(h0_all.shape[1:], F32)],
        scratch_shapes=[pltpu.VMEM((nseq, PAIRS_PER_GROUP, LANES, SSD_STATE), F32),
                        pltpu.VMEM((PAIRS_PER_GROUP, r, LANES), F32), pltpu.VMEM((PAIRS_PER_GROUP, r, LANES), F32)],
        compiler_params=_cparams(("arbitrary", "arbitrary", "arbitrary")),
        name="ssd",
    )(xc3, xc3, xc3, xw, xw, dtb, a, dsk, nw, h0_all)


def _layer_norm(x, g, b):
    mu = jnp.mean(x, axis=-1, keepdims=True)
    xc = x - mu
    var = jnp.mean(xc * xc, axis=-1, keepdims=True)
    return xc * lax.rsqrt(var + LN_EPS) * g + b


def _postmix_kernel(op_ref, os_ref, yp_ref, ys_ref, ga_ref, gb_ref, x_ref, hgp_ref, ssp_ref, wo_ref, g_ref, b_ref,
                    rwh_ref, rwl_ref, rb_ref, h_ref, h3_ref, lg_ref, *, n_p):
    first = pl.program_id(0) < n_p
    o = jnp.where(first, op_ref[...], os_ref[...])
    y = jnp.where(first, yp_ref[...], ys_ref[...])
    out_a = _dot(o.astype(BF16), hgp_ref[...])
    out_b = _dot(y.astype(BF16), ssp_ref[...])
    merged = _sigmoid(ga_ref[...]) * out_a + _sigmoid(gb_ref[...]) * out_b
    mix = _dot(merged.astype(BF16), wo_ref[...])
    h = _layer_norm(ALPHA * x_ref[...] + mix, g_ref[...], b_ref[...])
    h_ref[...] = h
    _store_rows(h3_ref, h)
    hh = h.astype(BF16)
    hl = (h - hh.astype(F32)).astype(BF16)
    lg_ref[...] = (_dot(hh, rwh_ref[...]) + _dot(hl, rwh_ref[...]) + _dot(hh, rwl_ref[...])) + rb_ref[...]


def _postmix(o_p, o_s, y_p, y_s, xw, x, hgp, ssp, wo, g, b, rwh, rwl, rb):
    t = x.shape[0]
    tm = TOKEN_TILE
    n_p = o_p.shape[0] // tm

    def full(a):
        return pl.BlockSpec(a.shape, lambda i: (0, 0))

    def first(w):
        return pl.BlockSpec((tm, w), lambda i: (jnp.minimum(i, n_p - 1), 0))

    def second(w):
        return pl.BlockSpec((tm, w), lambda i: (jnp.maximum(i - n_p, 0), 0))

    return pl.pallas_call(
        functools.partial(_postmix_kernel, n_p=n_p),
        grid=(t // tm,),
        in_specs=[first(HG_DIM), second(HG_DIM), first(SSD_INNER), second(SSD_INNER),
                  pl.BlockSpec((tm, D_MODEL), lambda i: (i, OFF_GA // D_MODEL)),
                  pl.BlockSpec((tm, D_MODEL), lambda i: (i, OFF_GB // D_MODEL)),
                  pl.BlockSpec((tm, D_MODEL), lambda i: (i, 0)),
                  full(hgp), full(ssp), full(wo), full(g), full(b), full(rwh), full(rwl), full(rb)],
        out_specs=[pl.BlockSpec((tm, D_MODEL), lambda i: (i, 0)),
                   pl.BlockSpec((tm, ROW_SUB, LANES), lambda i: (i, 0, 0)),
                   pl.BlockSpec((tm, LANES), lambda i: (i, 0))],
        out_shape=[jax.ShapeDtypeStruct((t, D_MODEL), F32),
                   jax.ShapeDtypeStruct((t, ROW_SUB, LANES), F32),
                   jax.ShapeDtypeStruct((t, LANES), F32)],
        compiler_params=_cparams(("arbitrary",)),
        name="postmix",
    )(o_p, o_s, y_p, y_s, xw, xw, x, hgp, ssp, wo, g, b, rwh, rwl, rb)


def _moe_kernel(be_ref, nu_ref, tokc_ref, tokn_ref, dstp_ref, dstc_ref, h_hbm, wgu_ref, bgu_ref, wd_ref, bd_ref,
                ys_hbm, xbuf, obuf, wgu_b, wd_b, gsem, ssem, *, nblk, trash_row):
    j = pl.program_id(0)
    nu = nu_ref[0]
    slot = j & 1
    nslot = 1 - slot

    def gather_start(tok_ref, s, i):
        pltpu.make_async_copy(h_hbm.at[pl.ds(tok_ref[0, 0, i], 1)], xbuf.at[s, pl.ds(i, 1)], gsem.at[s]).start()

    def scatter_start(dst_ref, s, i):
        pltpu.make_async_copy(obuf.at[s, pl.ds(i, 1)], ys_hbm.at[pl.ds(dst_ref[0, 0, i], 1)], ssem.at[s]).start()

    def gather_wait(s):
        pltpu.make_async_copy(h_hbm.at[pl.ds(0, MOE_BM)], xbuf.at[s], gsem.at[s]).wait()

    def scatter_wait(s):
        pltpu.make_async_copy(obuf.at[s], ys_hbm.at[pl.ds(0, MOE_BM)], ssem.at[s]).wait()

    @pl.when(j == 0)
    def _():
        obuf[...] = jnp.zeros(obuf.shape, F32)
        pltpu.make_async_copy(obuf.at[0], ys_hbm.at[pl.ds(trash_row, MOE_BM)], ssem.at[0]).start()

        def body(i, carry):
            gather_start(tokc_ref, 0, i)
            return carry
        lax.fori_loop(0, MOE_BM, body, 0, unroll=8)

    first_of_expert = (j == 0) | (be_ref[j] != be_ref[jnp.maximum(j - 1, 0)])

    @pl.when((j < nu) & first_of_expert)
    def _():
        rows = D_MODEL // 8
        for c in range(8):
            wgu_b[c * rows:(c + 1) * rows] = wgu_ref[0, 0, c * rows:(c + 1) * rows].astype(BF16)
            wd_b[c * rows:(c + 1) * rows] = wd_ref[0, 0, c * rows:(c + 1) * rows].astype(BF16)

    @pl.when(j < nu)
    def _():
        gather_wait(slot)
        scatter_wait(slot)
        for i in range(MOE_BM):
            gather_start(tokn_ref, nslot, i)
            scatter_start(dstp_ref, nslot, i)
        gu = _dot(_load_rows(xbuf.at[slot]).astype(BF16), wgu_b[...]) + bgu_ref[0, 0]
        gate = jnp.minimum(gu[:, :D_FF], SWIGLU_LIMIT)
        up = jnp.clip(gu[:, D_FF:], -SWIGLU_LIMIT, SWIGLU_LIMIT)
        act = (up + 1.0) * gate * _sigmoid(SWIGLU_ALPHA * gate)
        _store_rows(obuf.at[slot], _dot(act.astype(BF16), wd_b[...]) + bd_ref[0, 0])

    @pl.when(j == nu - 1)
    def _():
        def body(i, carry):
            scatter_start(dstc_ref, slot, i)
            return carry
        lax.fori_loop(0, MOE_BM, body, 0, unroll=8)

    @pl.when(j == nblk - 1)
    def _():
        scatter_wait((nu - 1) & 1)
        scatter_wait(nu & 1)
        gather_wait(nu & 1)


def _moe_experts(block_e, n_used, row_tok, row_dst, row_dst_prev, h, wgu, bgu, wd, bd, li):
    t = h.shape[0]
    nblk = row_tok.shape[0]
    trash_row = TOP_K * t
    kern = functools.partial(_moe_kernel, nblk=nblk, trash_row=trash_row)

    def smem(imap):
        return pl.BlockSpec((1, 1, MOE_BM), imap, memory_space=pltpu.SMEM)

    grid_spec = pltpu.PrefetchScalarGridSpec(
        num_scalar_prefetch=2,
        grid=(nblk,),
        in_specs=[smem(lambda j, be, nu: (j, 0, 0)),
                  smem(lambda j, be, nu: (jnp.minimum(j + 1, nblk - 1), 0, 0)),
                  smem(lambda j, be, nu: (j, 0, 0)),
                  smem(lambda j, be, nu: (j, 0, 0)),
                  pl.BlockSpec(memory_space=pl.ANY),
                  pl.BlockSpec((1, 1, D_MODEL, 2 * D_FF), lambda j, be, nu: (li, be[j], 0, 0)),
                  pl.BlockSpec((1, 1, 1, 2 * D_FF), lambda j, be, nu: (li, be[j], 0, 0)),
                  pl.BlockSpec((1, 1, D_FF, D_MODEL), lambda j, be, nu: (li, be[j], 0, 0)),
                  pl.BlockSpec((1, 1, 1, D_MODEL), lambda j, be, nu: (li, be[j], 0, 0))],
        out_specs=pl.BlockSpec(memory_space=pl.ANY),
        scratch_shapes=[pltpu.VMEM((2, MOE_BM, ROW_SUB, LANES), F32), pltpu.VMEM((2, MOE_BM, ROW_SUB, LANES), F32),
                        pltpu.VMEM((D_MODEL, 2 * D_FF), BF16), pltpu.VMEM((D_FF, D_MODEL), BF16),
                        pltpu.SemaphoreType.DMA((2,)), pltpu.SemaphoreType.DMA((2,))],
    )
    return pl.pallas_call(
        kern,
        grid_spec=grid_spec,
        out_shape=jax.ShapeDtypeStruct((trash_row + 2 * MOE_BM, ROW_SUB, LANES), F32),
        compiler_params=pltpu.CompilerParams(dimension_semantics=("arbitrary",), vmem_limit_bytes=MOE_VMEM_LIMIT),
        name="moe_experts",
    )(block_e, n_used, row_tok, row_tok, row_dst_prev, row_dst, h, wgu, bgu, wd, bd)


def _combine_kernel(h_ref, y0_ref, y1_ref, y2_ref, y3_ref, gt_ref, g_ref, b_ref, x_ref, xb_ref):
    gt = gt_ref[...]
    y = gt[:, 0:1] * _load_rows(y0_ref)
    for kk, y_ref in enumerate((y1_ref, y2_ref, y3_ref), start=1):
        y = y + gt[:, kk:kk + 1] * _load_rows(y_ref)
    x = _layer_norm(ALPHA * h_ref[...] + y, g_ref[...], b_ref[...])
    x_ref[...] = x
    xb_ref[...] = x.astype(BF16)


def _combine(h, ys, gates, g, b, tm):
    t = h.shape[0]
    nt = t // tm

    def yspec(kk):
        return pl.BlockSpec((tm, ROW_SUB, LANES), lambda i: (kk * nt + i, 0, 0))

    return pl.pallas_call(
        _combine_kernel,
        grid=(nt,),
        in_specs=[pl.BlockSpec((tm, D_MODEL), lambda i: (i, 0)),
                  yspec(0), yspec(1), yspec(2), yspec(3),
                  pl.BlockSpec((tm, TOP_K), lambda i: (i, 0)),
                  pl.BlockSpec((1, D_MODEL), lambda i: (0, 0)),
                  pl.BlockSpec((1, D_MODEL), lambda i: (0, 0))],
        out_specs=[pl.BlockSpec((tm, D_MODEL), lambda i: (i, 0)),
                   pl.BlockSpec((tm, D_MODEL), lambda i: (i, 0))],
        out_shape=[jax.ShapeDtypeStruct((t, D_MODEL), F32),
                   jax.ShapeDtypeStruct((t, D_MODEL), BF16)],
        compiler_params=_cparams(("arbitrary",)),
        name="combine_ln",
    )(h, ys, ys, ys, ys, gates, g, b)


def _split_cols(a, sizes):
    out, off = [], 0
    for s in sizes:
        out.append(a[..., off:off + s])
        off += s
    return out


def _per_group(vec_heads):
    v = vec_heads.reshape(SSD_GROUPS, SSD_HPG)
    return jnp.pad(v, ((0, 0), (0, LANES - SSD_HPG))).reshape(1, DT_W)


def _prep_layer(p, l):
    wq, wf, wv, wg, wz, wxbc, wdt, wga, wgb = _split_cols(p["w_in"][l], IN_SPLITS)
    wdt = jnp.pad(wdt.reshape(D_MODEL, SSD_GROUPS, SSD_HPG), ((0, 0), (0, 0), (0, LANES - SSD_HPG)))
    w_in = jnp.concatenate([wq, wf, wv, wg, wz, wxbc, wga, wgb, wdt.reshape(D_MODEL, DT_W)], axis=1).astype(BF16)
    lb = p["lb_all"][l].reshape(HG_HEADS, 1, HG_DK)
    rw = jnp.pad(p["router_w"][l], ((0, 0), (0, LANES - N_EXPERTS)))
    rwh = rw.astype(BF16)
    rwl = (rw - rwh.astype(F32)).astype(BF16)
    return dict(
        w_in=w_in,
        la=jnp.log(lb), l1m=jnp.log1p(-lb), oml=1.0 - lb,
        hg_nw=p["hg_norm_w"][l].reshape(HG_HEADS, 1, HG_DV),
        hgp=p["hg_proj"][l].astype(BF16),
        conv_w=p["conv_w"][l], conv_b=p["conv_b"][l].reshape(1, CONV_DIM),
        dtb=_per_group(p["dt_bias"][l]),
        a=_per_group(-jnp.exp(p["a_log"][l].astype(F32))),
        dsk=jnp.repeat(p["d_skip"][l], SSD_HEADDIM).reshape(1, SSD_INNER),
        ssd_nw=p["ssd_norm_w"][l].reshape(1, SSD_INNER),
        ssp=p["ssd_proj"][l].astype(BF16),
        wo=p["w_out"][l].astype(BF16),
        ln1_g=p["ln1_g"][l].reshape(1, D_MODEL), ln1_b=p["ln1_b"][l].reshape(1, D_MODEL),
        rwh=rwh, rwl=rwl,
        rb=jnp.pad(p["router_b"][l], (0, LANES - N_EXPERTS)).reshape(1, LANES),
        ln2_g=p["ln2_g"][l].reshape(1, D_MODEL), ln2_b=p["ln2_b"][l].reshape(1, D_MODEL),
    )


def _prep_experts(p):
    return dict(
        wgu=p["w_gu"], bgu=p["b_gu"].reshape(DEPTH, N_EXPERTS, 1, 2 * D_FF),
        wd=p["w_down"], bd=p["b_down"].reshape(DEPTH, N_EXPERTS, 1, D_MODEL),
    )


def _moe(h, h3, logits, lp, ep, li, tm):
    t = h.shape[0]
    s = t * TOP_K
    top_v, top_e = lax.top_k(logits[:, :N_EXPERTS], TOP_K)
    gates = jax.nn.softmax(top_v, axis=-1)
    e_slot = top_e.reshape(s).astype(jnp.int32)
    order = jnp.argsort(e_slot).astype(jnp.int32)
    counts = jnp.sum((e_slot[:, None] == jnp.arange(N_EXPERTS, dtype=jnp.int32)[None, :]).astype(jnp.int32), axis=0)
    padded = (counts + MOE_BM - 1) // MOE_BM * MOE_BM
    pend = jnp.cumsum(padded)
    cend = pend - padded + counts
    n_blocks = (s + N_EXPERTS * (MOE_BM - 1) + MOE_BM - 1) // MOE_BM
    nrows = n_blocks * MOE_BM
    rows = jnp.arange(nrows, dtype=jnp.int32)[:, None]
    before = pend[None, :] <= rows
    pad_before = jnp.sum(jnp.where(before, padded - counts, 0), axis=1)
    is_pad = jnp.any((cend[None, :] <= rows) & (rows < pend[None, :]), axis=1) | (rows[:, 0] >= pend[-1])
    slot = order[jnp.clip(rows[:, 0] - pad_before, 0, s - 1)]
    tok = slot // TOP_K
    row_tok = jnp.where(is_pad, 0, tok)
    row_dst = jnp.where(is_pad, TOP_K * t + rows[:, 0] % (2 * MOE_BM), (slot - tok * TOP_K) * t + tok)
    blk_rows = jnp.arange(n_blocks, dtype=jnp.int32)[:, None] * MOE_BM
    block_e = jnp.minimum(jnp.sum((pend[None, :] <= blk_rows).astype(jnp.int32), axis=1), N_EXPERTS - 1)
    n_used = (pend[-1:] // MOE_BM).astype(jnp.int32)
    first_prev = TOP_K * t + MOE_BM + jnp.arange(MOE_BM, dtype=jnp.int32)
    row_dst_prev = jnp.concatenate([first_prev, row_dst[:-MOE_BM]])
    ys = _moe_experts(block_e, n_used, row_tok.reshape(n_blocks, 1, MOE_BM), row_dst.reshape(n_blocks, 1, MOE_BM),
                      row_dst_prev.reshape(n_blocks, 1, MOE_BM), h3, ep["wgu"], ep["bgu"], ep["wd"], ep["bd"], li)
    return _combine(h, ys, gates, lp["ln2_g"], lp["ln2_b"], tm)


def _mixers(xw, li, lp, grp):
    bsz, l, valid, base = grp["bsz"], grp["l"], grp["valid"], grp["base"]
    t = bsz * l
    prev8 = jnp.pad(grp["s_conv"][li], ((0, 0), (SUBLANES - (CONV_K - 1), 0), (0, 0)))
    xc = _conv(xw, prev8, lp["conv_w"], lp["conv_b"], bsz, l, base)
    tail = base + jnp.arange(bsz, dtype=jnp.int32)[:, None] * l + jnp.arange(valid - (CONV_K - 1), valid)[None, :]
    conv_new = xw[tail.reshape(-1)][:, OFF_XBC:OFF_XBC + CONV_DIM].reshape(bsz, CONV_K - 1, CONV_DIM)
    o, shg = _hgrn(xw, grp["s_hg"], li, lp["la"], lp["l1m"], lp["oml"], lp["hg_nw"], bsz, l, valid, base // ROW_TILE)
    y, hss = _ssd(xc.reshape(t // ROW_TILE, ROW_TILE, CONV_DIM), xw, grp["s_ssm"], li, lp["dtb"], lp["a"], lp["dsk"],
                  lp["ssd_nw"], bsz, l, valid, base // ROW_TILE)
    return o, y, shg, hss, conv_new


def _forward(x, groups, layers, ep):
    t = x.shape[0]
    xb = x.astype(BF16)
    new = [dict(hg=[], ssm=[], conv=[]) for _ in groups]
    for li, lp in enumerate(layers):
        xw = _matmul(xb, lp["w_in"], 512 if t % 512 == 0 else TOKEN_TILE, N_W // 4)
        mix =[_mixers(xw, li, lp, grp) for grp in groups]
        for n, m in zip(new, mix):
            n["hg"].append(m[2])
            n["ssm"].append(m[3])
            n["conv"].append(m[4])
        h, h3, logits = _postmix(mix[0][0], mix[1][0], mix[0][1], mix[1][1], xw, x, lp["hgp"], lp["ssp"], lp["wo"],
                             lp["ln1_g"], lp["ln1_b"], lp["rwh"], lp["rwl"], lp["rb"])
        x, xb = _moe(h, h3, logits, lp, ep, li, TOKEN_TILE)
    return x, [{k: jnp.stack(v) for k, v in n.items()} for n in new]


def kernel(x_prompt, x_sample, state_hgrn, state_ssm, state_conv, hg_lower_bounds, w_in, hg_norm_w, hg_proj,
           conv_w, conv_b, dt_bias, a_log, d_skip, ssd_norm_w, ssd_proj, w_out, ln1_g, ln1_b, router_w,
           router_b, w_gu, b_gu, w_down, b_down, ln2_g, ln2_b):
    lb_all = jnp.cumsum(jax.nn.softmax(hg_lower_bounds.astype(F32), axis=0), axis=0)
    lb_all = lb_all - lb_all[0]
    p = dict(lb_all=lb_all, w_in=w_in, hg_norm_w=hg_norm_w, hg_proj=hg_proj, conv_w=conv_w, conv_b=conv_b,
             dt_bias=dt_bias, a_log=a_log, d_skip=d_skip, ssd_norm_w=ssd_norm_w, ssd_proj=ssd_proj, w_out=w_out,
             ln1_g=ln1_g, ln1_b=ln1_b, router_w=router_w, router_b=router_b, w_gu=w_gu, b_gu=b_gu,
             w_down=w_down, b_down=b_down, ln2_g=ln2_g, ln2_b=ln2_b)
    layers = [_prep_layer(p, l) for l in range(DEPTH)]
    ep = _prep_experts(p)

    return _run(x_prompt, x_sample, state_hgrn, state_ssm, state_conv, layers, ep)


def _run(x_prompt, x_sample, state_hgrn, state_ssm, state_conv, layers, ep):
    bp, lprompt, _ = x_prompt.shape
    bs, ls, _ = x_sample.shape
    tp = bp * lprompt

    def pairs(s_ssm):
        return s_ssm.reshape(s_ssm.shape[:2] + (SSD_PAIRS, LANES, SSD_STATE))

    groups = [
        dict(bsz=bp, l=lprompt, valid=lprompt, base=0,
             s_hg=jnp.zeros((DEPTH, bp) + state_hgrn.shape[2:], F32),
             s_ssm=pairs(jnp.zeros((DEPTH, bp) + state_ssm.shape[2:], F32)),
             s_conv=jnp.zeros((DEPTH, bp) + state_conv.shape[2:], F32)),
        dict(bsz=bs, l=SUBLANES, valid=ls, base=tp, s_hg=state_hgrn, s_ssm=pairs(state_ssm), s_conv=state_conv),
    ]
    xs = jnp.pad(x_sample, ((0, 0), (0, SUBLANES - ls), (0, 0)))
    x = jnp.concatenate([x_prompt.reshape(tp, D_MODEL), xs.reshape(bs * SUBLANES, D_MODEL)], axis=0)
    x, new = _forward(x, groups, layers, ep)
    y_p = x[:tp].reshape(bp, lprompt, D_MODEL)
    y_s = x[tp:].reshape(bs, SUBLANES, D_MODEL)[:, :ls]
    (n_p, n_s) = new
    return (y_p, y_s, n_p["hg"], n_p["ssm"].reshape((DEPTH, bp) + state_ssm.shape[2:]), n_p["conv"],
            n_s["hg"], n_s["ssm"].reshape(state_ssm.shape), n_s["conv"])
```

```python
import functools

import jax
import jax.numpy as jnp
from jax import lax
from jax.experimental import pallas as pl
from jax.experimental.pallas import tpu as pltpu

F32 = jnp.float32
BF16 = jnp.bfloat16

D_MODEL = 1024
DEPTH = 2
HG_HEADS = 8
HG_DK = 128
HG_DV = 128
HG_DIM = HG_HEADS * HG_DK
SSD_INNER = 2 * D_MODEL
SSD_HEADDIM = 64
SSD_HEADS = SSD_INNER // SSD_HEADDIM
SSD_GROUPS = 4
SSD_HPG = SSD_HEADS // SSD_GROUPS
SSD_STATE = 128
SSD_PAIRS = SSD_HEADS // 2
PAIRS_PER_GROUP = SSD_PAIRS // SSD_GROUPS
GROUP_W = SSD_INNER // SSD_GROUPS
CONV_K = 4
CONV_DIM = SSD_INNER + 2 * SSD_GROUPS * SSD_STATE
N_EXPERTS = 32
TOP_K = 4
D_FF = D_MODEL
SWIGLU_LIMIT = 7.0
SWIGLU_ALPHA = 1.702
ALPHA = (2.0 * DEPTH) ** 0.25
LN_EPS = 1e-5
RMS_EPS = 1e-6

LANES = 128
SUBLANES = 8
ROW_SUB = D_MODEL // LANES

DT_W = SSD_GROUPS * LANES
OFF_Q = 0
OFF_F = HG_DIM
OFF_V = 2 * HG_DIM
OFF_G = 3 * HG_DIM
OFF_Z = 4 * HG_DIM
OFF_XBC = OFF_Z + SSD_INNER
OFF_GA = OFF_XBC + CONV_DIM
OFF_GB = OFF_GA + D_MODEL
OFF_DT = OFF_GB + D_MODEL
N_W = OFF_DT + DT_W
IN_SPLITS = (HG_DIM, HG_DIM, HG_DIM, HG_DIM, SSD_INNER, CONV_DIM, SSD_HEADS, D_MODEL, D_MODEL)

ROW_TILE = 128
CONV_ROWS = 256
TOKEN_TILE = 256
MOE_BM = 256
VMEM_LIMIT = 48 * 1024 * 1024
MOE_VMEM_LIMIT = 56 * 1024 * 1024
LOG2E = 1.4426950408889634
HALO = 16
HGRN_HEADS_PER_STEP = 4
HGRN_SUB = 8


def _cparams(sem):
    return pltpu.CompilerParams(dimension_semantics=sem, vmem_limit_bytes=VMEM_LIMIT)


def _sigmoid(x):
    return 0.5 * jnp.tanh(0.5 * x) + 0.5


def _log1p(x):
    return jnp.log(1.0 + x)


def _dot(a, b):
    return jnp.dot(a, b, preferred_element_type=F32)


def _dot_nt(a, b):
    return lax.dot_general(a, b, (((1,), (1,)), ((), ())), preferred_element_type=F32)


def _dot_tn(a, b):
    return lax.dot_general(a, b, (((0,), (0,)), ((), ())), preferred_element_type=F32)


def _load_rows(ref3):
    return jnp.concatenate([ref3[:, s, :] for s in range(ROW_SUB)], axis=1)


def _store_rows(ref3, x):
    for s in range(ROW_SUB):
        ref3[:, s, :] = x[:, s * LANES:(s + 1) * LANES]


def _mm_kernel(x_ref, w_ref, o_ref):
    o_ref[...] = _dot(x_ref[...], w_ref[...])


def _matmul(x, w, tm, tn):
    m, k = x.shape
    n = w.shape[1]
    return pl.pallas_call(
        _mm_kernel,
        grid=(n // tn, m // tm),
        in_specs=[pl.BlockSpec((tm, k), lambda j, i: (i, 0)),
                  pl.BlockSpec((k, tn), lambda j, i: (0, j))],
        out_specs=pl.BlockSpec((tm, tn), lambda j, i: (i, j)),
        out_shape=jax.ShapeDtypeStruct((m, n), F32),
        compiler_params=_cparams(("arbitrary", "arbitrary")),
        name="in_proj",
    )(x, w)


def _conv_kernel(x_ref, pb_ref, p0_ref, w_ref, b_ref, o_ref):
    i = pl.program_id(1)
    x = x_ref[...]
    prev = jnp.where(i == 0, p0_ref[0], pb_ref[...])
    full = jnp.concatenate([prev, x], axis=0)
    w = w_ref[...]
    acc = b_ref[...] + w[CONV_K - 1:CONV_K] * x
    for s in range(1, CONV_K):
        xs = pltpu.roll(full, s, 0)[SUBLANES:]
        acc = acc + w[CONV_K - 1 - s:CONV_K - s] * xs
    o_ref[...] = acc * _sigmoid(acc)


def _conv_short_kernel(x_ref, p_ref, w_ref, b_ref, o_ref):
    x = x_ref[...]
    p = p_ref[...]
    rc = lax.broadcasted_iota(jnp.int32, x.shape, 0) & (SUBLANES - 1)
    w = w_ref[...]
    acc = b_ref[...] + w[CONV_K - 1:CONV_K] * x
    for s in range(1, CONV_K):
        xs = jnp.where(rc >= s, pltpu.roll(x, s, 0), pltpu.roll(p, (s - SUBLANES) % ROW_TILE, 0))
        acc = acc + w[CONV_K - 1 - s:CONV_K - s] * xs
    o_ref[...] = acc * _sigmoid(acc)


def _conv(xw, prev8, conv_w, conv_b, bsz, l, base_rows):
    cb = OFF_XBC // CONV_DIM
    wspec = [pl.BlockSpec((CONV_K, CONV_DIM), lambda *_: (0, 0)), pl.BlockSpec((1, CONV_DIM), lambda *_: (0, 0))]
    out_shape = jax.ShapeDtypeStruct((bsz * l, CONV_DIM), F32)
    if l == SUBLANES:
        r = ROW_TILE
        return pl.pallas_call(
            _conv_short_kernel,
            grid=(bsz * l // r,),
            in_specs=[pl.BlockSpec((r, CONV_DIM), lambda i: (base_rows // r + i, cb)),
                      pl.BlockSpec((r, CONV_DIM), lambda i: (i, 0))] + wspec,
            out_specs=pl.BlockSpec((r, CONV_DIM), lambda i: (i, 0)),
            out_shape=out_shape,
            compiler_params=_cparams(("arbitrary",)),
            name="conv_silu_short",
        )(xw, prev8.reshape(bsz * SUBLANES, CONV_DIM), conv_w, conv_b)
    r = CONV_ROWS
    tps = l // r
    rb = r // SUBLANES
    return pl.pallas_call(
        _conv_kernel,
        grid=(bsz, tps),
        in_specs=[pl.BlockSpec((r, CONV_DIM), lambda bi, i: (base_rows // r + bi * tps + i, cb)),
                  pl.BlockSpec((SUBLANES, CONV_DIM),
                               lambda bi, i: (jnp.maximum(base_rows // SUBLANES + (bi * tps + i) * rb - 1, 0), cb)),
                  pl.BlockSpec((1, SUBLANES, CONV_DIM), lambda bi, i: (bi, 0, 0))] + wspec,
        out_specs=pl.BlockSpec((r, CONV_DIM), lambda bi, i: (bi * tps + i, 0)),
        out_shape=out_shape,
        compiler_params=_cparams(("arbitrary", "arbitrary")),
        name="conv_silu",
    )(xw, xw, prev8, conv_w, conv_b)


def _seg_cumsum(x, rc, seg):
    s = 1
    while s < seg:
        x = x + jnp.where(rc >= s, pltpu.roll(x, s, 0), 0.0)
        s *= 2
    return x


def _pad_rows(x, rows):
    if x.shape[0] == rows:
        return x
    return jnp.concatenate([x, jnp.zeros((rows - x.shape[0], x.shape[1]), x.dtype)], axis=0)


def _split3(x):
    hi = x.astype(BF16)
    r1 = x - hi.astype(F32)
    mid = r1.astype(BF16)
    lo = (r1 - mid.astype(F32)).astype(BF16)
    return hi, mid, lo


def _hgrn_kernel(*refs, hps, nseq, nc, **kw):
    s0_ref, sout_ref, st_ref = refs[8], refs[10], refs[11]
    c = pl.program_id(2)

    @pl.when(c == 0)
    def _():
        for hh in range(hps):
            for i in range(nseq):
                st_ref[hh, i] = s0_ref[0, i, hh].T

    for hh in range(hps):
        _hgrn_head(hh, *refs, nseq=nseq, **kw)

    @pl.when(c == nc - 1)
    def _():
        for hh in range(hps):
            for i in range(nseq):
                sout_ref[i, hh] = st_ref[hh, i].T


def _hgrn_head(hh, q_ref, f_ref, v_ref, g_ref, la_ref, l1m_ref, oml_ref, nw_ref, s0_ref,
               o_ref, sout_ref, st_ref, k_scr, b_scr, *, ch, sub, nseq, valid):
    r = ROW_TILE
    cols = slice(hh * LANES, (hh + 1) * LANES)
    q = q_ref[:, cols]
    fr = f_ref[:, cols]
    v = v_ref[:, cols]
    q = q * _sigmoid(q)
    la = la_ref[hh]
    lsig = jnp.minimum(fr, 0.0) - _log1p(jnp.exp(-jnp.abs(fr)))
    cc = l1m_ref[hh] + lsig
    logf = jnp.maximum(la, cc) + _log1p(jnp.exp(-jnp.abs(la - cc)))
    k = oml_ref[hh] * _sigmoid(-fr)

    row = lax.broadcasted_iota(jnp.int32, (r, LANES), 0)
    lane = lax.broadcasted_iota(jnp.int32, (r, LANES), 1)
    if valid < ch:
        pad = (row & (ch - 1)) >= valid
        logf = jnp.where(pad, 0.0, logf)
        k = jnp.where(pad, 0.0, k)

    tri = lane <= row
    if ch < r:
        shift = ch.bit_length() - 1
        tri = tri & ((row >> shift) == (lane >> shift))
    tri = jnp.where(tri, 1.0, 0.0).astype(BF16)
    hi, mid, lo = _split3(logf)
    b2 = (_dot(tri, hi) + _dot(tri, mid) + _dot(tri, lo)) * LOG2E

    k_scr[hh, 0:HALO] = jnp.zeros((HALO, LANES), F32)
    b_scr[hh, 0:HALO] = jnp.zeros((HALO, LANES), F32)
    k_scr[hh, HALO:HALO + r] = k
    b_scr[hh, HALO:HALO + r] = b2
    rs = row & (sub - 1)
    scl = jnp.where(lane == rs, jnp.sum(q * k, axis=-1, keepdims=True), 0.0)
    for d in range(1, sub):
        kd = k_scr[hh, pl.ds(HALO - d, r), :]
        bd = b_scr[hh, pl.ds(HALO - d, r), :]
        sd = jnp.sum(q * kd * jnp.exp2(b2 - bd), axis=-1, keepdims=True)
        scl = jnp.where(lane == rs - d, sd, scl)

    spc = ch // sub
    pieces = []
    for m in range(r // sub):
        lo_, hi_ = m * sub, (m + 1) * sub
        piece = scl[lo_:hi_]
        if m:
            piece = pltpu.roll(piece, lo_, 1)
        j = m % spc
        if j:
            cs = (m - j) * sub
            ref_b = b2[lo_ - 1:lo_]
            qj = q[lo_:hi_] * jnp.exp2(b2[lo_:hi_] - ref_b)
            kk = k[cs:lo_] * jnp.exp2(ref_b - b2[cs:lo_])
            parts = ([jnp.zeros((cs, LANES), F32)] if cs else []) + [kk, jnp.zeros((r - lo_, LANES), F32)]
            piece = piece + _dot_nt(qj.astype(BF16), jnp.concatenate(parts, axis=0).astype(BF16))
        pieces.append(piece)
    sc = jnp.concatenate(pieces, axis=0)
    o = _dot(sc.astype(BF16), v.astype(BF16))

    inter = []
    for i in range(nseq):
        lo_, hi_ = i * ch, (i + 1) * ch
        bi = b2[lo_:hi_]
        bl = bi[ch - 1:ch]
        st = st_ref[hh, i]
        inter.append(_dot_nt((q[lo_:hi_] * jnp.exp2(bi)).astype(BF16), st.astype(BF16)))
        kh = _pad_rows(k[lo_:hi_] * jnp.exp2(bl - bi), r)
        vi = _pad_rows(v[lo_:hi_], r)
        st_ref[hh, i] = st * jnp.exp2(bl) + _dot_tn(vi.astype(BF16), kh.astype(BF16))
    o = o + (inter[0] if nseq == 1 else jnp.concatenate(inter, axis=0))

    ms = jnp.mean(o * o, axis=-1, keepdims=True)
    o_ref[:, cols] = o * lax.rsqrt(ms + RMS_EPS) * nw_ref[hh] * _sigmoid(g_ref[:, cols])


def _seq_tiling(bsz, l):
    r = ROW_TILE
    ch = min(l, r)
    nseq = r // ch
    nc = max(l // r, 1)
    nb = bsz * l // (r * nc)
    return ch, nseq, nc, nb


def _hgrn(xw, s0_all, li, la, l1m, oml, nw, bsz, l, valid, base_tile):
    r = ROW_TILE
    ch, nseq, nc, nb = _seq_tiling(bsz, l)
    sub = min(ch, HGRN_SUB)
    hps = HGRN_HEADS_PER_STEP
    w = hps * LANES
    kern = functools.partial(_hgrn_kernel, hps=hps, ch=ch, sub=sub, nseq=nseq, valid=valid, nc=nc)

    def xspec(off):
        return pl.BlockSpec((r, w), lambda bb, h, c: (base_tile + bb * nc + c, off // w + h))

    def pspec():
        return pl.BlockSpec((hps, 1, LANES), lambda bb, h, c: (h, 0, 0))

    return pl.pallas_call(
        kern,
        grid=(nb, HG_HEADS // hps, nc),
        in_specs=[xspec(OFF_Q), xspec(OFF_F), xspec(OFF_V), xspec(OFF_G),
                  pspec(), pspec(), pspec(), pspec(),
                  pl.BlockSpec((1, nseq, hps, HG_DK, HG_DV), lambda bb, h, c: (li, bb, h, 0, 0))],
        out_specs=[pl.BlockSpec((r, w), lambda bb, h, c: (bb * nc + c, h)),
                   pl.BlockSpec((nseq, hps, HG_DK, HG_DV), lambda bb, h, c: (bb, h, 0, 0))],
        out_shape=[jax.ShapeDtypeStruct((bsz * l, HG_DIM), F32),
                   jax.ShapeDtypeStruct(s0_all.shape[1:], F32)],
        scratch_shapes=[pltpu.VMEM((hps, nseq, HG_DV, HG_DK), F32),
                        pltpu.VMEM((hps, HALO + r, LANES), F32), pltpu.VMEM((hps, HALO + r, LANES), F32)],
        compiler_params=_cparams(("arbitrary", "arbitrary", "arbitrary")),
        name="hgrn2",
    )(xw, xw, xw, xw, la, l1m, oml, nw, s0_all)


def _ssd_kernel(xs_ref, bm_ref, cm_ref, z_ref, dt_ref, dtb_ref, a_ref, dsk_ref, nw_ref, h0_ref,
                y_ref, hout_ref, h_ref, al_scr, dl_scr, *, ch, nseq, valid, nc):
    r = ROW_TILE
    c = pl.program_id(2)

    @pl.when(c == 0)
    def _():
        h_ref[...] = h0_ref[0]

    shift = ch.bit_length() - 1
    rowl = lax.broadcasted_iota(jnp.int32, (r, LANES), 0)
    lane = lax.broadcasted_iota(jnp.int32, (r, LANES), 1)
    rc = rowl & (ch - 1)
    dtr = dt_ref[...] + dtb_ref[...]
    dt = jnp.maximum(dtr, 0.0) + _log1p(jnp.exp(-jnp.abs(dtr)))
    if valid < ch:
        dt = jnp.where(rc < valid, dt, 0.0)
    acum = _seg_cumsum(dt * a_ref[...], rc, ch)
    acum_t = acum.T
    dt_t = dt.T

    mask = (lane <= rowl) & ((rowl >> shift) == (lane >> shift))
    even_lane = lane < SSD_HEADDIM
    even_row = rowl < SSD_HEADDIM
    last_row = rc == ch - 1

    bmb = bm_ref[0].astype(BF16)
    cmb = cm_ref[0].astype(BF16)
    cb = _dot_nt(cmb, bmb)
    xs = xs_ref[0]
    z = z_ref[...]
    for pp in range(PAIRS_PER_GROUP):
        xp = xs[:, pp * LANES:(pp + 1) * LANES]
        acc = dsk_ref[:, pp * LANES:(pp + 1) * LANES] * xp
        for e in range(2):
            hd = 2 * pp + e
            w = cb * jnp.exp(jnp.where(mask, acum[:, hd:hd + 1] - acum_t[hd:hd + 1, :], -jnp.inf)) * dt_t[hd:hd + 1, :]
            xm = jnp.where(even_lane if e == 0 else jnp.logical_not(even_lane), xp, 0.0)
            acc = acc + _dot(w.astype(BF16), xm.astype(BF16))
        a_lane = jnp.where(even_lane, acum[:, 2 * pp:2 * pp + 1], acum[:, 2 * pp + 1:2 * pp + 2])
        dt_lane = jnp.where(even_lane, dt[:, 2 * pp:2 * pp + 1], dt[:, 2 * pp + 1:2 * pp + 2])
        ea = jnp.exp(a_lane)

        def seq_body(i, acc, pp=pp, xp=xp, a_lane=a_lane, dt_lane=dt_lane, ea=ea):
            in_seq = (rowl >> shift) == i
            al_lane = jnp.sum(jnp.where(in_seq & last_row, a_lane, 0.0), axis=0, keepdims=True)
            hp = h_ref[i, pp]
            yi = _dot_nt(cmb, hp.astype(BF16)) * ea
            acc = acc + jnp.where(in_seq, yi, 0.0)
            xw = jnp.where(in_seq, xp * (jnp.exp(al_lane - a_lane) * dt_lane), 0.0)
            dec = jnp.where(even_row, jnp.exp(al_lane[:, 0:1]), jnp.exp(al_lane[:, LANES - 1:LANES]))
            h_ref[i, pp] = dec * hp + _dot_tn(xw.astype(BF16), bmb)
            return acc

        if nseq == 1:
            acc = seq_body(0, acc)
        else:
            al_scr[pp] = a_lane
            dl_scr[pp] = dt_lane
        y_ref[:, pp * LANES:(pp + 1) * LANES] = acc

    if nseq > 1:
        def short_body(i, carry):
            rows = pl.ds(pl.multiple_of(i * ch, ch), ch)
            cm_i = _pad_rows(cm_ref[0, rows, :], 2 * SUBLANES).astype(BF16)
            bm_i = _pad_rows(bm_ref[0, rows, :], 2 * SUBLANES).astype(BF16)
            for pp in range(PAIRS_PER_GROUP):
                cols = slice(pp * LANES, (pp + 1) * LANES)
                a_i = al_scr[pp, rows, :]
                al = a_i[ch - 1:ch]
                hp = h_ref[i, pp]
                yi = _dot_nt(cm_i, hp.astype(BF16))[:ch] * jnp.exp(a_i)
                y_ref[rows, cols] = y_ref[rows, cols] + yi
                xw = xs_ref[0, rows, cols] * (jnp.exp(al - a_i) * dl_scr[pp, rows, :])
                dec = jnp.where(even_row, jnp.exp(al[:, 0:1]), jnp.exp(al[:, LANES - 1:LANES]))
                h_ref[i, pp] = dec * hp + _dot_tn(_pad_rows(xw, 2 * SUBLANES).astype(BF16), bm_i)
            return carry
        lax.fori_loop(0, nseq, short_body, 0)

    y = y_ref[...] * (z * _sigmoid(z))
    ms = jnp.mean(y * y, axis=-1, keepdims=True)
    y_ref[...] = y * lax.rsqrt(ms + RMS_EPS) * nw_ref[...]

    @pl.when(c == nc - 1)
    def _():
        hout_ref[...] = h_ref[...]


def _ssd(xc3, xw, h0_all, li, dtb, a, dsk, nw, bsz, l, valid, base_tile):
    r = ROW_TILE
    ch, nseq, nc, nb = _seq_tiling(bsz, l)
    kern = functools.partial(_ssd_kernel, ch=ch, nseq=nseq, valid=valid, nc=nc)
    hspec = pl.BlockSpec((nseq, PAIRS_PER_GROUP, LANES, SSD_STATE), lambda bb, g, c: (bb, g, 0, 0))
    h0spec = pl.BlockSpec((1, nseq, PAIRS_PER_GROUP, LANES, SSD_STATE), lambda bb, g, c: (li, bb, g, 0, 0))
    b_off = SSD_INNER // SSD_STATE
    c_off = b_off + SSD_GROUPS

    def vec(w):
        return pl.BlockSpec((1, w), lambda bb, g, c: (0, g))

    return pl.pallas_call(
        kern,
        grid=(nb, SSD_GROUPS, nc),
        in_specs=[pl.BlockSpec((1, r, GROUP_W), lambda bb, g, c: (bb * nc + c, 0, g)),
                  pl.BlockSpec((1, r, SSD_STATE), lambda bb, g, c: (bb * nc + c, 0, b_off + g)),
                  pl.BlockSpec((1, r, SSD_STATE), lambda bb, g, c: (bb * nc + c, 0, c_off + g)),
                  pl.BlockSpec((r, GROUP_W), lambda bb, g, c: (base_tile + bb * nc + c, OFF_Z // GROUP_W + g)),
                  pl.BlockSpec((r, LANES), lambda bb, g, c: (base_tile + bb * nc + c, OFF_DT // LANES + g)),
                  vec(LANES), vec(LANES), vec(GROUP_W), vec(GROUP_W), h0spec],
        out_specs=[pl.BlockSpec((r, GROUP_W), lambda bb, g, c: (bb * nc + c, g)), hspec],
        out_shape=[jax.ShapeDtypeStruct((bsz * l, SSD_INNER), F32),
                   jax.ShapeDtypeStruct(h0_all.shape[1:], F32)],
        scratch_shapes=[pltpu.VMEM((nseq, PAIRS_PER_GROUP, LANES, SSD_STATE), F32),
                        pltpu.VMEM((PAIRS_PER_GROUP, r, LANES), F32), pltpu.VMEM((PAIRS_PER_GROUP, r, LANES), F32)],
        compiler_params=_cparams(("arbitrary", "arbitrary", "arbitrary")),
        name="ssd",
    )(xc3, xc3, xc3, xw, xw, dtb, a, dsk, nw, h0_all)


def _layer_norm(x, g, b):
    mu = jnp.mean(x, axis=-1, keepdims=True)
    xc = x - mu
    var = jnp.mean(xc * xc, axis=-1, keepdims=True)
    return xc * lax.rsqrt(var + LN_EPS) * g + b


def _postmix_kernel(op_ref, os_ref, yp_ref, ys_ref, ga_ref, gb_ref, x_ref, hgp_ref, ssp_ref, wo_ref, g_ref, b_ref,
                    rwh_ref, rwl_ref, rb_ref, h_ref, h3_ref, lg_ref, *, n_p):
    first = pl.program_id(0) < n_p
    o = jnp.where(first, op_ref[...], os_ref[...])
    y = jnp.where(first, yp_ref[...], ys_ref[...])
    out_a = _dot(o.astype(BF16), hgp_ref[...])
    out_b = _dot(y.astype(BF16), ssp_ref[...])
    merged = _sigmoid(ga_ref[...]) * out_a + _sigmoid(gb_ref[...]) * out_b
    mix = _dot(merged.astype(BF16), wo_ref[...])
    h = _layer_norm(ALPHA * x_ref[...] + mix, g_ref[...], b_ref[...])
    h_ref[...] = h
    _store_rows(h3_ref, h)
    hh = h.astype(BF16)
    hl = (h - hh.astype(F32)).astype(BF16)
    lg_ref[...] = (_dot(hh, rwh_ref[...]) + _dot(hl, rwh_ref[...]) + _dot(hh, rwl_ref[...])) + rb_ref[...]


def _postmix(o_p, o_s, y_p, y_s, xw, x, hgp, ssp, wo, g, b, rwh, rwl, rb):
    t = x.shape[0]
    tm = TOKEN_TILE
    n_p = o_p.shape[0] // tm

    def full(a):
        return pl.BlockSpec(a.shape, lambda i: (0, 0))

    def first(w):
        return pl.BlockSpec((tm, w), lambda i: (jnp.minimum(i, n_p - 1), 0))

    def second(w):
        return pl.BlockSpec((tm, w), lambda i: (jnp.maximum(i - n_p, 0), 0))

    return pl.pallas_call(
        functools.partial(_postmix_kernel, n_p=n_p),
        grid=(t // tm,),
        in_specs=[first(HG_DIM), second(HG_DIM), first(SSD_INNER), second(SSD_INNER),
                  pl.BlockSpec((tm, D_MODEL), lambda i: (i, OFF_GA // D_MODEL)),
                  pl.BlockSpec((tm, D_MODEL), lambda i: (i, OFF_GB // D_MODEL)),
                  pl.BlockSpec((tm, D_MODEL), lambda i: (i, 0)),
                  full(hgp), full(ssp), full(wo), full(g), full(b), full(rwh), full(rwl), full(rb)],
        out_specs=[pl.BlockSpec((tm, D_MODEL), lambda i: (i, 0)),
                   pl.BlockSpec((tm, ROW_SUB, LANES), lambda i: (i, 0, 0)),
                   pl.BlockSpec((tm, LANES), lambda i: (i, 0))],
        out_shape=[jax.ShapeDtypeStruct((t, D_MODEL), F32),
                   jax.ShapeDtypeStruct((t, ROW_SUB, LANES), F32),
                   jax.ShapeDtypeStruct((t, LANES), F32)],
        compiler_params=_cparams(("arbitrary",)),
        name="postmix",
    )(o_p, o_s, y_p, y_s, xw, xw, x, hgp, ssp, wo, g, b, rwh, rwl, rb)


def _moe_kernel(be_ref, nu_ref, tokc_ref, tokn_ref, dstp_ref, dstc_ref, h_hbm, wgu_ref, bgu_ref, wd_ref, bd_ref,
                ys_hbm, xbuf, obuf, wgu_b, wd_b, gsem, ssem, *, nblk, trash_row):
    j = pl.program_id(0)
    nu = nu_ref[0]
    slot = j & 1
    nslot = 1 - slot

    def gather_start(tok_ref, s, i):
        pltpu.make_async_copy(h_hbm.at[pl.ds(tok_ref[0, 0, i], 1)], xbuf.at[s, pl.ds(i, 1)], gsem.at[s]).start()

    def scatter_start(dst_ref, s, i):
        pltpu.make_async_copy(obuf.at[s, pl.ds(i, 1)], ys_hbm.at[pl.ds(dst_ref[0, 0, i], 1)], ssem.at[s]).start()

    def gather_wait(s):
        pltpu.make_async_copy(h_hbm.at[pl.ds(0, MOE_BM)], xbuf.at[s], gsem.at[s]).wait()

    def scatter_wait(s):
        pltpu.make_async_copy(obuf.at[s], ys_hbm.at[pl.ds(0, MOE_BM)], ssem.at[s]).wait()

    @pl.when(j == 0)
    def _():
        obuf[...] = jnp.zeros(obuf.shape, F32)
        pltpu.make_async_copy(obuf.at[0], ys_hbm.at[pl.ds(trash_row, MOE_BM)], ssem.at[0]).start()

        def body(i, carry):
            gather_start(tokc_ref, 0, i)
            return carry
        lax.fori_loop(0, MOE_BM, body, 0, unroll=8)

    first_of_expert = (j == 0) | (be_ref[j] != be_ref[jnp.maximum(j - 1, 0)])

    @pl.when((j < nu) & first_of_expert)
    def _():
        rows = D_MODEL // 8
        for c in range(8):
            wgu_b[c * rows:(c + 1) * rows] = wgu_ref[0, 0, c * rows:(c + 1) * rows].astype(BF16)
            wd_b[c * rows:(c + 1) * rows] = wd_ref[0, 0, c * rows:(c + 1) * rows].astype(BF16)

    @pl.when(j < nu)
    def _():
        gather_wait(slot)
        scatter_wait(slot)
        for i in range(MOE_BM):
            gather_start(tokn_ref, nslot, i)
            scatter_start(dstp_ref, nslot, i)
        gu = _dot(_load_rows(xbuf.at[slot]).astype(BF16), wgu_b[...]) + bgu_ref[0, 0]
        gate = jnp.minimum(gu[:, :D_FF], SWIGLU_LIMIT)
        up = jnp.clip(gu[:, D_FF:], -SWIGLU_LIMIT, SWIGLU_LIMIT)
        act = (up + 1.0) * gate * _sigmoid(SWIGLU_ALPHA * gate)
        obuf[slot] = _dot(act.astype(BF16), wd_b[...]) + bd_ref[0, 0]

    @pl.when(j == nu - 1)
    def _():
        def body(i, carry):
            scatter_start(dstc_ref, slot, i)
            return carry
        lax.fori_loop(0, MOE_BM, body, 0, unroll=8)

    @pl.when(j == nblk - 1)
    def _():
        scatter_wait((nu - 1) & 1)
        scatter_wait(nu & 1)
        gather_wait(nu & 1)


def _moe_experts(block_e, n_used, row_tok, row_dst, row_dst_prev, h, wgu, bgu, wd, bd, li):
    t = h.shape[0]
    nblk = row_tok.shape[0]
    trash_row = TOP_K * t
    kern = functools.partial(_moe_kernel, nblk=nblk, trash_row=trash_row)

    def smem(imap):
        return pl.BlockSpec((1, 1, MOE_BM), imap, memory_space=pltpu.SMEM)

    grid_spec = pltpu.PrefetchScalarGridSpec(
        num_scalar_prefetch=2,
        grid=(nblk,),
        in_specs=[smem(lambda j, be, nu: (j, 0, 0)),
                  smem(lambda j, be, nu: (jnp.minimum(j + 1, nblk - 1), 0, 0)),
                  smem(lambda j, be, nu: (j, 0, 0)),
                  smem(lambda j, be, nu: (j, 0, 0)),
                  pl.BlockSpec(memory_space=pl.ANY),
                  pl.BlockSpec((1, 1, D_MODEL, 2 * D_FF), lambda j, be, nu: (li, be[j], 0, 0)),
                  pl.BlockSpec((1, 1, 1, 2 * D_FF), lambda j, be, nu: (li, be[j], 0, 0)),
                  pl.BlockSpec((1, 1, D_FF, D_MODEL), lambda j, be, nu: (li, be[j], 0, 0)),
                  pl.BlockSpec((1, 1, 1, D_MODEL), lambda j, be, nu: (li, be[j], 0, 0))],
        out_specs=pl.BlockSpec(memory_space=pl.ANY),
        scratch_shapes=[pltpu.VMEM((2, MOE_BM, ROW_SUB, LANES), F32), pltpu.VMEM((2, MOE_BM, D_MODEL), F32),
                        pltpu.VMEM((D_MODEL, 2 * D_FF), BF16), pltpu.VMEM((D_FF, D_MODEL), BF16),
                        pltpu.SemaphoreType.DMA((2,)), pltpu.SemaphoreType.DMA((2,))],
    )
    return pl.pallas_call(
        kern,
        grid_spec=grid_spec,
        out_shape=jax.ShapeDtypeStruct((trash_row + 2 * MOE_BM, D_MODEL), F32),
        compiler_params=pltpu.CompilerParams(dimension_semantics=("arbitrary",), vmem_limit_bytes=MOE_VMEM_LIMIT),
        name="moe_experts",
    )(block_e, n_used, row_tok, row_tok, row_dst_prev, row_dst, h, wgu, bgu, wd, bd)


def _combine_kernel(h_ref, y0_ref, y1_ref, y2_ref, y3_ref, gt_ref, g_ref, b_ref, x_ref, xb_ref):
    gt = gt_ref[...]
    y = gt[:, 0:1] * y0_ref[...]
    for kk, y_ref in enumerate((y1_ref, y2_ref, y3_ref), start=1):
        y = y + gt[:, kk:kk + 1] * y_ref[...]
    x = _layer_norm(ALPHA * h_ref[...] + y, g_ref[...], b_ref[...])
    x_ref[...] = x
    xb_ref[...] = x.astype(BF16)


def _combine(h, ys, gates, g, b, tm):
    t = h.shape[0]
    nt = t // tm

    def yspec(kk):
        return pl.BlockSpec((tm, D_MODEL), lambda i: (kk * nt + i, 0))

    return pl.pallas_call(
        _combine_kernel,
        grid=(nt,),
        in_specs=[pl.BlockSpec((tm, D_MODEL), lambda i: (i, 0)),
                  yspec(0), yspec(1), yspec(2), yspec(3),
                  pl.BlockSpec((tm, TOP_K), lambda i: (i, 0)),
                  pl.BlockSpec((1, D_MODEL), lambda i: (0, 0)),
                  pl.BlockSpec((1, D_MODEL), lambda i: (0, 0))],
        out_specs=[pl.BlockSpec((tm, D_MODEL), lambda i: (i, 0)),
                   pl.BlockSpec((tm, D_MODEL), lambda i: (i, 0))],
        out_shape=[jax.ShapeDtypeStruct((t, D_MODEL), F32),
                   jax.ShapeDtypeStruct((t, D_MODEL), BF16)],
        compiler_params=_cparams(("arbitrary",)),
        name="combine_ln",
    )(h, ys, ys, ys, ys, gates, g, b)


def _split_cols(a, sizes):
    out, off = [], 0
    for s in sizes:
        out.append(a[..., off:off + s])
        off += s
    return out


def _per_group(vec_heads):
    v = vec_heads.reshape(SSD_GROUPS, SSD_HPG)
    return jnp.pad(v, ((0, 0), (0, LANES - SSD_HPG))).reshape(1, DT_W)


def _prep_layer(p, l):
    wq, wf, wv, wg, wz, wxbc, wdt, wga, wgb = _split_cols(p["w_in"][l], IN_SPLITS)
    wdt = jnp.pad(wdt.reshape(D_MODEL, SSD_GROUPS, SSD_HPG), ((0, 0), (0, 0), (0, LANES - SSD_HPG)))
    w_in = jnp.concatenate([wq, wf, wv, wg, wz, wxbc, wga, wgb, wdt.reshape(D_MODEL, DT_W)], axis=1).astype(BF16)
    lb = p["lb_all"][l].reshape(HG_HEADS, 1, HG_DK)
    rw = jnp.pad(p["router_w"][l], ((0, 0), (0, LANES - N_EXPERTS)))
    rwh = rw.astype(BF16)
    rwl = (rw - rwh.astype(F32)).astype(BF16)
    return dict(
        w_in=w_in,
        la=jnp.log(lb), l1m=jnp.log1p(-lb), oml=1.0 - lb,
        hg_nw=p["hg_norm_w"][l].reshape(HG_HEADS, 1, HG_DV),
        hgp=p["hg_proj"][l].astype(BF16),
        conv_w=p["conv_w"][l], conv_b=p["conv_b"][l].reshape(1, CONV_DIM),
        dtb=_per_group(p["dt_bias"][l]),
        a=_per_group(-jnp.exp(p["a_log"][l].astype(F32))),
        dsk=jnp.repeat(p["d_skip"][l], SSD_HEADDIM).reshape(1, SSD_INNER),
        ssd_nw=p["ssd_norm_w"][l].reshape(1, SSD_INNER),
        ssp=p["ssd_proj"][l].astype(BF16),
        wo=p["w_out"][l].astype(BF16),
        ln1_g=p["ln1_g"][l].reshape(1, D_MODEL), ln1_b=p["ln1_b"][l].reshape(1, D_MODEL),
        rwh=rwh, rwl=rwl,
        rb=jnp.pad(p["router_b"][l], (0, LANES - N_EXPERTS)).reshape(1, LANES),
        ln2_g=p["ln2_g"][l].reshape(1, D_MODEL), ln2_b=p["ln2_b"][l].reshape(1, D_MODEL),
    )


def _prep_experts(p):
    return dict(
        wgu=p["w_gu"], bgu=p["b_gu"].reshape(DEPTH, N_EXPERTS, 1, 2 * D_FF),
        wd=p["w_down"], bd=p["b_down"].reshape(DEPTH, N_EXPERTS, 1, D_MODEL),
    )


def _moe(h, h3, logits, lp, ep, li, tm):
    t = h.shape[0]
    s = t * TOP_K
    top_v, top_e = lax.top_k(logits[:, :N_EXPERTS], TOP_K)
    gates = jax.nn.softmax(top_v, axis=-1)
    e_slot = top_e.reshape(s).astype(jnp.int32)
    order = jnp.argsort(e_slot).astype(jnp.int32)
    counts = jnp.sum((e_slot[:, None] == jnp.arange(N_EXPERTS, dtype=jnp.int32)[None, :]).astype(jnp.int32), axis=0)
    padded = (counts + MOE_BM - 1) // MOE_BM * MOE_BM
    pend = jnp.cumsum(padded)
    cend = pend - padded + counts
    n_blocks = (s + N_EXPERTS * (MOE_BM - 1) + MOE_BM - 1) // MOE_BM
    nrows = n_blocks * MOE_BM
    rows = jnp.arange(nrows, dtype=jnp.int32)[:, None]
    before = pend[None, :] <= rows
    pad_before = jnp.sum(jnp.where(before, padded - counts, 0), axis=1)
    is_pad = jnp.any((cend[None, :] <= rows) & (rows < pend[None, :]), axis=1) | (rows[:, 0] >= pend[-1])
    slot = order[jnp.clip(rows[:, 0] - pad_before, 0, s - 1)]
    tok = slot // TOP_K
    row_tok = jnp.where(is_pad, 0, tok)
    row_dst = jnp.where(is_pad, TOP_K * t + rows[:, 0] % (2 * MOE_BM), (slot - tok * TOP_K) * t + tok)
    blk_rows = jnp.arange(n_blocks, dtype=jnp.int32)[:, None] * MOE_BM
    block_e = jnp.minimum(jnp.sum((pend[None, :] <= blk_rows).astype(jnp.int32), axis=1), N_EXPERTS - 1)
    n_used = (pend[-1:] // MOE_BM).astype(jnp.int32)
    first_prev = TOP_K * t + MOE_BM + jnp.arange(MOE_BM, dtype=jnp.int32)
    row_dst_prev = jnp.concatenate([first_prev, row_dst[:-MOE_BM]])
    ys = _moe_experts(block_e, n_used, row_tok.reshape(n_blocks, 1, MOE_BM), row_dst.reshape(n_blocks, 1, MOE_BM),
                      row_dst_prev.reshape(n_blocks, 1, MOE_BM), h3, ep["wgu"], ep["bgu"], ep["wd"], ep["bd"], li)
    return _combine(h, ys, gates, lp["ln2_g"], lp["ln2_b"], tm)


def _mixers(xw, li, lp, grp):
    bsz, l, valid, base = grp["bsz"], grp["l"], grp["valid"], grp["base"]
    t = bsz * l
    prev8 = jnp.pad(grp["s_conv"][li], ((0, 0), (SUBLANES - (CONV_K - 1), 0), (0, 0)))
    xc = _conv(xw, prev8, lp["conv_w"], lp["conv_b"], bsz, l, base)
    tail = base + jnp.arange(bsz, dtype=jnp.int32)[:, None] * l + jnp.arange(valid - (CONV_K - 1), valid)[None, :]
    conv_new = xw[tail.reshape(-1)][:, OFF_XBC:OFF_XBC + CONV_DIM].reshape(bsz, CONV_K - 1, CONV_DIM)
    o, shg = _hgrn(xw, grp["s_hg"], li, lp["la"], lp["l1m"], lp["oml"], lp["hg_nw"], bsz, l, valid, base // ROW_TILE)
    y, hss = _ssd(xc.reshape(t // ROW_TILE, ROW_TILE, CONV_DIM), xw, grp["s_ssm"], li, lp["dtb"], lp["a"], lp["dsk"],
                  lp["ssd_nw"], bsz, l, valid, base // ROW_TILE)
    return o, y, shg, hss, conv_new


def _forward(x, groups, layers, ep):
    t = x.shape[0]
    xb = x.astype(BF16)
    new = [dict(hg=[], ssm=[], conv=[]) for _ in groups]
    for li, lp in enumerate(layers):
        xw = _matmul(xb, lp["w_in"], 512 if t % 512 == 0 else TOKEN_TILE, N_W // 4)
        mix =[_mixers(xw, li, lp, grp) for grp in groups]
        for n, m in zip(new, mix):
            n["hg"].append(m[2])
            n["ssm"].append(m[3])
            n["conv"].append(m[4])
        h, h3, logits = _postmix(mix[0][0], mix[1][0], mix[0][1], mix[1][1], xw, x, lp["hgp"], lp["ssp"], lp["wo"],
                             lp["ln1_g"], lp["ln1_b"], lp["rwh"], lp["rwl"], lp["rb"])
        x, xb = _moe(h, h3, logits, lp, ep, li, TOKEN_TILE)
    return x, [{k: jnp.stack(v) for k, v in n.items()} for n in new]


def kernel(x_prompt, x_sample, state_hgrn, state_ssm, state_conv, hg_lower_bounds, w_in, hg_norm_w, hg_proj,
           conv_w, conv_b, dt_bias, a_log, d_skip, ssd_norm_w, ssd_proj, w_out, ln1_g, ln1_b, router_w,
           router_b, w_gu, b_gu, w_down, b_down, ln2_g, ln2_b):
    lb_all = jnp.cumsum(jax.nn.softmax(hg_lower_bounds.astype(F32), axis=0), axis=0)
    lb_all = lb_all - lb_all[0]
    p = dict(lb_all=lb_all, w_in=w_in, hg_norm_w=hg_norm_w, hg_proj=hg_proj, conv_w=conv_w, conv_b=conv_b,
             dt_bias=dt_bias, a_log=a_log, d_skip=d_skip, ssd_norm_w=ssd_norm_w, ssd_proj=ssd_proj, w_out=w_out,
             ln1_g=ln1_g, ln1_b=ln1_b, router_w=router_w, router_b=router_b, w_gu=w_gu, b_gu=b_gu,
             w_down=w_down, b_down=b_down, ln2_g=ln2_g, ln2_b=ln2_b)
    layers = [_prep_layer(p, l) for l in range(DEPTH)]
    ep = _prep_experts(p)

    return _run(x_prompt, x_sample, state_hgrn, state_ssm, state_conv, layers, ep)


def _run(x_prompt, x_sample, state_hgrn, state_ssm, state_conv, layers, ep):
    bp, lprompt, _ = x_prompt.shape
    bs, ls, _ = x_sample.shape
    tp = bp * lprompt

    def pairs(s_ssm):
        return s_ssm.reshape(s_ssm.shape[:2] + (SSD_PAIRS, LANES, SSD_STATE))

    groups = [
        dict(bsz=bp, l=lprompt, valid=lprompt, base=0,
             s_hg=jnp.zeros((DEPTH, bp) + state_hgrn.shape[2:], F32),
             s_ssm=pairs(jnp.zeros((DEPTH, bp) + state_ssm.shape[2:], F32)),
             s_conv=jnp.zeros((DEPTH, bp) + state_conv.shape[2:], F32)),
        dict(bsz=bs, l=SUBLANES, valid=ls, base=tp, s_hg=state_hgrn, s_ssm=pairs(state_ssm), s_conv=state_conv),
    ]
    xs = jnp.pad(x_sample, ((0, 0), (0, SUBLANES - ls), (0, 0)))
    x = jnp.concatenate([x_prompt.reshape(tp, D_MODEL), xs.reshape(bs * SUBLANES, D_MODEL)], axis=0)
    x, new = _forward(x, groups, layers, ep)
    y_p = x[:tp].reshape(bp, lprompt, D_MODEL)
    y_s = x[tp:].reshape(bs, SUBLANES, D_MODEL)[:, :ls]
    (n_p, n_s) = new
    return (y_p, y_s, n_p["hg"], n_p["ssm"].reshape((DEPTH, bp) + state_ssm.shape[2:]), n_p["conv"],
            n_s["hg"], n_s["ssm"].reshape(state_ssm.shape), n_s["conv"])
```

```python
import functools

import jax
import jax.numpy as jnp
from jax import lax
from jax.experimental import pallas as pl
from jax.experimental.pallas import tpu as pltpu

F32 = jnp.float32
BF16 = jnp.bfloat16

D_MODEL = 1024
DEPTH = 2
HG_HEADS = 8
HG_DK = 128
HG_DV = 128
HG_DIM = HG_HEADS * HG_DK
SSD_INNER = 2 * D_MODEL
SSD_HEADDIM = 64
SSD_HEADS = SSD_INNER // SSD_HEADDIM
SSD_GROUPS = 4
SSD_HPG = SSD_HEADS // SSD_GROUPS
SSD_STATE = 128
SSD_PAIRS = SSD_HEADS // 2
PAIRS_PER_GROUP = SSD_PAIRS // SSD_GROUPS
GROUP_W = SSD_INNER // SSD_GROUPS
CONV_K = 4
CONV_DIM = SSD_INNER + 2 * SSD_GROUPS * SSD_STATE
N_EXPERTS = 32
TOP_K = 4
D_FF = D_MODEL
SWIGLU_LIMIT = 7.0
SWIGLU_ALPHA = 1.702
ALPHA = (2.0 * DEPTH) ** 0.25
LN_EPS = 1e-5
RMS_EPS = 1e-6

LANES = 128
SUBLANES = 8
ROW_SUB = D_MODEL // LANES

DT_W = SSD_GROUPS * LANES
OFF_Q = 0
OFF_F = HG_DIM
OFF_V = 2 * HG_DIM
OFF_G = 3 * HG_DIM
OFF_Z = 4 * HG_DIM
OFF_XBC = OFF_Z + SSD_INNER
OFF_GA = OFF_XBC + CONV_DIM
OFF_GB = OFF_GA + D_MODEL
OFF_DT = OFF_GB + D_MODEL
N_W = OFF_DT + DT_W
IN_SPLITS = (HG_DIM, HG_DIM, HG_DIM, HG_DIM, SSD_INNER, CONV_DIM, SSD_HEADS, D_MODEL, D_MODEL)

ROW_TILE = 128
CONV_ROWS = 256
TOKEN_TILE = 256
MOE_BM = 256
VMEM_LIMIT = 48 * 1024 * 1024
MOE_VMEM_LIMIT = 56 * 1024 * 1024
LOG2E = 1.4426950408889634
HALO = 16
HGRN_HEADS_PER_STEP = 4
HGRN_SUB = 8


def _cparams(sem):
    return pltpu.CompilerParams(dimension_semantics=sem, vmem_limit_bytes=VMEM_LIMIT)


def _sigmoid(x):
    return 0.5 * jnp.tanh(0.5 * x) + 0.5


def _log1p(x):
    return jnp.log(1.0 + x)


def _dot(a, b):
    return jnp.dot(a, b, preferred_element_type=F32)


def _dot_nt(a, b):
    return lax.dot_general(a, b, (((1,), (1,)), ((), ())), preferred_element_type=F32)


def _dot_tn(a, b):
    return lax.dot_general(a, b, (((0,), (0,)), ((), ())), preferred_element_type=F32)


def _load_rows(ref3):
    return jnp.concatenate([ref3[:, s, :] for s in range(ROW_SUB)], axis=1)


def _store_rows(ref3, x):
    for s in range(ROW_SUB):
        ref3[:, s, :] = x[:, s * LANES:(s + 1) * LANES]


def _mm_kernel(x_ref, w_ref, o_ref):
    o_ref[...] = _dot(x_ref[...], w_ref[...])


def _matmul(x, w, tm, tn):
    m, k = x.shape
    n = w.shape[1]
    return pl.pallas_call(
        _mm_kernel,
        grid=(n // tn, m // tm),
        in_specs=[pl.BlockSpec((tm, k), lambda j, i: (i, 0)),
                  pl.BlockSpec((k, tn), lambda j, i: (0, j))],
        out_specs=pl.BlockSpec((tm, tn), lambda j, i: (i, j)),
        out_shape=jax.ShapeDtypeStruct((m, n), F32),
        compiler_params=_cparams(("arbitrary", "arbitrary")),
        name="in_proj",
    )(x, w)


def _conv_kernel(x_ref, pb_ref, p0_ref, w_ref, b_ref, o_ref):
    i = pl.program_id(1)
    x = x_ref[...]
    prev = jnp.where(i == 0, p0_ref[0], pb_ref[...])
    full = jnp.concatenate([prev, x], axis=0)
    w = w_ref[...]
    acc = b_ref[...] + w[CONV_K - 1:CONV_K] * x
    for s in range(1, CONV_K):
        xs = pltpu.roll(full, s, 0)[SUBLANES:]
        acc = acc + w[CONV_K - 1 - s:CONV_K - s] * xs
    o_ref[...] = acc * _sigmoid(acc)


def _conv_short_kernel(x_ref, p_ref, w_ref, b_ref, o_ref):
    x = x_ref[...]
    p = p_ref[...]
    rc = lax.broadcasted_iota(jnp.int32, x.shape, 0) & (SUBLANES - 1)
    w = w_ref[...]
    acc = b_ref[...] + w[CONV_K - 1:CONV_K] * x
    for s in range(1, CONV_K):
        xs = jnp.where(rc >= s, pltpu.roll(x, s, 0), pltpu.roll(p, (s - SUBLANES) % ROW_TILE, 0))
        acc = acc + w[CONV_K - 1 - s:CONV_K - s] * xs
    o_ref[...] = acc * _sigmoid(acc)


def _conv(xw, prev8, conv_w, conv_b, bsz, l, base_rows):
    cb = OFF_XBC // CONV_DIM
    wspec = [pl.BlockSpec((CONV_K, CONV_DIM), lambda *_: (0, 0)), pl.BlockSpec((1, CONV_DIM), lambda *_: (0, 0))]
    out_shape = jax.ShapeDtypeStruct((bsz * l, CONV_DIM), F32)
    if l == SUBLANES:
        r = ROW_TILE
        return pl.pallas_call(
            _conv_short_kernel,
            grid=(bsz * l // r,),
            in_specs=[pl.BlockSpec((r, CONV_DIM), lambda i: (base_rows // r + i, cb)),
                      pl.BlockSpec((r, CONV_DIM), lambda i: (i, 0))] + wspec,
            out_specs=pl.BlockSpec((r, CONV_DIM), lambda i: (i, 0)),
            out_shape=out_shape,
            compiler_params=_cparams(("arbitrary",)),
            name="conv_silu_short",
        )(xw, prev8.reshape(bsz * SUBLANES, CONV_DIM), conv_w, conv_b)
    r = CONV_ROWS
    tps = l // r
    rb = r // SUBLANES
    return pl.pallas_call(
        _conv_kernel,
        grid=(bsz, tps),
        in_specs=[pl.BlockSpec((r, CONV_DIM), lambda bi, i: (base_rows // r + bi * tps + i, cb)),
                  pl.BlockSpec((SUBLANES, CONV_DIM),
                               lambda bi, i: (jnp.maximum(base_rows // SUBLANES + (bi * tps + i) * rb - 1, 0), cb)),
                  pl.BlockSpec((1, SUBLANES, CONV_DIM), lambda bi, i: (bi, 0, 0))] + wspec,
        out_specs=pl.BlockSpec((r, CONV_DIM), lambda bi, i: (bi * tps + i, 0)),
        out_shape=out_shape,
        compiler_params=_cparams(("arbitrary", "arbitrary")),
        name="conv_silu",
    )(xw, xw, prev8, conv_w, conv_b)


def _seg_cumsum(x, rc, seg):
    s = 1
    while s < seg:
        x = x + jnp.where(rc >= s, pltpu.roll(x, s, 0), 0.0)
        s *= 2
    return x


def _pad_rows(x, rows):
    if x.shape[0] == rows:
        return x
    return jnp.concatenate([x, jnp.zeros((rows - x.shape[0], x.shape[1]), x.dtype)], axis=0)


def _split3(x):
    hi = x.astype(BF16)
    r1 = x - hi.astype(F32)
    mid = r1.astype(BF16)
    lo = (r1 - mid.astype(F32)).astype(BF16)
    return hi, mid, lo


def _hgrn_kernel(*refs, hps, nseq, nc, **kw):
    refs = refs[:9] + refs[10:]
    s0_ref, sout_ref, st_ref = refs[8], refs[10], refs[11]
    c = pl.program_id(2)

    @pl.when(c == 0)
    def _():
        for hh in range(hps):
            for i in range(nseq):
                st_ref[hh, i] = s0_ref[0, i, hh].T

    for hh in range(hps):
        _hgrn_head(hh, *refs, nseq=nseq, **kw)

    @pl.when(c == nc - 1)
    def _():
        for hh in range(hps):
            for i in range(nseq):
                sout_ref[0, i, hh] = st_ref[hh, i].T


def _hgrn_head(hh, q_ref, f_ref, v_ref, g_ref, la_ref, l1m_ref, oml_ref, nw_ref, s0_ref,
               o_ref, sout_ref, st_ref, k_scr, b_scr, *, ch, sub, nseq, valid):
    r = ROW_TILE
    cols = slice(hh * LANES, (hh + 1) * LANES)
    q = q_ref[:, cols]
    fr = f_ref[:, cols]
    v = v_ref[:, cols]
    q = q * _sigmoid(q)
    la = la_ref[hh]
    lsig = jnp.minimum(fr, 0.0) - _log1p(jnp.exp(-jnp.abs(fr)))
    cc = l1m_ref[hh] + lsig
    logf = jnp.maximum(la, cc) + _log1p(jnp.exp(-jnp.abs(la - cc)))
    k = oml_ref[hh] * _sigmoid(-fr)

    row = lax.broadcasted_iota(jnp.int32, (r, LANES), 0)
    lane = lax.broadcasted_iota(jnp.int32, (r, LANES), 1)
    if valid < ch:
        pad = (row & (ch - 1)) >= valid
        logf = jnp.where(pad, 0.0, logf)
        k = jnp.where(pad, 0.0, k)

    tri = lane <= row
    if ch < r:
        shift = ch.bit_length() - 1
        tri = tri & ((row >> shift) == (lane >> shift))
    tri = jnp.where(tri, 1.0, 0.0).astype(BF16)
    hi, mid, lo = _split3(logf)
    b2 = (_dot(tri, hi) + _dot(tri, mid) + _dot(tri, lo)) * LOG2E

    k_scr[hh, 0:HALO] = jnp.zeros((HALO, LANES), F32)
    b_scr[hh, 0:HALO] = jnp.zeros((HALO, LANES), F32)
    k_scr[hh, HALO:HALO + r] = k
    b_scr[hh, HALO:HALO + r] = b2
    rs = row & (sub - 1)
    scl = jnp.where(lane == rs, jnp.sum(q * k, axis=-1, keepdims=True), 0.0)
    for d in range(1, sub):
        kd = k_scr[hh, pl.ds(HALO - d, r), :]
        bd = b_scr[hh, pl.ds(HALO - d, r), :]
        sd = jnp.sum(q * kd * jnp.exp2(b2 - bd), axis=-1, keepdims=True)
        scl = jnp.where(lane == rs - d, sd, scl)

    spc = ch // sub
    pieces = []
    for m in range(r // sub):
        lo_, hi_ = m * sub, (m + 1) * sub
        piece = scl[lo_:hi_]
        if m:
            piece = pltpu.roll(piece, lo_, 1)
        j = m % spc
        if j:
            cs = (m - j) * sub
            ref_b = b2[lo_ - 1:lo_]
            qj = q[lo_:hi_] * jnp.exp2(b2[lo_:hi_] - ref_b)
            kk = k[cs:lo_] * jnp.exp2(ref_b - b2[cs:lo_])
            parts = ([jnp.zeros((cs, LANES), F32)] if cs else []) + [kk, jnp.zeros((r - lo_, LANES), F32)]
            piece = piece + _dot_nt(qj.astype(BF16), jnp.concatenate(parts, axis=0).astype(BF16))
        pieces.append(piece)
    sc = jnp.concatenate(pieces, axis=0)
    o = _dot(sc.astype(BF16), v.astype(BF16))

    inter = []
    for i in range(nseq):
        lo_, hi_ = i * ch, (i + 1) * ch
        bi = b2[lo_:hi_]
        bl = bi[ch - 1:ch]
        st = st_ref[hh, i]
        inter.append(_dot_nt((q[lo_:hi_] * jnp.exp2(bi)).astype(BF16), st.astype(BF16)))
        kh = _pad_rows(k[lo_:hi_] * jnp.exp2(bl - bi), r)
        vi = _pad_rows(v[lo_:hi_], r)
        st_ref[hh, i] = st * jnp.exp2(bl) + _dot_tn(vi.astype(BF16), kh.astype(BF16))
    o = o + (inter[0] if nseq == 1 else jnp.concatenate(inter, axis=0))

    ms = jnp.mean(o * o, axis=-1, keepdims=True)
    o_ref[:, cols] = o * lax.rsqrt(ms + RMS_EPS) * nw_ref[hh] * _sigmoid(g_ref[:, cols])


def _seq_tiling(bsz, l):
    r = ROW_TILE
    ch = min(l, r)
    nseq = r // ch
    nc = max(l // r, 1)
    nb = bsz * l // (r * nc)
    return ch, nseq, nc, nb


def _carry_alias(new_all, n_in):
    return [pl.BlockSpec(memory_space=pl.ANY)], [new_all], {n_in: 1}


def _hgrn(xw, s0_all, new_all, li, la, l1m, oml, nw, bsz, l, valid, base_tile):
    r = ROW_TILE
    ch, nseq, nc, nb = _seq_tiling(bsz, l)
    sub = min(ch, HGRN_SUB)
    hps = HGRN_HEADS_PER_STEP
    w = hps * LANES
    extra_specs, extra_args, aliases = _carry_alias(new_all, 9)
    kern = functools.partial(_hgrn_kernel, hps=hps, ch=ch, sub=sub, nseq=nseq, valid=valid, nc=nc)

    def xspec(off):
        return pl.BlockSpec((r, w), lambda bb, h, c: (base_tile + bb * nc + c, off // w + h))

    def pspec():
        return pl.BlockSpec((hps, 1, LANES), lambda bb, h, c: (h, 0, 0))

    return pl.pallas_call(
        kern,
        grid=(nb, HG_HEADS // hps, nc),
        in_specs=[xspec(OFF_Q), xspec(OFF_F), xspec(OFF_V), xspec(OFF_G),
                  pspec(), pspec(), pspec(), pspec(),
                  pl.BlockSpec((1, nseq, hps, HG_DK, HG_DV), lambda bb, h, c: (li, bb, h, 0, 0))] + extra_specs,
        out_specs=[pl.BlockSpec((r, w), lambda bb, h, c: (bb * nc + c, h)),
                   pl.BlockSpec((1, nseq, hps, HG_DK, HG_DV), lambda bb, h, c: (li, bb, h, 0, 0))],
        out_shape=[jax.ShapeDtypeStruct((bsz * l, HG_DIM), F32),
                   jax.ShapeDtypeStruct(s0_all.shape, F32)],
        scratch_shapes=[pltpu.VMEM((hps, nseq, HG_DV, HG_DK), F32),
                        pltpu.VMEM((hps, HALO + r, LANES), F32), pltpu.VMEM((hps, HALO + r, LANES), F32)],
        input_output_aliases=aliases,
        compiler_params=_cparams(("arbitrary", "arbitrary", "arbitrary")),
        name="hgrn2",
    )(xw, xw, xw, xw, la, l1m, oml, nw, s0_all, *extra_args)


def _ssd_kernel(*refs, **kw):
    _ssd_body(*(refs[:10] + refs[11:]), **kw)


def _ssd_body(xs_ref, bm_ref, cm_ref, z_ref, dt_ref, dtb_ref, a_ref, dsk_ref, nw_ref, h0_ref,
              y_ref, hout_ref, h_ref, al_scr, dl_scr, *, ch, nseq, valid, nc):
    r = ROW_TILE
    c = pl.program_id(2)

    @pl.when(c == 0)
    def _():
        h_ref[...] = h0_ref[0]

    shift = ch.bit_length() - 1
    rowl = lax.broadcasted_iota(jnp.int32, (r, LANES), 0)
    lane = lax.broadcasted_iota(jnp.int32, (r, LANES), 1)
    rc = rowl & (ch - 1)
    dtr = dt_ref[...] + dtb_ref[...]
    dt = jnp.maximum(dtr, 0.0) + _log1p(jnp.exp(-jnp.abs(dtr)))
    if valid < ch:
        dt = jnp.where(rc < valid, dt, 0.0)
    acum = _seg_cumsum(dt * a_ref[...], rc, ch)
    acum_t = acum.T
    dt_t = dt.T

    mask = (lane <= rowl) & ((rowl >> shift) == (lane >> shift))
    even_lane = lane < SSD_HEADDIM
    even_row = rowl < SSD_HEADDIM
    last_row = rc == ch - 1

    bmb = bm_ref[0].astype(BF16)
    cmb = cm_ref[0].astype(BF16)
    cb = _dot_nt(cmb, bmb)
    xs = xs_ref[0]
    z = z_ref[...]
    for pp in range(PAIRS_PER_GROUP):
        xp = xs[:, pp * LANES:(pp + 1) * LANES]
        acc = dsk_ref[:, pp * LANES:(pp + 1) * LANES] * xp
        for e in range(2):
            hd = 2 * pp + e
            w = cb * jnp.exp(jnp.where(mask, acum[:, hd:hd + 1] - acum_t[hd:hd + 1, :], -jnp.inf)) * dt_t[hd:hd + 1, :]
            xm = jnp.where(even_lane if e == 0 else jnp.logical_not(even_lane), xp, 0.0)
            acc = acc + _dot(w.astype(BF16), xm.astype(BF16))
        a_lane = jnp.where(even_lane, acum[:, 2 * pp:2 * pp + 1], acum[:, 2 * pp + 1:2 * pp + 2])
        dt_lane = jnp.where(even_lane, dt[:, 2 * pp:2 * pp + 1], dt[:, 2 * pp + 1:2 * pp + 2])
        ea = jnp.exp(a_lane)

        def seq_body(i, acc, pp=pp, xp=xp, a_lane=a_lane, dt_lane=dt_lane, ea=ea):
            in_seq = (rowl >> shift) == i
            al_lane = jnp.sum(jnp.where(in_seq & last_row, a_lane, 0.0), axis=0, keepdims=True)
            hp = h_ref[i, pp]
            yi = _dot_nt(cmb, hp.astype(BF16)) * ea
            acc = acc + jnp.where(in_seq, yi, 0.0)
            xw = jnp.where(in_seq, xp * (jnp.exp(al_lane - a_lane) * dt_lane), 0.0)
            dec = jnp.where(even_row, jnp.exp(al_lane[:, 0:1]), jnp.exp(al_lane[:, LANES - 1:LANES]))
            h_ref[i, pp] = dec * hp + _dot_tn(xw.astype(BF16), bmb)
            return acc

        if nseq == 1:
            acc = seq_body(0, acc)
        else:
            al_scr[pp] = a_lane
            dl_scr[pp] = dt_lane
        y_ref[:, pp * LANES:(pp + 1) * LANES] = acc

    if nseq > 1:
        def short_body(i, carry):
            rows = pl.ds(pl.multiple_of(i * ch, ch), ch)
            cm_i = _pad_rows(cm_ref[0, rows, :], 2 * SUBLANES).astype(BF16)
            bm_i = _pad_rows(bm_ref[0, rows, :], 2 * SUBLANES).astype(BF16)
            for pp in range(PAIRS_PER_GROUP):
                cols = slice(pp * LANES, (pp + 1) * LANES)
                a_i = al_scr[pp, rows, :]
                al = a_i[ch - 1:ch]
                hp = h_ref[i, pp]
                yi = _dot_nt(cm_i, hp.astype(BF16))[:ch] * jnp.exp(a_i)
                y_ref[rows, cols] = y_ref[rows, cols] + yi
                xw = xs_ref[0, rows, cols] * (jnp.exp(al - a_i) * dl_scr[pp, rows, :])
                dec = jnp.where(even_row, jnp.exp(al[:, 0:1]), jnp.exp(al[:, LANES - 1:LANES]))
                h_ref[i, pp] = dec * hp + _dot_tn(_pad_rows(xw, 2 * SUBLANES).astype(BF16), bm_i)
            return carry
        lax.fori_loop(0, nseq, short_body, 0)

    y = y_ref[...] * (z * _sigmoid(z))
    ms = jnp.mean(y * y, axis=-1, keepdims=True)
    y_ref[...] = y * lax.rsqrt(ms + RMS_EPS) * nw_ref[...]

    @pl.when(c == nc - 1)
    def _():
        hout_ref[0] = h_ref[...]


def _ssd(xc3, xw, h0_all, new_all, li, dtb, a, dsk, nw, bsz, l, valid, base_tile):
    r = ROW_TILE
    ch, nseq, nc, nb = _seq_tiling(bsz, l)
    extra_specs, extra_args, aliases = _carry_alias(new_all, 10)
    kern = functools.partial(_ssd_kernel, ch=ch, nseq=nseq, valid=valid, nc=nc)
    h0spec = pl.BlockSpec((1, nseq, PAIRS_PER_GROUP, LANES, SSD_STATE), lambda bb, g, c: (li, bb, g, 0, 0))
    hspec = h0spec
    b_off = SSD_INNER // SSD_STATE
    c_off = b_off + SSD_GROUPS

    def vec(w):
        return pl.BlockSpec((1, w), lambda bb, g, c: (0, g))

    return pl.pallas_call(
        kern,
        grid=(nb, SSD_GROUPS, nc),
        in_specs=[pl.BlockSpec((1, r, GROUP_W), lambda bb, g, c: (bb * nc + c, 0, g)),
                  pl.BlockSpec((1, r, SSD_STATE), lambda bb, g, c: (bb * nc + c, 0, b_off + g)),
                  pl.BlockSpec((1, r, SSD_STATE), lambda bb, g, c: (bb * nc + c, 0, c_off + g)),
                  pl.BlockSpec((r, GROUP_W), lambda bb, g, c: (base_tile + bb * nc + c, OFF_Z // GROUP_W + g)),
                  pl.BlockSpec((r, LANES), lambda bb, g, c: (base_tile + bb * nc + c, OFF_DT // LANES + g)),
                  vec(LANES), vec(LANES), vec(GROUP_W), vec(GROUP_W), h0spec] + extra_specs,
        out_specs=[pl.BlockSpec((r, GROUP_W), lambda bb, g, c: (bb * nc + c, g)), hspec],
        out_shape=[jax.ShapeDtypeStruct((bsz * l, SSD_INNER), F32),
                   jax.ShapeDtypeStruct(h0_all.shape, F32)],
        scratch_shapes=[pltpu.VMEM((nseq, PAIRS_PER_GROUP, LANES, SSD_STATE), F32),
                        pltpu.VMEM((PAIRS_PER_GROUP, r, LANES), F32), pltpu.VMEM((PAIRS_PER_GROUP, r, LANES), F32)],
        input_output_aliases=aliases,
        compiler_params=_cparams(("arbitrary", "arbitrary", "arbitrary")),
        name="ssd",
    )(xc3, xc3, xc3, xw, xw, dtb, a, dsk, nw, h0_all, *extra_args)


def _layer_norm(x, g, b):
    mu = jnp.mean(x, axis=-1, keepdims=True)
    xc = x - mu
    var = jnp.mean(xc * xc, axis=-1, keepdims=True)
    return xc * lax.rsqrt(var + LN_EPS) * g + b


def _postmix_kernel(op_ref, os_ref, yp_ref, ys_ref, ga_ref, gb_ref, x_ref, hgp_ref, ssp_ref, wo_ref, g_ref, b_ref,
                    rwh_ref, rwl_ref, rb_ref, h_ref, h3_ref, lg_ref, *, n_p):
    first = pl.program_id(0) < n_p
    o = jnp.where(first, op_ref[...], os_ref[...])
    y = jnp.where(first, yp_ref[...], ys_ref[...])
    out_a = _dot(o.astype(BF16), hgp_ref[...])
    out_b = _dot(y.astype(BF16), ssp_ref[...])
    merged = _sigmoid(ga_ref[...]) * out_a + _sigmoid(gb_ref[...]) * out_b
    mix = _dot(merged.astype(BF16), wo_ref[...])
    h = _layer_norm(ALPHA * x_ref[...] + mix, g_ref[...], b_ref[...])
    h_ref[...] = h
    _store_rows(h3_ref, h)
    hh = h.astype(BF16)
    hl = (h - hh.astype(F32)).astype(BF16)
    lg_ref[...] = (_dot(hh, rwh_ref[...]) + _dot(hl, rwh_ref[...]) + _dot(hh, rwl_ref[...])) + rb_ref[...]


def _postmix(o_p, o_s, y_p, y_s, xw, x, hgp, ssp, wo, g, b, rwh, rwl, rb):
    t = x.shape[0]
    tm = TOKEN_TILE
    n_p = o_p.shape[0] // tm

    def full(a):
        return pl.BlockSpec(a.shape, lambda i: (0, 0))

    def first(w):
        return pl.BlockSpec((tm, w), lambda i: (jnp.minimum(i, n_p - 1), 0))

    def second(w):
        return pl.BlockSpec((tm, w), lambda i: (jnp.maximum(i - n_p, 0), 0))

    return pl.pallas_call(
        functools.partial(_postmix_kernel, n_p=n_p),
        grid=(t // tm,),
        in_specs=[first(HG_DIM), second(HG_DIM), first(SSD_INNER), second(SSD_INNER),
                  pl.BlockSpec((tm, D_MODEL), lambda i: (i, OFF_GA // D_MODEL)),
                  pl.BlockSpec((tm, D_MODEL), lambda i: (i, OFF_GB // D_MODEL)),
                  pl.BlockSpec((tm, D_MODEL), lambda i: (i, 0)),
                  full(hgp), full(ssp), full(wo), full(g), full(b), full(rwh), full(rwl), full(rb)],
        out_specs=[pl.BlockSpec((tm, D_MODEL), lambda i: (i, 0)),
                   pl.BlockSpec((tm, ROW_SUB, LANES), lambda i: (i, 0, 0)),
                   pl.BlockSpec((tm, LANES), lambda i: (i, 0))],
        out_shape=[jax.ShapeDtypeStruct((t, D_MODEL), F32),
                   jax.ShapeDtypeStruct((t, ROW_SUB, LANES), F32),
                   jax.ShapeDtypeStruct((t, LANES), F32)],
        compiler_params=_cparams(("arbitrary",)),
        name="postmix",
    )(o_p, o_s, y_p, y_s, xw, xw, x, hgp, ssp, wo, g, b, rwh, rwl, rb)


def _moe_kernel(be_ref, nu_ref, tokc_ref, tokn_ref, dstp_ref, dstc_ref, h_hbm, wgu_ref, bgu_ref, wd_ref, bd_ref,
                ys_hbm, xbuf0, xbuf1, obuf0, obuf1, wgu_b, wd_b, gsem, ssem, *, nblk, trash_row):
    j = pl.program_id(0)
    nu = nu_ref[0]
    xbufs = (xbuf0, xbuf1)
    obufs = (obuf0, obuf1)

    def gather_start(tok_ref, s, i):
        pltpu.make_async_copy(h_hbm.at[pl.ds(tok_ref[0, 0, i], 1)], xbufs[s].at[pl.ds(i, 1)], gsem.at[s]).start()

    def scatter_start(dst_ref, s, i):
        pltpu.make_async_copy(obufs[s].at[pl.ds(i, 1)], ys_hbm.at[pl.ds(dst_ref[0, 0, i], 1)], ssem.at[s]).start()

    def gather_wait(s):
        pltpu.make_async_copy(h_hbm.at[pl.ds(0, MOE_BM)], xbuf0, gsem.at[s]).wait()

    def scatter_wait(s):
        pltpu.make_async_copy(obuf0, ys_hbm.at[pl.ds(0, MOE_BM)], ssem.at[s]).wait()

    @pl.when(j == 0)
    def _():
        obuf0[...] = jnp.zeros(obuf0.shape, F32)
        obuf1[...] = jnp.zeros(obuf1.shape, F32)
        pltpu.make_async_copy(obuf0, ys_hbm.at[pl.ds(trash_row, MOE_BM)], ssem.at[0]).start()

        def body(i, carry):
            gather_start(tokc_ref, 0, i)
            return carry
        lax.fori_loop(0, MOE_BM, body, 0, unroll=8)

    first_of_expert = (j == 0) | (be_ref[j] != be_ref[jnp.maximum(j - 1, 0)])

    @pl.when((j < nu) & first_of_expert)
    def _():
        rows = D_MODEL // 8
        for c in range(8):
            wgu_b[c * rows:(c + 1) * rows] = wgu_ref[0, 0, c * rows:(c + 1) * rows].astype(BF16)
            wd_b[c * rows:(c + 1) * rows] = wd_ref[0, 0, c * rows:(c + 1) * rows].astype(BF16)

    for par in range(2):
        @pl.when((j < nu) & ((j & 1) == par))
        def _(par=par):
            other = 1 - par
            gather_wait(par)
            scatter_wait(par)
            for i in range(MOE_BM):
                gather_start(tokn_ref, other, i)
                scatter_start(dstp_ref, other, i)
            gu = _dot(_load_rows(xbufs[par]).astype(BF16), wgu_b[...]) + bgu_ref[0, 0]
            gate = jnp.minimum(gu[:, :D_FF], SWIGLU_LIMIT)
            up = jnp.clip(gu[:, D_FF:], -SWIGLU_LIMIT, SWIGLU_LIMIT)
            act = (up + 1.0) * gate * _sigmoid(SWIGLU_ALPHA * gate)
            obufs[par][...] = _dot(act.astype(BF16), wd_b[...]) + bd_ref[0, 0]

        @pl.when((j == nu - 1) & ((j & 1) == par))
        def _(par=par):
            def body(i, carry):
                scatter_start(dstc_ref, par, i)
                return carry
            lax.fori_loop(0, MOE_BM, body, 0, unroll=8)

    @pl.when(j == nblk - 1)
    def _():
        scatter_wait((nu - 1) & 1)
        scatter_wait(nu & 1)
        gather_wait(nu & 1)


def _moe_experts(block_e, n_used, row_tok, row_dst, row_dst_prev, h, wgu, bgu, wd, bd, li):
    t = h.shape[0]
    nblk = row_tok.shape[0]
    trash_row = TOP_K * t
    kern = functools.partial(_moe_kernel, nblk=nblk, trash_row=trash_row)

    def smem(imap):
        return pl.BlockSpec((1, 1, MOE_BM), imap, memory_space=pltpu.SMEM)

    grid_spec = pltpu.PrefetchScalarGridSpec(
        num_scalar_prefetch=2,
        grid=(nblk,),
        in_specs=[smem(lambda j, be, nu: (j, 0, 0)),
                  smem(lambda j, be, nu: (jnp.minimum(j + 1, nblk - 1), 0, 0)),
                  smem(lambda j, be, nu: (j, 0, 0)),
                  smem(lambda j, be, nu: (j, 0, 0)),
                  pl.BlockSpec(memory_space=pl.ANY),
                  pl.BlockSpec((1, 1, D_MODEL, 2 * D_FF), lambda j, be, nu: (li, be[j], 0, 0)),
                  pl.BlockSpec((1, 1, 1, 2 * D_FF), lambda j, be, nu: (li, be[j], 0, 0)),
                  pl.BlockSpec((1, 1, D_FF, D_MODEL), lambda j, be, nu: (li, be[j], 0, 0)),
                  pl.BlockSpec((1, 1, 1, D_MODEL), lambda j, be, nu: (li, be[j], 0, 0))],
        out_specs=pl.BlockSpec(memory_space=pl.ANY),
        scratch_shapes=[pltpu.VMEM((MOE_BM, ROW_SUB, LANES), F32), pltpu.VMEM((MOE_BM, ROW_SUB, LANES), F32),
                        pltpu.VMEM((MOE_BM, D_MODEL), F32), pltpu.VMEM((MOE_BM, D_MODEL), F32),
                        pltpu.VMEM((D_MODEL, 2 * D_FF), BF16), pltpu.VMEM((D_FF, D_MODEL), BF16),
                        pltpu.SemaphoreType.DMA((2,)), pltpu.SemaphoreType.DMA((2,))],
    )
    return pl.pallas_call(
        kern,
        grid_spec=grid_spec,
        out_shape=jax.ShapeDtypeStruct((trash_row + 2 * MOE_BM, D_MODEL), F32),
        compiler_params=pltpu.CompilerParams(dimension_semantics=("arbitrary",), vmem_limit_bytes=MOE_VMEM_LIMIT),
        name="moe_experts",
    )(block_e, n_used, row_tok, row_tok, row_dst_prev, row_dst, h, wgu, bgu, wd, bd)


def _combine_kernel(h_ref, y0_ref, y1_ref, y2_ref, y3_ref, gt_ref, g_ref, b_ref, x_ref, xb_ref):
    gt = gt_ref[...]
    y = gt[:, 0:1] * y0_ref[...]
    for kk, y_ref in enumerate((y1_ref, y2_ref, y3_ref), start=1):
        y = y + gt[:, kk:kk + 1] * y_ref[...]
    x = _layer_norm(ALPHA * h_ref[...] + y, g_ref[...], b_ref[...])
    x_ref[...] = x
    xb_ref[...] = x.astype(BF16)


def _combine(h, ys, gates, g, b, tm):
    t = h.shape[0]
    nt = t // tm

    def yspec(kk):
        return pl.BlockSpec((tm, D_MODEL), lambda i: (kk * nt + i, 0))

    return pl.pallas_call(
        _combine_kernel,
        grid=(nt,),
        in_specs=[pl.BlockSpec((tm, D_MODEL), lambda i: (i, 0)),
                  yspec(0), yspec(1), yspec(2), yspec(3),
                  pl.BlockSpec((tm, TOP_K), lambda i: (i, 0)),
                  pl.BlockSpec((1, D_MODEL), lambda i: (0, 0)),
                  pl.BlockSpec((1, D_MODEL), lambda i: (0, 0))],
        out_specs=[pl.BlockSpec((tm, D_MODEL), lambda i: (i, 0)),
                   pl.BlockSpec((tm, D_MODEL), lambda i: (i, 0))],
        out_shape=[jax.ShapeDtypeStruct((t, D_MODEL), F32),
                   jax.ShapeDtypeStruct((t, D_MODEL), BF16)],
        compiler_params=_cparams(("arbitrary",)),
        name="combine_ln",
    )(h, ys, ys, ys, ys, gates, g, b)


def _split_cols(a, sizes):
    out, off = [], 0
    for s in sizes:
        out.append(a[..., off:off + s])
        off += s
    return out


def _per_group(vec_heads):
    v = vec_heads.reshape(SSD_GROUPS, SSD_HPG)
    return jnp.pad(v, ((0, 0), (0, LANES - SSD_HPG))).reshape(1, DT_W)


def _prep_layer(p, l):
    wq, wf, wv, wg, wz, wxbc, wdt, wga, wgb = _split_cols(p["w_in"][l], IN_SPLITS)
    wdt = jnp.pad(wdt.reshape(D_MODEL, SSD_GROUPS, SSD_HPG), ((0, 0), (0, 0), (0, LANES - SSD_HPG)))
    w_in = jnp.concatenate([wq, wf, wv, wg, wz, wxbc, wga, wgb, wdt.reshape(D_MODEL, DT_W)], axis=1).astype(BF16)
    lb = p["lb_all"][l].reshape(HG_HEADS, 1, HG_DK)
    rw = jnp.pad(p["router_w"][l], ((0, 0), (0, LANES - N_EXPERTS)))
    rwh = rw.astype(BF16)
    rwl = (rw - rwh.astype(F32)).astype(BF16)
    return dict(
        w_in=w_in,
        la=jnp.log(lb), l1m=jnp.log1p(-lb), oml=1.0 - lb,
        hg_nw=p["hg_norm_w"][l].reshape(HG_HEADS, 1, HG_DV),
        hgp=p["hg_proj"][l].astype(BF16),
        conv_w=p["conv_w"][l], conv_b=p["conv_b"][l].reshape(1, CONV_DIM),
        dtb=_per_group(p["dt_bias"][l]),
        a=_per_group(-jnp.exp(p["a_log"][l].astype(F32))),
        dsk=jnp.repeat(p["d_skip"][l], SSD_HEADDIM).reshape(1, SSD_INNER),
        ssd_nw=p["ssd_norm_w"][l].reshape(1, SSD_INNER),
        ssp=p["ssd_proj"][l].astype(BF16),
        wo=p["w_out"][l].astype(BF16),
        ln1_g=p["ln1_g"][l].reshape(1, D_MODEL), ln1_b=p["ln1_b"][l].reshape(1, D_MODEL),
        rwh=rwh, rwl=rwl,
        rb=jnp.pad(p["router_b"][l], (0, LANES - N_EXPERTS)).reshape(1, LANES),
        ln2_g=p["ln2_g"][l].reshape(1, D_MODEL), ln2_b=p["ln2_b"][l].reshape(1, D_MODEL),
    )


def _prep_experts(p):
    return dict(
        wgu=p["w_gu"], bgu=p["b_gu"].reshape(DEPTH, N_EXPERTS, 1, 2 * D_FF),
        wd=p["w_down"], bd=p["b_down"].reshape(DEPTH, N_EXPERTS, 1, D_MODEL),
    )


def _moe(h, h3, logits, lp, ep, li, tm):
    t = h.shape[0]
    s = t * TOP_K
    top_v, top_e = lax.top_k(logits[:, :N_EXPERTS], TOP_K)
    gates = jax.nn.softmax(top_v, axis=-1)
    e_slot = top_e.reshape(s).astype(jnp.int32)
    order = jnp.argsort(e_slot).astype(jnp.int32)
    counts = jnp.sum((e_slot[:, None] == jnp.arange(N_EXPERTS, dtype=jnp.int32)[None, :]).astype(jnp.int32), axis=0)
    padded = (counts + MOE_BM - 1) // MOE_BM * MOE_BM
    pend = jnp.cumsum(padded)
    cend = pend - padded + counts
    n_blocks = (s + N_EXPERTS * (MOE_BM - 1) + MOE_BM - 1) // MOE_BM
    nrows = n_blocks * MOE_BM
    rows = jnp.arange(nrows, dtype=jnp.int32)[:, None]
    before = pend[None, :] <= rows
    pad_before = jnp.sum(jnp.where(before, padded - counts, 0), axis=1)
    is_pad = jnp.any((cend[None, :] <= rows) & (rows < pend[None, :]), axis=1) | (rows[:, 0] >= pend[-1])
    slot = order[jnp.clip(rows[:, 0] - pad_before, 0, s - 1)]
    tok = slot // TOP_K
    row_tok = jnp.where(is_pad, 0, tok)
    row_dst = jnp.where(is_pad, TOP_K * t + rows[:, 0] % (2 * MOE_BM), (slot - tok * TOP_K) * t + tok)
    blk_rows = jnp.arange(n_blocks, dtype=jnp.int32)[:, None] * MOE_BM
    block_e = jnp.minimum(jnp.sum((pend[None, :] <= blk_rows).astype(jnp.int32), axis=1), N_EXPERTS - 1)
    n_used = (pend[-1:] // MOE_BM).astype(jnp.int32)
    first_prev = TOP_K * t + MOE_BM + jnp.arange(MOE_BM, dtype=jnp.int32)
    row_dst_prev = jnp.concatenate([first_prev, row_dst[:-MOE_BM]])
    ys = _moe_experts(block_e, n_used, row_tok.reshape(n_blocks, 1, MOE_BM), row_dst.reshape(n_blocks, 1, MOE_BM),
                      row_dst_prev.reshape(n_blocks, 1, MOE_BM), h3, ep["wgu"], ep["bgu"], ep["wd"], ep["bd"], li)
    return _combine(h, ys, gates, lp["ln2_g"], lp["ln2_b"], tm)


def _mixers(xw, li, lp, grp, new):
    bsz, l, valid, base = grp["bsz"], grp["l"], grp["valid"], grp["base"]
    t = bsz * l
    prev8 = jnp.pad(grp["s_conv"][li], ((0, 0), (SUBLANES - (CONV_K - 1), 0), (0, 0)))
    xc = _conv(xw, prev8, lp["conv_w"], lp["conv_b"], bsz, l, base)
    tail = base + jnp.arange(bsz, dtype=jnp.int32)[:, None] * l + jnp.arange(valid - (CONV_K - 1), valid)[None, :]
    conv_new = xw[tail.reshape(-1)][:, OFF_XBC:OFF_XBC + CONV_DIM].reshape(bsz, CONV_K - 1, CONV_DIM)
    o, hg = _hgrn(xw, grp["s_hg"], new["hg"], li, lp["la"], lp["l1m"], lp["oml"], lp["hg_nw"], bsz, l, valid,
                  base // ROW_TILE)
    y, ssm = _ssd(xc.reshape(t // ROW_TILE, ROW_TILE, CONV_DIM), xw, grp["s_ssm"], new["ssm"], li, lp["dtb"],
                  lp["a"], lp["dsk"], lp["ssd_nw"], bsz, l, valid, base // ROW_TILE)
    return o, y, dict(hg=hg, ssm=ssm, conv=new["conv"] + [conv_new])


def _forward(x, groups, layers, ep):
    t = x.shape[0]
    xb = x.astype(BF16)
    new = [dict(hg=jnp.zeros(g["s_hg"].shape, F32), ssm=jnp.zeros(g["s_ssm"].shape, F32), conv=[]) for g in groups]
    for li, lp in enumerate(layers):
        xw = _matmul(xb, lp["w_in"], 512 if t % 512 == 0 else TOKEN_TILE, N_W // 4)
        mix = [_mixers(xw, li, lp, grp, n) for grp, n in zip(groups, new)]
        new = [m[2] for m in mix]
        h, h3, logits = _postmix(mix[0][0], mix[1][0], mix[0][1], mix[1][1], xw, x, lp["hgp"], lp["ssp"], lp["wo"],
                             lp["ln1_g"], lp["ln1_b"], lp["rwh"], lp["rwl"], lp["rb"])
        x, xb = _moe(h, h3, logits, lp, ep, li, TOKEN_TILE)
    return x, [dict(hg=n["hg"], ssm=n["ssm"], conv=jnp.stack(n["conv"])) for n in new]


def kernel(x_prompt, x_sample, state_hgrn, state_ssm, state_conv, hg_lower_bounds, w_in, hg_norm_w, hg_proj,
           conv_w, conv_b, dt_bias, a_log, d_skip, ssd_norm_w, ssd_proj, w_out, ln1_g, ln1_b, router_w,
           router_b, w_gu, b_gu, w_down, b_down, ln2_g, ln2_b):
    lb_all = jnp.cumsum(jax.nn.softmax(hg_lower_bounds.astype(F32), axis=0), axis=0)
    lb_all = lb_all - lb_all[0]
    p = dict(lb_all=lb_all, w_in=w_in, hg_norm_w=hg_norm_w, hg_proj=hg_proj, conv_w=conv_w, conv_b=conv_b,
             dt_bias=dt_bias, a_log=a_log, d_skip=d_skip, ssd_norm_w=ssd_norm_w, ssd_proj=ssd_proj, w_out=w_out,
             ln1_g=ln1_g, ln1_b=ln1_b, router_w=router_w, router_b=router_b, w_gu=w_gu, b_gu=b_gu,
             w_down=w_down, b_down=b_down, ln2_g=ln2_g, ln2_b=ln2_b)
    layers = [_prep_layer(p, l) for l in range(DEPTH)]
    ep = _prep_experts(p)

    return _run(x_prompt, x_sample, state_hgrn, state_ssm, state_conv, layers, ep)


def _run(x_prompt, x_sample, state_hgrn, state_ssm, state_conv, layers, ep):
    bp, lprompt, _ = x_prompt.shape
    bs, ls, _ = x_sample.shape
    tp = bp * lprompt

    def pairs(s_ssm):
        return s_ssm.reshape(s_ssm.shape[:2] + (SSD_PAIRS, LANES, SSD_STATE))

    groups = [
        dict(bsz=bp, l=lprompt, valid=lprompt, base=0,
             s_hg=jnp.zeros((DEPTH, bp) + state_hgrn.shape[2:], F32),
             s_ssm=pairs(jnp.zeros((DEPTH, bp) + state_ssm.shape[2:], F32)),
             s_conv=jnp.zeros((DEPTH, bp) + state_conv.shape[2:], F32)),
        dict(bsz=bs, l=SUBLANES, valid=ls, base=tp, s_hg=state_hgrn, s_ssm=pairs(state_ssm), s_conv=state_conv),
    ]
    xs = jnp.pad(x_sample, ((0, 0), (0, SUBLANES - ls), (0, 0)))
    x = jnp.concatenate([x_prompt.reshape(tp, D_MODEL), xs.reshape(bs * SUBLANES, D_MODEL)], axis=0)
    x, new = _forward(x, groups, layers, ep)
    y_p = x[:tp].reshape(bp, lprompt, D_MODEL)
    y_s = x[tp:].reshape(bs, SUBLANES, D_MODEL)[:, :ls]
    (n_p, n_s) = new
    return (y_p, y_s, n_p["hg"], n_p["ssm"].reshape((DEPTH, bp) + state_ssm.shape[2:]), n_p["conv"],
            n_s["hg"], n_s["ssm"].reshape(state_ssm.shape), n_s["conv"])
```

```python
import functools

import jax
import jax.numpy as jnp
from jax import lax
from jax.experimental import pallas as pl
from jax.experimental.pallas import tpu as pltpu

F32 = jnp.float32
BF16 = jnp.bfloat16

D_MODEL = 1024
DEPTH = 2
HG_HEADS = 8
HG_DK = 128
HG_DV = 128
HG_DIM = HG_HEADS * HG_DK
SSD_INNER = 2 * D_MODEL
SSD_HEADDIM = 64
SSD_HEADS = SSD_INNER // SSD_HEADDIM
SSD_GROUPS = 4
SSD_HPG = SSD_HEADS // SSD_GROUPS
SSD_STATE = 128
SSD_PAIRS = SSD_HEADS // 2
PAIRS_PER_GROUP = SSD_PAIRS // SSD_GROUPS
GROUP_W = SSD_INNER // SSD_GROUPS
CONV_K = 4
CONV_DIM = SSD_INNER + 2 * SSD_GROUPS * SSD_STATE
N_EXPERTS = 32
TOP_K = 4
D_FF = D_MODEL
SWIGLU_LIMIT = 7.0
SWIGLU_ALPHA = 1.702
ALPHA = (2.0 * DEPTH) ** 0.25
LN_EPS = 1e-5
RMS_EPS = 1e-6

LANES = 128
SUBLANES = 8
ROW_SUB = D_MODEL // LANES

DT_W = SSD_GROUPS * LANES
OFF_Q = 0
OFF_F = HG_DIM
OFF_V = 2 * HG_DIM
OFF_G = 3 * HG_DIM
OFF_Z = 4 * HG_DIM
OFF_XBC = OFF_Z + SSD_INNER
OFF_GA = OFF_XBC + CONV_DIM
OFF_GB = OFF_GA + D_MODEL
OFF_DT = OFF_GB + D_MODEL
N_W = OFF_DT + DT_W
IN_SPLITS = (HG_DIM, HG_DIM, HG_DIM, HG_DIM, SSD_INNER, CONV_DIM, SSD_HEADS, D_MODEL, D_MODEL)

ROW_TILE = 128
CONV_ROWS = 256
TOKEN_TILE = 256
MOE_BM = 256
VMEM_LIMIT = 48 * 1024 * 1024
MOE_VMEM_LIMIT = 56 * 1024 * 1024
LOG2E = 1.4426950408889634
HALO = 16
HGRN_HEADS_PER_STEP = 4
HGRN_SUB = 8
SSD_GROUPS_PER_STEP = 4


def _cparams(sem):
    return pltpu.CompilerParams(dimension_semantics=sem, vmem_limit_bytes=VMEM_LIMIT)


def _sigmoid(x):
    return 0.5 * jnp.tanh(0.5 * x) + 0.5


def _log1p(x):
    return jnp.log(1.0 + x)


def _dot(a, b):
    return jnp.dot(a, b, preferred_element_type=F32)


def _dot_nt(a, b):
    return lax.dot_general(a, b, (((1,), (1,)), ((), ())), preferred_element_type=F32)


def _dot_tn(a, b):
    return lax.dot_general(a, b, (((0,), (0,)), ((), ())), preferred_element_type=F32)


def _load_rows(ref3):
    return jnp.concatenate([ref3[:, s, :] for s in range(ROW_SUB)], axis=1)


def _store_rows(ref3, x):
    for s in range(ROW_SUB):
        ref3[:, s, :] = x[:, s * LANES:(s + 1) * LANES]


def _mm_kernel(x_ref, w_ref, o_ref):
    o_ref[...] = _dot(x_ref[...], w_ref[...])


def _matmul(x, w, tm, tn):
    m, k = x.shape
    n = w.shape[1]
    return pl.pallas_call(
        _mm_kernel,
        grid=(n // tn, m // tm),
        in_specs=[pl.BlockSpec((tm, k), lambda j, i: (i, 0)),
                  pl.BlockSpec((k, tn), lambda j, i: (0, j))],
        out_specs=pl.BlockSpec((tm, tn), lambda j, i: (i, j)),
        out_shape=jax.ShapeDtypeStruct((m, n), F32),
        compiler_params=_cparams(("arbitrary", "arbitrary")),
        name="in_proj",
    )(x, w)


def _conv_kernel(x_ref, pb_ref, p0_ref, w_ref, b_ref, o_ref):
    i = pl.program_id(1)
    x = x_ref[...]
    prev = jnp.where(i == 0, p0_ref[0], pb_ref[...])
    full = jnp.concatenate([prev, x], axis=0)
    w = w_ref[...]
    acc = b_ref[...] + w[CONV_K - 1:CONV_K] * x
    for s in range(1, CONV_K):
        xs = pltpu.roll(full, s, 0)[SUBLANES:]
        acc = acc + w[CONV_K - 1 - s:CONV_K - s] * xs
    o_ref[...] = acc * _sigmoid(acc)


def _conv_short_kernel(x_ref, p_ref, w_ref, b_ref, o_ref):
    x = x_ref[...]
    p = p_ref[...]
    rc = lax.broadcasted_iota(jnp.int32, x.shape, 0) & (SUBLANES - 1)
    w = w_ref[...]
    acc = b_ref[...] + w[CONV_K - 1:CONV_K] * x
    for s in range(1, CONV_K):
        xs = jnp.where(rc >= s, pltpu.roll(x, s, 0), pltpu.roll(p, (s - SUBLANES) % ROW_TILE, 0))
        acc = acc + w[CONV_K - 1 - s:CONV_K - s] * xs
    o_ref[...] = acc * _sigmoid(acc)


def _conv(xw, prev8, conv_w, conv_b, bsz, l, base_rows):
    cb = OFF_XBC // CONV_DIM
    wspec = [pl.BlockSpec((CONV_K, CONV_DIM), lambda *_: (0, 0)), pl.BlockSpec((1, CONV_DIM), lambda *_: (0, 0))]
    out_shape = jax.ShapeDtypeStruct((bsz * l, CONV_DIM), F32)
    if l == SUBLANES:
        r = ROW_TILE
        return pl.pallas_call(
            _conv_short_kernel,
            grid=(bsz * l // r,),
            in_specs=[pl.BlockSpec((r, CONV_DIM), lambda i: (base_rows // r + i, cb)),
                      pl.BlockSpec((r, CONV_DIM), lambda i: (i, 0))] + wspec,
            out_specs=pl.BlockSpec((r, CONV_DIM), lambda i: (i, 0)),
            out_shape=out_shape,
            compiler_params=_cparams(("arbitrary",)),
            name="conv_silu_short",
        )(xw, prev8.reshape(bsz * SUBLANES, CONV_DIM), conv_w, conv_b)
    r = CONV_ROWS
    tps = l // r
    rb = r // SUBLANES
    return pl.pallas_call(
        _conv_kernel,
        grid=(bsz, tps),
        in_specs=[pl.BlockSpec((r, CONV_DIM), lambda bi, i: (base_rows // r + bi * tps + i, cb)),
                  pl.BlockSpec((SUBLANES, CONV_DIM),
                               lambda bi, i: (jnp.maximum(base_rows // SUBLANES + (bi * tps + i) * rb - 1, 0), cb)),
                  pl.BlockSpec((1, SUBLANES, CONV_DIM), lambda bi, i: (bi, 0, 0))] + wspec,
        out_specs=pl.BlockSpec((r, CONV_DIM), lambda bi, i: (bi * tps + i, 0)),
        out_shape=out_shape,
        compiler_params=_cparams(("arbitrary", "arbitrary")),
        name="conv_silu",
    )(xw, xw, prev8, conv_w, conv_b)


def _seg_cumsum(x, rc, seg):
    s = 1
    while s < seg:
        x = x + jnp.where(rc >= s, pltpu.roll(x, s, 0), 0.0)
        s *= 2
    return x


def _pad_rows(x, rows):
    if x.shape[0] == rows:
        return x
    return jnp.concatenate([x, jnp.zeros((rows - x.shape[0], x.shape[1]), x.dtype)], axis=0)


def _split3(x):
    hi = x.astype(BF16)
    r1 = x - hi.astype(F32)
    mid = r1.astype(BF16)
    lo = (r1 - mid.astype(F32)).astype(BF16)
    return hi, mid, lo


def _hgrn_kernel(*refs, hps, nseq, nc, **kw):
    refs = refs[:9] + refs[10:]
    s0_ref, sout_ref, st_ref = refs[8], refs[10], refs[11]
    c = pl.program_id(2)

    @pl.when(c == 0)
    def _():
        for hh in range(hps):
            for i in range(nseq):
                st_ref[hh, i] = s0_ref[0, i, hh].T

    for hh in range(hps):
        _hgrn_head(hh, *refs, nseq=nseq, **kw)

    @pl.when(c == nc - 1)
    def _():
        for hh in range(hps):
            for i in range(nseq):
                sout_ref[0, i, hh] = st_ref[hh, i].T


def _hgrn_head(hh, q_ref, f_ref, v_ref, g_ref, la_ref, l1m_ref, oml_ref, nw_ref, s0_ref,
               o_ref, sout_ref, st_ref, k_scr, b_scr, *, ch, sub, nseq, valid):
    r = ROW_TILE
    cols = slice(hh * LANES, (hh + 1) * LANES)
    q = q_ref[:, cols]
    fr = f_ref[:, cols]
    v = v_ref[:, cols]
    q = q * _sigmoid(q)
    la = la_ref[hh]
    lsig = jnp.minimum(fr, 0.0) - _log1p(jnp.exp(-jnp.abs(fr)))
    cc = l1m_ref[hh] + lsig
    logf = jnp.maximum(la, cc) + _log1p(jnp.exp(-jnp.abs(la - cc)))
    k = oml_ref[hh] * _sigmoid(-fr)

    row = lax.broadcasted_iota(jnp.int32, (r, LANES), 0)
    lane = lax.broadcasted_iota(jnp.int32, (r, LANES), 1)
    if valid < ch:
        pad = (row & (ch - 1)) >= valid
        logf = jnp.where(pad, 0.0, logf)
        k = jnp.where(pad, 0.0, k)

    tri = lane <= row
    if ch < r:
        shift = ch.bit_length() - 1
        tri = tri & ((row >> shift) == (lane >> shift))
    tri = jnp.where(tri, 1.0, 0.0).astype(BF16)
    hi, mid, lo = _split3(logf)
    b2 = (_dot(tri, hi) + _dot(tri, mid) + _dot(tri, lo)) * LOG2E

    k_scr[hh, 0:HALO] = jnp.zeros((HALO, LANES), F32)
    b_scr[hh, 0:HALO] = jnp.zeros((HALO, LANES), F32)
    k_scr[hh, HALO:HALO + r] = k
    b_scr[hh, HALO:HALO + r] = b2
    rs = row & (sub - 1)
    scl = jnp.where(lane == rs, jnp.sum(q * k, axis=-1, keepdims=True), 0.0)
    for d in range(1, sub):
        kd = k_scr[hh, pl.ds(HALO - d, r), :]
        bd = b_scr[hh, pl.ds(HALO - d, r), :]
        sd = jnp.sum(q * kd * jnp.exp2(b2 - bd), axis=-1, keepdims=True)
        scl = jnp.where(lane == rs - d, sd, scl)

    spc = ch // sub
    pieces = []
    for m in range(r // sub):
        lo_, hi_ = m * sub, (m + 1) * sub
        piece = scl[lo_:hi_]
        if m:
            piece = pltpu.roll(piece, lo_, 1)
        j = m % spc
        if j:
            cs = (m - j) * sub
            ref_b = b2[lo_ - 1:lo_]
            qj = q[lo_:hi_] * jnp.exp2(b2[lo_:hi_] - ref_b)
            kk = k[cs:lo_] * jnp.exp2(ref_b - b2[cs:lo_])
            parts = ([jnp.zeros((cs, LANES), F32)] if cs else []) + [kk, jnp.zeros((r - lo_, LANES), F32)]
            piece = piece + _dot_nt(qj.astype(BF16), jnp.concatenate(parts, axis=0).astype(BF16))
        pieces.append(piece)
    sc = jnp.concatenate(pieces, axis=0)
    o = _dot(sc.astype(BF16), v.astype(BF16))

    inter = []
    for i in range(nseq):
        lo_, hi_ = i * ch, (i + 1) * ch
        bi = b2[lo_:hi_]
        bl = bi[ch - 1:ch]
        st = st_ref[hh, i]
        inter.append(_dot_nt((q[lo_:hi_] * jnp.exp2(bi)).astype(BF16), st.astype(BF16)))
        kh = _pad_rows(k[lo_:hi_] * jnp.exp2(bl - bi), r)
        vi = _pad_rows(v[lo_:hi_], r)
        st_ref[hh, i] = st * jnp.exp2(bl) + _dot_tn(vi.astype(BF16), kh.astype(BF16))
    o = o + (inter[0] if nseq == 1 else jnp.concatenate(inter, axis=0))

    ms = jnp.mean(o * o, axis=-1, keepdims=True)
    o_ref[:, cols] = o * lax.rsqrt(ms + RMS_EPS) * nw_ref[hh] * _sigmoid(g_ref[:, cols])


def _seq_tiling(bsz, l):
    r = ROW_TILE
    ch = min(l, r)
    nseq = r // ch
    nc = max(l // r, 1)
    nb = bsz * l // (r * nc)
    return ch, nseq, nc, nb


def _carry_alias(new_all, n_in):
    return [pl.BlockSpec(memory_space=pl.ANY)], [new_all], {n_in: 1}


def _hgrn(xw, s0_all, new_all, li, la, l1m, oml, nw, bsz, l, valid, base_tile):
    r = ROW_TILE
    ch, nseq, nc, nb = _seq_tiling(bsz, l)
    sub = min(ch, HGRN_SUB)
    hps = HGRN_HEADS_PER_STEP
    w = hps * LANES
    extra_specs, extra_args, aliases = _carry_alias(new_all, 9)
    kern = functools.partial(_hgrn_kernel, hps=hps, ch=ch, sub=sub, nseq=nseq, valid=valid, nc=nc)

    def xspec(off):
        return pl.BlockSpec((r, w), lambda bb, h, c: (base_tile + bb * nc + c, off // w + h))

    def pspec():
        return pl.BlockSpec((hps, 1, LANES), lambda bb, h, c: (h, 0, 0))

    return pl.pallas_call(
        kern,
        grid=(nb, HG_HEADS // hps, nc),
        in_specs=[xspec(OFF_Q), xspec(OFF_F), xspec(OFF_V), xspec(OFF_G),
                  pspec(), pspec(), pspec(), pspec(),
                  pl.BlockSpec((1, nseq, hps, HG_DK, HG_DV), lambda bb, h, c: (li, bb, h, 0, 0))] + extra_specs,
        out_specs=[pl.BlockSpec((r, w), lambda bb, h, c: (bb * nc + c, h)),
                   pl.BlockSpec((1, nseq, hps, HG_DK, HG_DV), lambda bb, h, c: (li, bb, h, 0, 0))],
        out_shape=[jax.ShapeDtypeStruct((bsz * l, HG_DIM), F32),
                   jax.ShapeDtypeStruct(s0_all.shape, F32)],
        scratch_shapes=[pltpu.VMEM((hps, nseq, HG_DV, HG_DK), F32),
                        pltpu.VMEM((hps, HALO + r, LANES), F32), pltpu.VMEM((hps, HALO + r, LANES), F32)],
        input_output_aliases=aliases,
        compiler_params=_cparams(("arbitrary", "arbitrary", "arbitrary")),
        name="hgrn2",
    )(xw, xw, xw, xw, la, l1m, oml, nw, s0_all, *extra_args)


def _ssd_kernel(*refs, **kw):
    _ssd_body(*(refs[:10] + refs[11:]), **kw)


def _ssd_body(xs_ref, bm_ref, cm_ref, z_ref, dt_ref, dtb_ref, a_ref, dsk_ref, nw_ref, h0_ref,
              y_ref, hout_ref, h_ref, al_scr, dl_scr, *, gps, nc, **kw):
    c = pl.program_id(2)

    @pl.when(c == 0)
    def _():
        h_ref[...] = h0_ref[0]

    for gg in range(gps):
        _ssd_group(gg, xs_ref, bm_ref, cm_ref, z_ref, dt_ref, dtb_ref, a_ref, dsk_ref, nw_ref, y_ref, h_ref,
                   al_scr, dl_scr, **kw)

    @pl.when(c == nc - 1)
    def _():
        hout_ref[0] = h_ref[...]


def _ssd_group(gg, xs_ref, bm_ref, cm_ref, z_ref, dt_ref, dtb_ref, a_ref, dsk_ref, nw_ref, y_ref, h_ref,
               al_scr, dl_scr, *, ch, nseq, valid):
    r = ROW_TILE
    scol = slice(gg * LANES, (gg + 1) * LANES)
    goff = gg * GROUP_W
    poff = gg * PAIRS_PER_GROUP
    shift = ch.bit_length() - 1
    rowl = lax.broadcasted_iota(jnp.int32, (r, LANES), 0)
    lane = lax.broadcasted_iota(jnp.int32, (r, LANES), 1)
    rc = rowl & (ch - 1)
    dtr = dt_ref[:, scol] + dtb_ref[:, scol]
    dt = jnp.maximum(dtr, 0.0) + _log1p(jnp.exp(-jnp.abs(dtr)))
    if valid < ch:
        dt = jnp.where(rc < valid, dt, 0.0)
    mask = (lane <= rowl) & ((rowl >> shift) == (lane >> shift))
    tri = jnp.where(mask, 1.0, 0.0).astype(BF16)
    hi, mid, lo = _split3(dt * a_ref[:, scol])
    acum = _dot(tri, hi) + _dot(tri, mid) + _dot(tri, lo)
    acum_t = acum.T
    dt_t = dt.T

    even_lane = lane < SSD_HEADDIM
    even_row = rowl < SSD_HEADDIM
    last_row = rc == ch - 1

    bmb = bm_ref[0, :, scol].astype(BF16)
    cmb = cm_ref[0, :, scol].astype(BF16)
    cb = _dot_nt(cmb, bmb)
    xs = xs_ref[0, :, goff:goff + GROUP_W]
    z = z_ref[:, goff:goff + GROUP_W]
    for pp in range(PAIRS_PER_GROUP):
        xp = xs[:, pp * LANES:(pp + 1) * LANES]
        acc = dsk_ref[:, goff + pp * LANES:goff + (pp + 1) * LANES] * xp
        for e in range(2):
            hd = 2 * pp + e
            w = cb * jnp.exp(jnp.where(mask, acum[:, hd:hd + 1] - acum_t[hd:hd + 1, :], -jnp.inf)) * dt_t[hd:hd + 1, :]
            xm = jnp.where(even_lane if e == 0 else jnp.logical_not(even_lane), xp, 0.0)
            acc = acc + _dot(w.astype(BF16), xm.astype(BF16))
        a_lane = jnp.where(even_lane, acum[:, 2 * pp:2 * pp + 1], acum[:, 2 * pp + 1:2 * pp + 2])
        dt_lane = jnp.where(even_lane, dt[:, 2 * pp:2 * pp + 1], dt[:, 2 * pp + 1:2 * pp + 2])
        ea = jnp.exp(a_lane)

        def seq_body(i, acc, pp=pp, xp=xp, a_lane=a_lane, dt_lane=dt_lane, ea=ea):
            in_seq = (rowl >> shift) == i
            al_lane = jnp.sum(jnp.where(in_seq & last_row, a_lane, 0.0), axis=0, keepdims=True)
            hp = h_ref[i, poff + pp]
            yi = _dot_nt(cmb, hp.astype(BF16)) * ea
            acc = acc + jnp.where(in_seq, yi, 0.0)
            xw = jnp.where(in_seq, xp * (jnp.exp(al_lane - a_lane) * dt_lane), 0.0)
            dec = jnp.where(even_row, jnp.exp(al_lane[:, 0:1]), jnp.exp(al_lane[:, LANES - 1:LANES]))
            h_ref[i, poff + pp] = dec * hp + _dot_tn(xw.astype(BF16), bmb)
            return acc

        if nseq == 1:
            acc = seq_body(0, acc)
        else:
            al_scr[pp] = a_lane
            dl_scr[pp] = dt_lane
        y_ref[:, goff + pp * LANES:goff + (pp + 1) * LANES] = acc

    if nseq > 1:
        def short_body(i, carry):
            rows = pl.ds(pl.multiple_of(i * ch, ch), ch)
            cm_i = _pad_rows(cm_ref[0, rows, scol], 2 * SUBLANES).astype(BF16)
            bm_i = _pad_rows(bm_ref[0, rows, scol], 2 * SUBLANES).astype(BF16)
            for pp in range(PAIRS_PER_GROUP):
                cols = slice(goff + pp * LANES, goff + (pp + 1) * LANES)
                a_i = al_scr[pp, rows, :]
                al = a_i[ch - 1:ch]
                hp = h_ref[i, poff + pp]
                yi = _dot_nt(cm_i, hp.astype(BF16))[:ch] * jnp.exp(a_i)
                y_ref[rows, cols] = y_ref[rows, cols] + yi
                xw = xs_ref[0, rows, cols] * (jnp.exp(al - a_i) * dl_scr[pp, rows, :])
                dec = jnp.where(even_row, jnp.exp(al[:, 0:1]), jnp.exp(al[:, LANES - 1:LANES]))
                h_ref[i, poff + pp] = dec * hp + _dot_tn(_pad_rows(xw, 2 * SUBLANES).astype(BF16), bm_i)
            return carry
        lax.fori_loop(0, nseq, short_body, 0)

    y = y_ref[:, goff:goff + GROUP_W] * (z * _sigmoid(z))
    ms = jnp.mean(y * y, axis=-1, keepdims=True)
    y_ref[:, goff:goff + GROUP_W] = y * lax.rsqrt(ms + RMS_EPS) * nw_ref[:, goff:goff + GROUP_W]


def _ssd(xc3, xw, h0_all, new_all, li, dtb, a, dsk, nw, bsz, l, valid, base_tile):
    r = ROW_TILE
    ch, nseq, nc, nb = _seq_tiling(bsz, l)
    gps = SSD_GROUPS_PER_STEP if nseq == 1 else 1
    gw, sw, pairs = gps * GROUP_W, gps * SSD_STATE, gps * PAIRS_PER_GROUP
    extra_specs, extra_args, aliases = _carry_alias(new_all, 10)
    kern = functools.partial(_ssd_kernel, gps=gps, ch=ch, nseq=nseq, valid=valid, nc=nc)
    hspec = pl.BlockSpec((1, nseq, pairs, LANES, SSD_STATE), lambda bb, g, c: (li, bb, g, 0, 0))
    b_off = SSD_INNER // sw
    c_off = b_off + SSD_GROUPS // gps

    def vec(w):
        return pl.BlockSpec((1, w), lambda bb, g, c: (0, g))

    return pl.pallas_call(
        kern,
        grid=(nb, SSD_GROUPS // gps, nc),
        in_specs=[pl.BlockSpec((1, r, gw), lambda bb, g, c: (bb * nc + c, 0, g)),
                  pl.BlockSpec((1, r, sw), lambda bb, g, c: (bb * nc + c, 0, b_off + g)),
                  pl.BlockSpec((1, r, sw), lambda bb, g, c: (bb * nc + c, 0, c_off + g)),
                  pl.BlockSpec((r, gw), lambda bb, g, c: (base_tile + bb * nc + c, OFF_Z // gw + g)),
                  pl.BlockSpec((r, sw), lambda bb, g, c: (base_tile + bb * nc + c, OFF_DT // sw + g)),
                  vec(sw), vec(sw), vec(gw), vec(gw), hspec] + extra_specs,
        out_specs=[pl.BlockSpec((r, gw), lambda bb, g, c: (bb * nc + c, g)), hspec],
        out_shape=[jax.ShapeDtypeStruct((bsz * l, SSD_INNER), F32),
                   jax.ShapeDtypeStruct(h0_all.shape, F32)],
        scratch_shapes=[pltpu.VMEM((nseq, pairs, LANES, SSD_STATE), F32),
                        pltpu.VMEM((PAIRS_PER_GROUP, r, LANES), F32), pltpu.VMEM((PAIRS_PER_GROUP, r, LANES), F32)],
        input_output_aliases=aliases,
        compiler_params=_cparams(("arbitrary", "arbitrary", "arbitrary")),
        name="ssd",
    )(xc3, xc3, xc3, xw, xw, dtb, a, dsk, nw, h0_all, *extra_args)


def _layer_norm(x, g, b):
    mu = jnp.mean(x, axis=-1, keepdims=True)
    xc = x - mu
    var = jnp.mean(xc * xc, axis=-1, keepdims=True)
    return xc * lax.rsqrt(var + LN_EPS) * g + b


def _postmix_kernel(op_ref, os_ref, yp_ref, ys_ref, ga_ref, gb_ref, x_ref, hgp_ref, ssp_ref, wo_ref, g_ref, b_ref,
                    rwh_ref, rwl_ref, rb_ref, h_ref, h3_ref, lg_ref, *, n_p):
    first = pl.program_id(0) < n_p
    o = jnp.where(first, op_ref[...], os_ref[...])
    y = jnp.where(first, yp_ref[...], ys_ref[...])
    out_a = _dot(o.astype(BF16), hgp_ref[...])
    out_b = _dot(y.astype(BF16), ssp_ref[...])
    merged = _sigmoid(ga_ref[...]) * out_a + _sigmoid(gb_ref[...]) * out_b
    mix = _dot(merged.astype(BF16), wo_ref[...])
    h = _layer_norm(ALPHA * x_ref[...] + mix, g_ref[...], b_ref[...])
    h_ref[...] = h
    _store_rows(h3_ref, h)
    hh = h.astype(BF16)
    hl = (h - hh.astype(F32)).astype(BF16)
    lg_ref[...] = (_dot(hh, rwh_ref[...]) + _dot(hl, rwh_ref[...]) + _dot(hh, rwl_ref[...])) + rb_ref[...]


def _postmix(o_p, o_s, y_p, y_s, xw, x, hgp, ssp, wo, g, b, rwh, rwl, rb):
    t = x.shape[0]
    tm = TOKEN_TILE
    n_p = o_p.shape[0] // tm

    def full(a):
        return pl.BlockSpec(a.shape, lambda i: (0, 0))

    def first(w):
        return pl.BlockSpec((tm, w), lambda i: (jnp.minimum(i, n_p - 1), 0))

    def second(w):
        return pl.BlockSpec((tm, w), lambda i: (jnp.maximum(i - n_p, 0), 0))

    return pl.pallas_call(
        functools.partial(_postmix_kernel, n_p=n_p),
        grid=(t // tm,),
        in_specs=[first(HG_DIM), second(HG_DIM), first(SSD_INNER), second(SSD_INNER),
                  pl.BlockSpec((tm, D_MODEL), lambda i: (i, OFF_GA // D_MODEL)),
                  pl.BlockSpec((tm, D_MODEL), lambda i: (i, OFF_GB // D_MODEL)),
                  pl.BlockSpec((tm, D_MODEL), lambda i: (i, 0)),
                  full(hgp), full(ssp), full(wo), full(g), full(b), full(rwh), full(rwl), full(rb)],
        out_specs=[pl.BlockSpec((tm, D_MODEL), lambda i: (i, 0)),
                   pl.BlockSpec((tm, ROW_SUB, LANES), lambda i: (i, 0, 0)),
                   pl.BlockSpec((tm, LANES), lambda i: (i, 0))],
        out_shape=[jax.ShapeDtypeStruct((t, D_MODEL), F32),
                   jax.ShapeDtypeStruct((t, ROW_SUB, LANES), F32),
                   jax.ShapeDtypeStruct((t, LANES), F32)],
        compiler_params=_cparams(("arbitrary",)),
        name="postmix",
    )(o_p, o_s, y_p, y_s, xw, xw, x, hgp, ssp, wo, g, b, rwh, rwl, rb)


def _moe_kernel(be_ref, nu_ref, tokc_ref, tokn_ref, dstp_ref, dstc_ref, h_hbm, wgu_ref, bgu_ref, wd_ref, bd_ref,
                ys_hbm, xbuf, obuf, wgu_b, wd_b, gsem, ssem, *, nblk, trash_row):
    j = pl.program_id(0)
    nu = nu_ref[0]
    slot = j & 1
    nslot = 1 - slot

    def gather_start(tok_ref, s, i):
        pltpu.make_async_copy(h_hbm.at[pl.ds(tok_ref[0, 0, i], 1)], xbuf.at[s, pl.ds(i, 1)], gsem.at[s]).start()

    def scatter_start(dst_ref, s, i):
        pltpu.make_async_copy(obuf.at[s, pl.ds(i, 1)], ys_hbm.at[pl.ds(dst_ref[0, 0, i], 1)], ssem.at[s]).start()

    def gather_wait(s):
        pltpu.make_async_copy(h_hbm.at[pl.ds(0, MOE_BM)], xbuf.at[s], gsem.at[s]).wait()

    def scatter_wait(s):
        pltpu.make_async_copy(obuf.at[s], ys_hbm.at[pl.ds(0, MOE_BM)], ssem.at[s]).wait()

    @pl.when(j == 0)
    def _():
        obuf[...] = jnp.zeros(obuf.shape, F32)
        pltpu.make_async_copy(obuf.at[0], ys_hbm.at[pl.ds(trash_row, MOE_BM)], ssem.at[0]).start()

        def body(i, carry):
            gather_start(tokc_ref, 0, i)
            return carry
        lax.fori_loop(0, MOE_BM, body, 0, unroll=8)

    first_of_expert = (j == 0) | (be_ref[j] != be_ref[jnp.maximum(j - 1, 0)])

    @pl.when((j < nu) & first_of_expert)
    def _():
        rows = D_MODEL // 8
        for c in range(8):
            wgu_b[c * rows:(c + 1) * rows] = wgu_ref[0, 0, c * rows:(c + 1) * rows].astype(BF16)
            wd_b[c * rows:(c + 1) * rows] = wd_ref[0, 0, c * rows:(c + 1) * rows].astype(BF16)

    @pl.when(j < nu)
    def _():
        gather_wait(slot)
        scatter_wait(slot)
        for i in range(MOE_BM):
            gather_start(tokn_ref, nslot, i)
            scatter_start(dstp_ref, nslot, i)
        gu = _dot(_load_rows(xbuf.at[slot]).astype(BF16), wgu_b[...]) + bgu_ref[0, 0]
        gate = jnp.minimum(gu[:, :D_FF], SWIGLU_LIMIT)
        up = jnp.clip(gu[:, D_FF:], -SWIGLU_LIMIT, SWIGLU_LIMIT)
        act = (up + 1.0) * gate * _sigmoid(SWIGLU_ALPHA * gate)
        _store_rows(obuf.at[slot], _dot(act.astype(BF16), wd_b[...]) + bd_ref[0, 0])

    @pl.when(j == nu - 1)
    def _():
        def body(i, carry):
            scatter_start(dstc_ref, slot, i)
            return carry
        lax.fori_loop(0, MOE_BM, body, 0, unroll=8)

    @pl.when(j == nblk - 1)
    def _():
        scatter_wait((nu - 1) & 1)
        scatter_wait(nu & 1)
        gather_wait(nu & 1)


def _moe_experts(block_e, n_used, row_tok, row_dst, row_dst_prev, h, wgu, bgu, wd, bd, li):
    t = h.shape[0]
    nblk = row_tok.shape[0]
    trash_row = TOP_K * t
    kern = functools.partial(_moe_kernel, nblk=nblk, trash_row=trash_row)

    def smem(imap):
        return pl.BlockSpec((1, 1, MOE_BM), imap, memory_space=pltpu.SMEM)

    grid_spec = pltpu.PrefetchScalarGridSpec(
        num_scalar_prefetch=2,
        grid=(nblk,),
        in_specs=[smem(lambda j, be, nu: (j, 0, 0)),
                  smem(lambda j, be, nu: (jnp.minimum(j + 1, nblk - 1), 0, 0)),
                  smem(lambda j, be, nu: (j, 0, 0)),
                  smem(lambda j, be, nu: (j, 0, 0)),
                  pl.BlockSpec(memory_space=pl.ANY),
                  pl.BlockSpec((1, 1, D_MODEL, 2 * D_FF), lambda j, be, nu: (li, be[j], 0, 0)),
                  pl.BlockSpec((1, 1, 1, 2 * D_FF), lambda j, be, nu: (li, be[j], 0, 0)),
                  pl.BlockSpec((1, 1, D_FF, D_MODEL), lambda j, be, nu: (li, be[j], 0, 0)),
                  pl.BlockSpec((1, 1, 1, D_MODEL), lambda j, be, nu: (li, be[j], 0, 0))],
        out_specs=pl.BlockSpec(memory_space=pl.ANY),
        scratch_shapes=[pltpu.VMEM((2, MOE_BM, ROW_SUB, LANES), F32), pltpu.VMEM((2, MOE_BM, ROW_SUB, LANES), F32),
                        pltpu.VMEM((D_MODEL, 2 * D_FF), BF16), pltpu.VMEM((D_FF, D_MODEL), BF16),
                        pltpu.SemaphoreType.DMA((2,)), pltpu.SemaphoreType.DMA((2,))],
    )
    return pl.pallas_call(
        kern,
        grid_spec=grid_spec,
        out_shape=jax.ShapeDtypeStruct((trash_row + 2 * MOE_BM, ROW_SUB, LANES), F32),
        compiler_params=pltpu.CompilerParams(dimension_semantics=("arbitrary",), vmem_limit_bytes=MOE_VMEM_LIMIT),
        name="moe_experts",
    )(block_e, n_used, row_tok, row_tok, row_dst_prev, row_dst, h, wgu, bgu, wd, bd)


def _combine_kernel(h_ref, y0_ref, y1_ref, y2_ref, y3_ref, gt_ref, g_ref, b_ref, x_ref, xb_ref):
    gt = gt_ref[...]
    y = gt[:, 0:1] * _load_rows(y0_ref)
    for kk, y_ref in enumerate((y1_ref, y2_ref, y3_ref), start=1):
        y = y + gt[:, kk:kk + 1] * _load_rows(y_ref)
    x = _layer_norm(ALPHA * h_ref[...] + y, g_ref[...], b_ref[...])
    x_ref[...] = x
    xb_ref[...] = x.astype(BF16)


def _combine(h, ys, gates, g, b, tm):
    t = h.shape[0]
    nt = t // tm

    def yspec(kk):
        return pl.BlockSpec((tm, ROW_SUB, LANES), lambda i: (kk * nt + i, 0, 0))

    return pl.pallas_call(
        _combine_kernel,
        grid=(nt,),
        in_specs=[pl.BlockSpec((tm, D_MODEL), lambda i: (i, 0)),
                  yspec(0), yspec(1), yspec(2), yspec(3),
                  pl.BlockSpec((tm, TOP_K), lambda i: (i, 0)),
                  pl.BlockSpec((1, D_MODEL), lambda i: (0, 0)),
                  pl.BlockSpec((1, D_MODEL), lambda i: (0, 0))],
        out_specs=[pl.BlockSpec((tm, D_MODEL), lambda i: (i, 0)),
                   pl.BlockSpec((tm, D_MODEL), lambda i: (i, 0))],
        out_shape=[jax.ShapeDtypeStruct((t, D_MODEL), F32),
                   jax.ShapeDtypeStruct((t, D_MODEL), BF16)],
        compiler_params=_cparams(("arbitrary",)),
        name="combine_ln",
    )(h, ys, ys, ys, ys, gates, g, b)


def _split_cols(a, sizes):
    out, off = [], 0
    for s in sizes:
        out.append(a[..., off:off + s])
        off += s
    return out


def _per_group(vec_heads):
    v = vec_heads.reshape(SSD_GROUPS, SSD_HPG)
    return jnp.pad(v, ((0, 0), (0, LANES - SSD_HPG))).reshape(1, DT_W)


def _prep_layer(p, l):
    wq, wf, wv, wg, wz, wxbc, wdt, wga, wgb = _split_cols(p["w_in"][l], IN_SPLITS)
    wdt = jnp.pad(wdt.reshape(D_MODEL, SSD_GROUPS, SSD_HPG), ((0, 0), (0, 0), (0, LANES - SSD_HPG)))
    w_in = jnp.concatenate([wq, wf, wv, wg, wz, wxbc, wga, wgb, wdt.reshape(D_MODEL, DT_W)], axis=1).astype(BF16)
    lb = p["lb_all"][l].reshape(HG_HEADS, 1, HG_DK)
    rw = jnp.pad(p["router_w"][l], ((0, 0), (0, LANES - N_EXPERTS)))
    rwh = rw.astype(BF16)
    rwl = (rw - rwh.astype(F32)).astype(BF16)
    return dict(
        w_in=w_in,
        la=jnp.log(lb), l1m=jnp.log1p(-lb), oml=1.0 - lb,
        hg_nw=p["hg_norm_w"][l].reshape(HG_HEADS, 1, HG_DV),
        hgp=p["hg_proj"][l].astype(BF16),
        conv_w=p["conv_w"][l], conv_b=p["conv_b"][l].reshape(1, CONV_DIM),
        dtb=_per_group(p["dt_bias"][l]),
        a=_per_group(-jnp.exp(p["a_log"][l].astype(F32))),
        dsk=jnp.repeat(p["d_skip"][l], SSD_HEADDIM).reshape(1, SSD_INNER),
        ssd_nw=p["ssd_norm_w"][l].reshape(1, SSD_INNER),
        ssp=p["ssd_proj"][l].astype(BF16),
        wo=p["w_out"][l].astype(BF16),
        ln1_g=p["ln1_g"][l].reshape(1, D_MODEL), ln1_b=p["ln1_b"][l].reshape(1, D_MODEL),
        rwh=rwh, rwl=rwl,
        rb=jnp.pad(p["router_b"][l], (0, LANES - N_EXPERTS)).reshape(1, LANES),
        ln2_g=p["ln2_g"][l].reshape(1, D_MODEL), ln2_b=p["ln2_b"][l].reshape(1, D_MODEL),
    )


def _prep_experts(p):
    return dict(
        wgu=p["w_gu"], bgu=p["b_gu"].reshape(DEPTH, N_EXPERTS, 1, 2 * D_FF),
        wd=p["w_down"], bd=p["b_down"].reshape(DEPTH, N_EXPERTS, 1, D_MODEL),
    )


def _moe(h, h3, logits, lp, ep, li, tm):
    t = h.shape[0]
    s = t * TOP_K
    top_v, top_e = lax.top_k(logits[:, :N_EXPERTS], TOP_K)
    gates = jax.nn.softmax(top_v, axis=-1)
    e_slot = top_e.reshape(s).astype(jnp.int32)
    order = jnp.argsort(e_slot).astype(jnp.int32)
    counts = jnp.sum((e_slot[:, None] == jnp.arange(N_EXPERTS, dtype=jnp.int32)[None, :]).astype(jnp.int32), axis=0)
    padded = (counts + MOE_BM - 1) // MOE_BM * MOE_BM
    pend = jnp.cumsum(padded)
    cend = pend - padded + counts
    n_blocks = (s + N_EXPERTS * (MOE_BM - 1) + MOE_BM - 1) // MOE_BM
    nrows = n_blocks * MOE_BM
    rows = jnp.arange(nrows, dtype=jnp.int32)[:, None]
    before = pend[None, :] <= rows
    pad_before = jnp.sum(jnp.where(before, padded - counts, 0), axis=1)
    is_pad = jnp.any((cend[None, :] <= rows) & (rows < pend[None, :]), axis=1) | (rows[:, 0] >= pend[-1])
    slot = order[jnp.clip(rows[:, 0] - pad_before, 0, s - 1)]
    tok = slot // TOP_K
    row_tok = jnp.where(is_pad, 0, tok)
    row_dst = jnp.where(is_pad, TOP_K * t + rows[:, 0] % (2 * MOE_BM), (slot - tok * TOP_K) * t + tok)
    blk_rows = jnp.arange(n_blocks, dtype=jnp.int32)[:, None] * MOE_BM
    block_e = jnp.minimum(jnp.sum((pend[None, :] <= blk_rows).astype(jnp.int32), axis=1), N_EXPERTS - 1)
    n_used = (pend[-1:] // MOE_BM).astype(jnp.int32)
    first_prev = TOP_K * t + MOE_BM + jnp.arange(MOE_BM, dtype=jnp.int32)
    row_dst_prev = jnp.concatenate([first_prev, row_dst[:-MOE_BM]])
    ys = _moe_experts(block_e, n_used, row_tok.reshape(n_blocks, 1, MOE_BM), row_dst.reshape(n_blocks, 1, MOE_BM),
                      row_dst_prev.reshape(n_blocks, 1, MOE_BM), h3, ep["wgu"], ep["bgu"], ep["wd"], ep["bd"], li)
    return _combine(h, ys, gates, lp["ln2_g"], lp["ln2_b"], tm)


def _mixers(xw, li, lp, grp, new):
    bsz, l, valid, base = grp["bsz"], grp["l"], grp["valid"], grp["base"]
    t = bsz * l
    prev8 = jnp.pad(grp["s_conv"][li], ((0, 0), (SUBLANES - (CONV_K - 1), 0), (0, 0)))
    xc = _conv(xw, prev8, lp["conv_w"], lp["conv_b"], bsz, l, base)
    tail = base + jnp.arange(bsz, dtype=jnp.int32)[:, None] * l + jnp.arange(valid - (CONV_K - 1), valid)[None, :]
    conv_new = xw[tail.reshape(-1)][:, OFF_XBC:OFF_XBC + CONV_DIM].reshape(bsz, CONV_K - 1, CONV_DIM)
    o, hg = _hgrn(xw, grp["s_hg"], new["hg"], li, lp["la"], lp["l1m"], lp["oml"], lp["hg_nw"], bsz, l, valid,
                  base // ROW_TILE)
    y, ssm = _ssd(xc.reshape(t // ROW_TILE, ROW_TILE, CONV_DIM), xw, grp["s_ssm"], new["ssm"], li, lp["dtb"],
                  lp["a"], lp["dsk"], lp["ssd_nw"], bsz, l, valid, base // ROW_TILE)
    return o, y, dict(hg=hg, ssm=ssm, conv=new["conv"] + [conv_new])


def _forward(x, groups, layers, ep):
    t = x.shape[0]
    xb = x.astype(BF16)
    new = [dict(hg=jnp.zeros(g["s_hg"].shape, F32), ssm=jnp.zeros(g["s_ssm"].shape, F32), conv=[]) for g in groups]
    for li, lp in enumerate(layers):
        xw = _matmul(xb, lp["w_in"], 512 if t % 512 == 0 else TOKEN_TILE, N_W // 4)
        mix = [_mixers(xw, li, lp, grp, n) for grp, n in zip(groups, new)]
        new = [m[2] for m in mix]
        h, h3, logits = _postmix(mix[0][0], mix[1][0], mix[0][1], mix[1][1], xw, x, lp["hgp"], lp["ssp"], lp["wo"],
                             lp["ln1_g"], lp["ln1_b"], lp["rwh"], lp["rwl"], lp["rb"])
        x, xb = _moe(h, h3, logits, lp, ep, li, TOKEN_TILE)
    return x, [dict(hg=n["hg"], ssm=n["ssm"], conv=jnp.stack(n["conv"])) for n in new]


def kernel(x_prompt, x_sample, state_hgrn, state_ssm, state_conv, hg_lower_bounds, w_in, hg_norm_w, hg_proj,
           conv_w, conv_b, dt_bias, a_log, d_skip, ssd_norm_w, ssd_proj, w_out, ln1_g, ln1_b, router_w,
           router_b, w_gu, b_gu, w_down, b_down, ln2_g, ln2_b):
    lb_all = jnp.cumsum(jax.nn.softmax(hg_lower_bounds.astype(F32), axis=0), axis=0)
    lb_all = lb_all - lb_all[0]
    p = dict(lb_all=lb_all, w_in=w_in, hg_norm_w=hg_norm_w, hg_proj=hg_proj, conv_w=conv_w, conv_b=conv_b,
             dt_bias=dt_bias, a_log=a_log, d_skip=d_skip, ssd_norm_w=ssd_norm_w, ssd_proj=ssd_proj, w_out=w_out,
             ln1_g=ln1_g, ln1_b=ln1_b, router_w=router_w, router_b=router_b, w_gu=w_gu, b_gu=b_gu,
             w_down=w_down, b_down=b_down, ln2_g=ln2_g, ln2_b=ln2_b)
    layers = [_prep_layer(p, l) for l in range(DEPTH)]
    ep = _prep_experts(p)

    return _run(x_prompt, x_sample, state_hgrn, state_ssm, state_conv, layers, ep)


def _run(x_prompt, x_sample, state_hgrn, state_ssm, state_conv, layers, ep):
    bp, lprompt, _ = x_prompt.shape
    bs, ls, _ = x_sample.shape
    tp = bp * lprompt

    def pairs(s_ssm):
        return s_ssm.reshape(s_ssm.shape[:2] + (SSD_PAIRS, LANES, SSD_STATE))

    groups = [
        dict(bsz=bp, l=lprompt, valid=lprompt, base=0,
             s_hg=jnp.zeros((DEPTH, bp) + state_hgrn.shape[2:], F32),
             s_ssm=pairs(jnp.zeros((DEPTH, bp) + state_ssm.shape[2:], F32)),
             s_conv=jnp.zeros((DEPTH, bp) + state_conv.shape[2:], F32)),
        dict(bsz=bs, l=SUBLANES, valid=ls, base=tp, s_hg=state_hgrn, s_ssm=pairs(state_ssm), s_conv=state_conv),
    ]
    xs = jnp.pad(x_sample, ((0, 0), (0, SUBLANES - ls), (0, 0)))
    x = jnp.concatenate([x_prompt.reshape(tp, D_MODEL), xs.reshape(bs * SUBLANES, D_MODEL)], axis=0)
    x, new = _forward(x, groups, layers, ep)
    y_p = x[:tp].reshape(bp, lprompt, D_MODEL)
    y_s = x[tp:].reshape(bs, SUBLANES, D_MODEL)[:, :ls]
    (n_p, n_s) = new
    return (y_p, y_s, n_p["hg"], n_p["ssm"].reshape((DEPTH, bp) + state_ssm.shape[2:]), n_p["conv"],
            n_s["hg"], n_s["ssm"].reshape(state_ssm.shape), n_s["conv"])
```

```python
import functools

import jax
import jax.numpy as jnp
from jax import lax
from jax.experimental import pallas as pl
from jax.experimental.pallas import tpu as pltpu

F32 = jnp.float32
BF16 = jnp.bfloat16

D_MODEL = 1024
DEPTH = 2
HG_HEADS = 8
HG_DK = 128
HG_DV = 128
HG_DIM = HG_HEADS * HG_DK
SSD_INNER = 2 * D_MODEL
SSD_HEADDIM = 64
SSD_HEADS = SSD_INNER // SSD_HEADDIM
SSD_GROUPS = 4
SSD_HPG = SSD_HEADS // SSD_GROUPS
SSD_STATE = 128
SSD_PAIRS = SSD_HEADS // 2
PAIRS_PER_GROUP = SSD_PAIRS // SSD_GROUPS
GROUP_W = SSD_INNER // SSD_GROUPS
CONV_K = 4
CONV_DIM = SSD_INNER + 2 * SSD_GROUPS * SSD_STATE
N_EXPERTS = 32
TOP_K = 4
D_FF = D_MODEL
SWIGLU_LIMIT = 7.0
SWIGLU_ALPHA = 1.702
ALPHA = (2.0 * DEPTH) ** 0.25
LN_EPS = 1e-5
RMS_EPS = 1e-6

LANES = 128
SUBLANES = 8
ROW_SUB = D_MODEL // LANES

DT_W = SSD_GROUPS * LANES
OFF_Q = 0
OFF_F = HG_DIM
OFF_V = 2 * HG_DIM
OFF_G = 3 * HG_DIM
OFF_Z = 4 * HG_DIM
OFF_XBC = OFF_Z + SSD_INNER
OFF_GA = OFF_XBC + CONV_DIM
OFF_GB = OFF_GA + D_MODEL
OFF_DT = OFF_GB + D_MODEL
N_W = OFF_DT + DT_W
IN_SPLITS = (HG_DIM, HG_DIM, HG_DIM, HG_DIM, SSD_INNER, CONV_DIM, SSD_HEADS, D_MODEL, D_MODEL)

ROW_TILE = 128
CONV_ROWS = 256
TOKEN_TILE = 256
MOE_BM = 256
VMEM_LIMIT = 48 * 1024 * 1024
MOE_VMEM_LIMIT = 56 * 1024 * 1024
LOG2E = 1.4426950408889634
HALO = 16
HGRN_HEADS_PER_STEP = 8
HGRN_SUB = 8
SSD_GROUPS_PER_STEP = 4


def _cparams(sem):
    return pltpu.CompilerParams(dimension_semantics=sem, vmem_limit_bytes=VMEM_LIMIT)


def _sigmoid(x):
    return 0.5 * jnp.tanh(0.5 * x) + 0.5


def _log1p(x):
    return jnp.log(1.0 + x)


def _dot(a, b):
    return jnp.dot(a, b, preferred_element_type=F32)


def _dot_nt(a, b):
    return lax.dot_general(a, b, (((1,), (1,)), ((), ())), preferred_element_type=F32)


def _dot_tn(a, b):
    return lax.dot_general(a, b, (((0,), (0,)), ((), ())), preferred_element_type=F32)


def _load_rows(ref3):
    return pltpu.einshape("tsl->t(sl)", ref3[...])


def _store_rows(ref3, x):
    ref3[...] = pltpu.einshape("t(sl)->tsl", x, s=ROW_SUB)


def _mm_kernel(x_ref, w_ref, o_ref):
    o_ref[...] = _dot(x_ref[...], w_ref[...])


def _matmul(x, w, tm, tn):
    m, k = x.shape
    n = w.shape[1]
    return pl.pallas_call(
        _mm_kernel,
        grid=(n // tn, m // tm),
        in_specs=[pl.BlockSpec((tm, k), lambda j, i: (i, 0)),
                  pl.BlockSpec((k, tn), lambda j, i: (0, j))],
        out_specs=pl.BlockSpec((tm, tn), lambda j, i: (i, j)),
        out_shape=jax.ShapeDtypeStruct((m, n), F32),
        compiler_params=_cparams(("arbitrary", "arbitrary")),
        name="in_proj",
    )(x, w)


def _conv_kernel(x_ref, pb_ref, p0_ref, w_ref, b_ref, o_ref):
    i = pl.program_id(1)
    x = x_ref[...]
    prev = jnp.where(i == 0, p0_ref[0], pb_ref[...])
    full = jnp.concatenate([prev, x], axis=0)
    w = w_ref[...]
    acc = b_ref[...] + w[CONV_K - 1:CONV_K] * x
    for s in range(1, CONV_K):
        xs = pltpu.roll(full, s, 0)[SUBLANES:]
        acc = acc + w[CONV_K - 1 - s:CONV_K - s] * xs
    o_ref[...] = acc * _sigmoid(acc)


def _conv_short_kernel(x_ref, p_ref, w_ref, b_ref, o_ref):
    x = x_ref[...]
    p = p_ref[...]
    rc = lax.broadcasted_iota(jnp.int32, x.shape, 0) & (SUBLANES - 1)
    w = w_ref[...]
    acc = b_ref[...] + w[CONV_K - 1:CONV_K] * x
    for s in range(1, CONV_K):
        xs = jnp.where(rc >= s, pltpu.roll(x, s, 0), pltpu.roll(p, (s - SUBLANES) % ROW_TILE, 0))
        acc = acc + w[CONV_K - 1 - s:CONV_K - s] * xs
    o_ref[...] = acc * _sigmoid(acc)


def _conv(xw, prev8, conv_w, conv_b, bsz, l, base_rows):
    cb = OFF_XBC // CONV_DIM
    wspec = [pl.BlockSpec((CONV_K, CONV_DIM), lambda *_: (0, 0)), pl.BlockSpec((1, CONV_DIM), lambda *_: (0, 0))]
    out_shape = jax.ShapeDtypeStruct((bsz * l, CONV_DIM), F32)
    if l == SUBLANES:
        r = ROW_TILE
        return pl.pallas_call(
            _conv_short_kernel,
            grid=(bsz * l // r,),
            in_specs=[pl.BlockSpec((r, CONV_DIM), lambda i: (base_rows // r + i, cb)),
                      pl.BlockSpec((r, CONV_DIM), lambda i: (i, 0))] + wspec,
            out_specs=pl.BlockSpec((r, CONV_DIM), lambda i: (i, 0)),
            out_shape=out_shape,
            compiler_params=_cparams(("arbitrary",)),
            name="conv_silu_short",
        )(xw, prev8.reshape(bsz * SUBLANES, CONV_DIM), conv_w, conv_b)
    r = CONV_ROWS
    tps = l // r
    rb = r // SUBLANES
    return pl.pallas_call(
        _conv_kernel,
        grid=(bsz, tps),
        in_specs=[pl.BlockSpec((r, CONV_DIM), lambda bi, i: (base_rows // r + bi * tps + i, cb)),
                  pl.BlockSpec((SUBLANES, CONV_DIM),
                               lambda bi, i: (jnp.maximum(base_rows // SUBLANES + (bi * tps + i) * rb - 1, 0), cb)),
                  pl.BlockSpec((1, SUBLANES, CONV_DIM), lambda bi, i: (bi, 0, 0))] + wspec,
        out_specs=pl.BlockSpec((r, CONV_DIM), lambda bi, i: (bi * tps + i, 0)),
        out_shape=out_shape,
        compiler_params=_cparams(("arbitrary", "arbitrary")),
        name="conv_silu",
    )(xw, xw, prev8, conv_w, conv_b)


def _seg_cumsum(x, rc, seg):
    s = 1
    while s < seg:
        x = x + jnp.where(rc >= s, pltpu.roll(x, s, 0), 0.0)
        s *= 2
    return x


def _pad_rows(x, rows):
    if x.shape[0] == rows:
        return x
    return jnp.concatenate([x, jnp.zeros((rows - x.shape[0], x.shape[1]), x.dtype)], axis=0)


def _split3(x):
    hi = x.astype(BF16)
    r1 = x - hi.astype(F32)
    mid = r1.astype(BF16)
    lo = (r1 - mid.astype(F32)).astype(BF16)
    return hi, mid, lo


def _hgrn_kernel(*refs, hps, nseq, nc, **kw):
    refs = refs[:9] + refs[10:]
    s0_ref, sout_ref, st_ref = refs[8], refs[10], refs[11]
    c = pl.program_id(2)

    @pl.when(c == 0)
    def _():
        for hh in range(hps):
            for i in range(nseq):
                st_ref[hh, i] = s0_ref[0, i, hh].T

    for hh in range(hps):
        _hgrn_head(hh, *refs, nseq=nseq, **kw)

    @pl.when(c == nc - 1)
    def _():
        for hh in range(hps):
            for i in range(nseq):
                sout_ref[0, i, hh] = st_ref[hh, i].T


def _hgrn_head(hh, q_ref, f_ref, v_ref, g_ref, la_ref, l1m_ref, oml_ref, nw_ref, s0_ref,
               o_ref, sout_ref, st_ref, k_scr, b_scr, *, ch, sub, nseq, valid):
    r = ROW_TILE
    cols = slice(hh * LANES, (hh + 1) * LANES)
    q = q_ref[:, cols]
    fr = f_ref[:, cols]
    v = v_ref[:, cols]
    q = q * _sigmoid(q)
    la = la_ref[hh]
    lsig = jnp.minimum(fr, 0.0) - _log1p(jnp.exp(-jnp.abs(fr)))
    cc = l1m_ref[hh] + lsig
    logf = jnp.maximum(la, cc) + _log1p(jnp.exp(-jnp.abs(la - cc)))
    k = oml_ref[hh] * _sigmoid(-fr)

    row = lax.broadcasted_iota(jnp.int32, (r, LANES), 0)
    lane = lax.broadcasted_iota(jnp.int32, (r, LANES), 1)
    if valid < ch:
        pad = (row & (ch - 1)) >= valid
        logf = jnp.where(pad, 0.0, logf)
        k = jnp.where(pad, 0.0, k)

    tri = lane <= row
    if ch < r:
        shift = ch.bit_length() - 1
        tri = tri & ((row >> shift) == (lane >> shift))
    tri = jnp.where(tri, 1.0, 0.0).astype(BF16)
    hi, mid, lo = _split3(logf)
    b2 = (_dot(tri, hi) + _dot(tri, mid) + _dot(tri, lo)) * LOG2E

    k_scr[hh, 0:HALO] = jnp.zeros((HALO, LANES), F32)
    b_scr[hh, 0:HALO] = jnp.zeros((HALO, LANES), F32)
    k_scr[hh, HALO:HALO + r] = k
    b_scr[hh, HALO:HALO + r] = b2
    rs = row & (sub - 1)
    scl = jnp.where(lane == rs, jnp.sum(q * k, axis=-1, keepdims=True), 0.0)
    for d in range(1, sub):
        kd = k_scr[hh, pl.ds(HALO - d, r), :]
        bd = b_scr[hh, pl.ds(HALO - d, r), :]
        sd = jnp.sum(q * kd * jnp.exp2(b2 - bd), axis=-1, keepdims=True)
        scl = jnp.where(lane == rs - d, sd, scl)

    spc = ch // sub
    pieces = []
    for m in range(r // sub):
        lo_, hi_ = m * sub, (m + 1) * sub
        piece = scl[lo_:hi_]
        if m:
            piece = pltpu.roll(piece, lo_, 1)
        j = m % spc
        if j:
            cs = (m - j) * sub
            ref_b = b2[lo_ - 1:lo_]
            qj = q[lo_:hi_] * jnp.exp2(b2[lo_:hi_] - ref_b)
            kk = k[cs:lo_] * jnp.exp2(ref_b - b2[cs:lo_])
            parts = ([jnp.zeros((cs, LANES), F32)] if cs else []) + [kk, jnp.zeros((r - lo_, LANES), F32)]
            piece = piece + _dot_nt(qj.astype(BF16), jnp.concatenate(parts, axis=0).astype(BF16))
        pieces.append(piece)
    sc = jnp.concatenate(pieces, axis=0)
    o = _dot(sc.astype(BF16), v.astype(BF16))

    inter = []
    for i in range(nseq):
        lo_, hi_ = i * ch, (i + 1) * ch
        bi = b2[lo_:hi_]
        bl = bi[ch - 1:ch]
        st = st_ref[hh, i]
        inter.append(_dot_nt((q[lo_:hi_] * jnp.exp2(bi)).astype(BF16), st.astype(BF16)))
        kh = _pad_rows(k[lo_:hi_] * jnp.exp2(bl - bi), r)
        vi = _pad_rows(v[lo_:hi_], r)
        st_ref[hh, i] = st * jnp.exp2(bl) + _dot_tn(vi.astype(BF16), kh.astype(BF16))
    o = o + (inter[0] if nseq == 1 else jnp.concatenate(inter, axis=0))

    ms = jnp.mean(o * o, axis=-1, keepdims=True)
    o_ref[:, cols] = o * lax.rsqrt(ms + RMS_EPS) * nw_ref[hh] * _sigmoid(g_ref[:, cols])


def _seq_tiling(bsz, l):
    r = ROW_TILE
    ch = min(l, r)
    nseq = r // ch
    nc = max(l // r, 1)
    nb = bsz * l // (r * nc)
    return ch, nseq, nc, nb


def _carry_alias(new_all, n_in):
    return [pl.BlockSpec(memory_space=pl.ANY)], [new_all], {n_in: 1}


def _hgrn(xw, s0_all, new_all, li, la, l1m, oml, nw, bsz, l, valid, base_tile):
    r = ROW_TILE
    ch, nseq, nc, nb = _seq_tiling(bsz, l)
    sub = min(ch, HGRN_SUB)
    hps = HGRN_HEADS_PER_STEP
    w = hps * LANES
    extra_specs, extra_args, aliases = _carry_alias(new_all, 9)
    kern = functools.partial(_hgrn_kernel, hps=hps, ch=ch, sub=sub, nseq=nseq, valid=valid, nc=nc)

    def xspec(off):
        return pl.BlockSpec((r, w), lambda bb, h, c: (base_tile + bb * nc + c, off // w + h))

    def pspec():
        return pl.BlockSpec((hps, 1, LANES), lambda bb, h, c: (h, 0, 0))

    return pl.pallas_call(
        kern,
        grid=(nb, HG_HEADS // hps, nc),
        in_specs=[xspec(OFF_Q), xspec(OFF_F), xspec(OFF_V), xspec(OFF_G),
                  pspec(), pspec(), pspec(), pspec(),
                  pl.BlockSpec((1, nseq, hps, HG_DK, HG_DV), lambda bb, h, c: (li, bb, h, 0, 0))] + extra_specs,
        out_specs=[pl.BlockSpec((r, w), lambda bb, h, c: (bb * nc + c, h)),
                   pl.BlockSpec((1, nseq, hps, HG_DK, HG_DV), lambda bb, h, c: (li, bb, h, 0, 0))],
        out_shape=[jax.ShapeDtypeStruct((bsz * l, HG_DIM), F32),
                   jax.ShapeDtypeStruct(s0_all.shape, F32)],
        scratch_shapes=[pltpu.VMEM((hps, nseq, HG_DV, HG_DK), F32),
                        pltpu.VMEM((hps, HALO + r, LANES), F32), pltpu.VMEM((hps, HALO + r, LANES), F32)],
        input_output_aliases=aliases,
        compiler_params=_cparams(("arbitrary", "arbitrary", "arbitrary")),
        name="hgrn2",
    )(xw, xw, xw, xw, la, l1m, oml, nw, s0_all, *extra_args)


def _ssd_kernel(*refs, **kw):
    _ssd_body(*(refs[:10] + refs[11:]), **kw)


def _ssd_body(xs_ref, bm_ref, cm_ref, z_ref, dt_ref, dtb_ref, a_ref, dsk_ref, nw_ref, h0_ref,
              y_ref, hout_ref, h_ref, al_scr, dl_scr, *, gps, nc, **kw):
    c = pl.program_id(2)

    @pl.when(c == 0)
    def _():
        h_ref[...] = h0_ref[0]

    for gg in range(gps):
        _ssd_group(gg, xs_ref, bm_ref, cm_ref, z_ref, dt_ref, dtb_ref, a_ref, dsk_ref, nw_ref, y_ref, h_ref,
                   al_scr, dl_scr, **kw)

    @pl.when(c == nc - 1)
    def _():
        hout_ref[0] = h_ref[...]


def _ssd_group(gg, xs_ref, bm_ref, cm_ref, z_ref, dt_ref, dtb_ref, a_ref, dsk_ref, nw_ref, y_ref, h_ref,
               al_scr, dl_scr, *, ch, nseq, valid):
    r = ROW_TILE
    scol = slice(gg * LANES, (gg + 1) * LANES)
    goff = gg * GROUP_W
    poff = gg * PAIRS_PER_GROUP
    shift = ch.bit_length() - 1
    rowl = lax.broadcasted_iota(jnp.int32, (r, LANES), 0)
    lane = lax.broadcasted_iota(jnp.int32, (r, LANES), 1)
    rc = rowl & (ch - 1)
    dtr = dt_ref[:, scol] + dtb_ref[:, scol]
    dt = jnp.maximum(dtr, 0.0) + _log1p(jnp.exp(-jnp.abs(dtr)))
    if valid < ch:
        dt = jnp.where(rc < valid, dt, 0.0)
    mask = (lane <= rowl) & ((rowl >> shift) == (lane >> shift))
    tri = jnp.where(mask, 1.0, 0.0).astype(BF16)
    hi, mid, lo = _split3(dt * a_ref[:, scol])
    acum = _dot(tri, hi) + _dot(tri, mid) + _dot(tri, lo)
    acum_t = acum.T
    dt_t = dt.T

    even_lane = lane < SSD_HEADDIM
    even_row = rowl < SSD_HEADDIM
    last_row = rc == ch - 1

    bmb = bm_ref[0, :, scol].astype(BF16)
    cmb = cm_ref[0, :, scol].astype(BF16)
    cb = _dot_nt(cmb, bmb)
    xs = xs_ref[0, :, goff:goff + GROUP_W]
    z = z_ref[:, goff:goff + GROUP_W]
    for pp in range(PAIRS_PER_GROUP):
        xp = xs[:, pp * LANES:(pp + 1) * LANES]
        acc = dsk_ref[:, goff + pp * LANES:goff + (pp + 1) * LANES] * xp
        for e in range(2):
            hd = 2 * pp + e
            w = cb * jnp.exp(jnp.where(mask, acum[:, hd:hd + 1] - acum_t[hd:hd + 1, :], -jnp.inf)) * dt_t[hd:hd + 1, :]
            xm = jnp.where(even_lane if e == 0 else jnp.logical_not(even_lane), xp, 0.0)
            acc = acc + _dot(w.astype(BF16), xm.astype(BF16))
        a_lane = jnp.where(even_lane, acum[:, 2 * pp:2 * pp + 1], acum[:, 2 * pp + 1:2 * pp + 2])
        dt_lane = jnp.where(even_lane, dt[:, 2 * pp:2 * pp + 1], dt[:, 2 * pp + 1:2 * pp + 2])
        ea = jnp.exp(a_lane)

        def seq_body(i, acc, pp=pp, xp=xp, a_lane=a_lane, dt_lane=dt_lane, ea=ea):
            in_seq = (rowl >> shift) == i
            al_lane = jnp.sum(jnp.where(in_seq & last_row, a_lane, 0.0), axis=0, keepdims=True)
            hp = h_ref[i, poff + pp]
            yi = _dot_nt(cmb, hp.astype(BF16)) * ea
            acc = acc + jnp.where(in_seq, yi, 0.0)
            xw = jnp.where(in_seq, xp * (jnp.exp(al_lane - a_lane) * dt_lane), 0.0)
            dec = jnp.where(even_row, jnp.exp(al_lane[:, 0:1]), jnp.exp(al_lane[:, LANES - 1:LANES]))
            h_ref[i, poff + pp] = dec * hp + _dot_tn(xw.astype(BF16), bmb)
            return acc

        if nseq == 1:
            acc = seq_body(0, acc)
        else:
            al_scr[pp] = a_lane
            dl_scr[pp] = dt_lane
        y_ref[:, goff + pp * LANES:goff + (pp + 1) * LANES] = acc

    if nseq > 1:
        def short_body(i, carry):
            rows = pl.ds(pl.multiple_of(i * ch, ch), ch)
            cm_i = _pad_rows(cm_ref[0, rows, scol], 2 * SUBLANES).astype(BF16)
            bm_i = _pad_rows(bm_ref[0, rows, scol], 2 * SUBLANES).astype(BF16)
            for pp in range(PAIRS_PER_GROUP):
                cols = slice(goff + pp * LANES, goff + (pp + 1) * LANES)
                a_i = al_scr[pp, rows, :]
                al = a_i[ch - 1:ch]
                hp = h_ref[i, poff + pp]
                yi = _dot_nt(cm_i, hp.astype(BF16))[:ch] * jnp.exp(a_i)
                y_ref[rows, cols] = y_ref[rows, cols] + yi
                xw = xs_ref[0, rows, cols] * (jnp.exp(al - a_i) * dl_scr[pp, rows, :])
                dec = jnp.where(even_row, jnp.exp(al[:, 0:1]), jnp.exp(al[:, LANES - 1:LANES]))
                h_ref[i, poff + pp] = dec * hp + _dot_tn(_pad_rows(xw, 2 * SUBLANES).astype(BF16), bm_i)
            return carry
        lax.fori_loop(0, nseq, short_body, 0)

    y = y_ref[:, goff:goff + GROUP_W] * (z * _sigmoid(z))
    ms = jnp.mean(y * y, axis=-1, keepdims=True)
    y_ref[:, goff:goff + GROUP_W] = y * lax.rsqrt(ms + RMS_EPS) * nw_ref[:, goff:goff + GROUP_W]


def _ssd(xc3, xw, h0_all, new_all, li, dtb, a, dsk, nw, bsz, l, valid, base_tile):
    r = ROW_TILE
    ch, nseq, nc, nb = _seq_tiling(bsz, l)
    gps = SSD_GROUPS_PER_STEP if nseq == 1 else 1
    gw, sw, pairs = gps * GROUP_W, gps * SSD_STATE, gps * PAIRS_PER_GROUP
    extra_specs, extra_args, aliases = _carry_alias(new_all, 10)
    kern = functools.partial(_ssd_kernel, gps=gps, ch=ch, nseq=nseq, valid=valid, nc=nc)
    hspec = pl.BlockSpec((1, nseq, pairs, LANES, SSD_STATE), lambda bb, g, c: (li, bb, g, 0, 0))
    b_off = SSD_INNER // sw
    c_off = b_off + SSD_GROUPS // gps

    def vec(w):
        return pl.BlockSpec((1, w), lambda bb, g, c: (0, g))

    return pl.pallas_call(
        kern,
        grid=(nb, SSD_GROUPS // gps, nc),
        in_specs=[pl.BlockSpec((1, r, gw), lambda bb, g, c: (bb * nc + c, 0, g)),
                  pl.BlockSpec((1, r, sw), lambda bb, g, c: (bb * nc + c, 0, b_off + g)),
                  pl.BlockSpec((1, r, sw), lambda bb, g, c: (bb * nc + c, 0, c_off + g)),
                  pl.BlockSpec((r, gw), lambda bb, g, c: (base_tile + bb * nc + c, OFF_Z // gw + g)),
                  pl.BlockSpec((r, sw), lambda bb, g, c: (base_tile + bb * nc + c, OFF_DT // sw + g)),
                  vec(sw), vec(sw), vec(gw), vec(gw), hspec] + extra_specs,
        out_specs=[pl.BlockSpec((r, gw), lambda bb, g, c: (bb * nc + c, g)), hspec],
        out_shape=[jax.ShapeDtypeStruct((bsz * l, SSD_INNER), F32),
                   jax.ShapeDtypeStruct(h0_all.shape, F32)],
        scratch_shapes=[pltpu.VMEM((nseq, pairs, LANES, SSD_STATE), F32),
                        pltpu.VMEM((PAIRS_PER_GROUP, r, LANES), F32), pltpu.VMEM((PAIRS_PER_GROUP, r, LANES), F32)],
        input_output_aliases=aliases,
        compiler_params=_cparams(("arbitrary", "arbitrary", "arbitrary")),
        name="ssd",
    )(xc3, xc3, xc3, xw, xw, dtb, a, dsk, nw, h0_all, *extra_args)


def _layer_norm(x, g, b):
    mu = jnp.mean(x, axis=-1, keepdims=True)
    xc = x - mu
    var = jnp.mean(xc * xc, axis=-1, keepdims=True)
    return xc * lax.rsqrt(var + LN_EPS) * g + b


def _postmix_kernel(op_ref, os_ref, yp_ref, ys_ref, ga_ref, gb_ref, x_ref, hgp_ref, ssp_ref, wo_ref, g_ref, b_ref,
                    rwh_ref, rwl_ref, rb_ref, h_ref, h3_ref, lg_ref, *, n_p):
    first = pl.program_id(0) < n_p
    o = jnp.where(first, op_ref[...], os_ref[...])
    y = jnp.where(first, yp_ref[...], ys_ref[...])
    out_a = _dot(o.astype(BF16), hgp_ref[...])
    out_b = _dot(y.astype(BF16), ssp_ref[...])
    merged = _sigmoid(ga_ref[...]) * out_a + _sigmoid(gb_ref[...]) * out_b
    mix = _dot(merged.astype(BF16), wo_ref[...])
    h = _layer_norm(ALPHA * x_ref[...] + mix, g_ref[...], b_ref[...])
    h_ref[...] = h
    _store_rows(h3_ref, h)
    hh = h.astype(BF16)
    hl = (h - hh.astype(F32)).astype(BF16)
    lg_ref[...] = (_dot(hh, rwh_ref[...]) + _dot(hl, rwh_ref[...]) + _dot(hh, rwl_ref[...])) + rb_ref[...]


def _postmix(o_p, o_s, y_p, y_s, xw, x, hgp, ssp, wo, g, b, rwh, rwl, rb):
    t = x.shape[0]
    tm = TOKEN_TILE
    n_p = o_p.shape[0] // tm

    def full(a):
        return pl.BlockSpec(a.shape, lambda i: (0, 0))

    def first(w):
        return pl.BlockSpec((tm, w), lambda i: (jnp.minimum(i, n_p - 1), 0))

    def second(w):
        return pl.BlockSpec((tm, w), lambda i: (jnp.maximum(i - n_p, 0), 0))

    return pl.pallas_call(
        functools.partial(_postmix_kernel, n_p=n_p),
        grid=(t // tm,),
        in_specs=[first(HG_DIM), second(HG_DIM), first(SSD_INNER), second(SSD_INNER),
                  pl.BlockSpec((tm, D_MODEL), lambda i: (i, OFF_GA // D_MODEL)),
                  pl.BlockSpec((tm, D_MODEL), lambda i: (i, OFF_GB // D_MODEL)),
                  pl.BlockSpec((tm, D_MODEL), lambda i: (i, 0)),
                  full(hgp), full(ssp), full(wo), full(g), full(b), full(rwh), full(rwl), full(rb)],
        out_specs=[pl.BlockSpec((tm, D_MODEL), lambda i: (i, 0)),
                   pl.BlockSpec((tm, ROW_SUB, LANES), lambda i: (i, 0, 0)),
                   pl.BlockSpec((tm, LANES), lambda i: (i, 0))],
        out_shape=[jax.ShapeDtypeStruct((t, D_MODEL), F32),
                   jax.ShapeDtypeStruct((t, ROW_SUB, LANES), F32),
                   jax.ShapeDtypeStruct((t, LANES), F32)],
        compiler_params=_cparams(("arbitrary",)),
        name="postmix",
    )(o_p, o_s, y_p, y_s, xw, xw, x, hgp, ssp, wo, g, b, rwh, rwl, rb)


def _moe_kernel(be_ref, nu_ref, tokc_ref, tokn_ref, dstp_ref, dstc_ref, h_hbm, wgu_ref, bgu_ref, wd_ref, bd_ref,
                ys_hbm, xbuf, obuf, xb_scr, act_scr, wgu_b, wd_b, gsem, ssem, *, nblk, trash_row):
    j = pl.program_id(0)
    nu = nu_ref[0]
    slot = j & 1
    nslot = 1 - slot

    def gather_start(tok_ref, s, i):
        pltpu.make_async_copy(h_hbm.at[pl.ds(tok_ref[0, 0, i], 1)], xbuf.at[s, pl.ds(i, 1)], gsem.at[s]).start()

    def scatter_start(dst_ref, s, i):
        pltpu.make_async_copy(obuf.at[s, pl.ds(i, 1)], ys_hbm.at[pl.ds(dst_ref[0, 0, i], 1)], ssem.at[s]).start()

    def gather_wait(s):
        pltpu.make_async_copy(h_hbm.at[pl.ds(0, MOE_BM)], xbuf.at[s], gsem.at[s]).wait()

    def scatter_wait(s):
        pltpu.make_async_copy(obuf.at[s], ys_hbm.at[pl.ds(0, MOE_BM)], ssem.at[s]).wait()

    @pl.when(j == 0)
    def _():
        obuf[...] = jnp.zeros(obuf.shape, F32)
        pltpu.make_async_copy(obuf.at[0], ys_hbm.at[pl.ds(trash_row, MOE_BM)], ssem.at[0]).start()

        def body(i, carry):
            gather_start(tokc_ref, 0, i)
            return carry
        lax.fori_loop(0, MOE_BM, body, 0, unroll=8)

    first_of_expert = (j == 0) | (be_ref[j] != be_ref[jnp.maximum(j - 1, 0)])

    @pl.when((j < nu) & first_of_expert)
    def _():
        rows = D_MODEL // 8
        for c in range(8):
            wgu_b[c * rows:(c + 1) * rows] = wgu_ref[0, 0, c * rows:(c + 1) * rows].astype(BF16)
            wd_b[c * rows:(c + 1) * rows] = wd_ref[0, 0, c * rows:(c + 1) * rows].astype(BF16)

    @pl.when(j < nu)
    def _():
        gather_wait(slot)
        scatter_wait(slot)
        xb_scr[...] = _load_rows(xbuf.at[slot]).astype(BF16)
        for i in range(MOE_BM):
            gather_start(tokn_ref, nslot, i)
            scatter_start(dstp_ref, nslot, i)
        gu = _dot(xb_scr[...], wgu_b[...]) + bgu_ref[0, 0]
        gate = jnp.minimum(gu[:, :D_FF], SWIGLU_LIMIT)
        up = jnp.clip(gu[:, D_FF:], -SWIGLU_LIMIT, SWIGLU_LIMIT)
        act_scr[...] = ((up + 1.0) * gate * _sigmoid(SWIGLU_ALPHA * gate)).astype(BF16)

    def down(_, carry):
        _store_rows(obuf.at[slot], _dot(act_scr[...], wd_b[...]) + bd_ref[0, 0])
        return carry
    lax.fori_loop(0, (j < nu).astype(jnp.int32), down, 0)

    @pl.when(j == nu - 1)
    def _():
        def body(i, carry):
            scatter_start(dstc_ref, slot, i)
            return carry
        lax.fori_loop(0, MOE_BM, body, 0, unroll=8)

    @pl.when(j == nblk - 1)
    def _():
        scatter_wait((nu - 1) & 1)
        scatter_wait(nu & 1)
        gather_wait(nu & 1)


def _moe_experts(block_e, n_used, row_tok, row_dst, row_dst_prev, h, wgu, bgu, wd, bd, li):
    t = h.shape[0]
    nblk = row_tok.shape[0]
    trash_row = TOP_K * t
    kern = functools.partial(_moe_kernel, nblk=nblk, trash_row=trash_row)

    def smem(imap):
        return pl.BlockSpec((1, 1, MOE_BM), imap, memory_space=pltpu.SMEM)

    grid_spec = pltpu.PrefetchScalarGridSpec(
        num_scalar_prefetch=2,
        grid=(nblk,),
        in_specs=[smem(lambda j, be, nu: (j, 0, 0)),
                  smem(lambda j, be, nu: (jnp.minimum(j + 1, nblk - 1), 0, 0)),
                  smem(lambda j, be, nu: (j, 0, 0)),
                  smem(lambda j, be, nu: (j, 0, 0)),
                  pl.BlockSpec(memory_space=pl.ANY),
                  pl.BlockSpec((1, 1, D_MODEL, 2 * D_FF), lambda j, be, nu: (li, be[j], 0, 0)),
                  pl.BlockSpec((1, 1, 1, 2 * D_FF), lambda j, be, nu: (li, be[j], 0, 0)),
                  pl.BlockSpec((1, 1, D_FF, D_MODEL), lambda j, be, nu: (li, be[j], 0, 0)),
                  pl.BlockSpec((1, 1, 1, D_MODEL), lambda j, be, nu: (li, be[j], 0, 0))],
        out_specs=pl.BlockSpec(memory_space=pl.ANY),
        scratch_shapes=[pltpu.VMEM((2, MOE_BM, ROW_SUB, LANES), F32), pltpu.VMEM((2, MOE_BM, ROW_SUB, LANES), F32),
                        pltpu.VMEM((MOE_BM, D_MODEL), BF16), pltpu.VMEM((MOE_BM, D_FF), BF16),
                        pltpu.VMEM((D_MODEL, 2 * D_FF), BF16), pltpu.VMEM((D_FF, D_MODEL), BF16),
                        pltpu.SemaphoreType.DMA((2,)), pltpu.SemaphoreType.DMA((2,))],
    )
    return pl.pallas_call(
        kern,
        grid_spec=grid_spec,
        out_shape=jax.ShapeDtypeStruct((trash_row + 2 * MOE_BM, ROW_SUB, LANES), F32),
        compiler_params=pltpu.CompilerParams(dimension_semantics=("arbitrary",), vmem_limit_bytes=MOE_VMEM_LIMIT),
        name="moe_experts",
    )(block_e, n_used, row_tok, row_tok, row_dst_prev, row_dst, h, wgu, bgu, wd, bd)


def _combine_kernel(h_ref, y0_ref, y1_ref, y2_ref, y3_ref, gt_ref, g_ref, b_ref, x_ref, xb_ref):
    gt = gt_ref[...]
    y = gt[:, 0:1] * _load_rows(y0_ref)
    for kk, y_ref in enumerate((y1_ref, y2_ref, y3_ref), start=1):
        y = y + gt[:, kk:kk + 1] * _load_rows(y_ref)
    x = _layer_norm(ALPHA * h_ref[...] + y, g_ref[...], b_ref[...])
    x_ref[...] = x
    xb_ref[...] = x.astype(BF16)


def _combine(h, ys, gates, g, b, tm):
    t = h.shape[0]
    nt = t // tm

    def yspec(kk):
        return pl.BlockSpec((tm, ROW_SUB, LANES), lambda i: (kk * nt + i, 0, 0))

    return pl.pallas_call(
        _combine_kernel,
        grid=(nt,),
        in_specs=[pl.BlockSpec((tm, D_MODEL), lambda i: (i, 0)),
                  yspec(0), yspec(1), yspec(2), yspec(3),
                  pl.BlockSpec((tm, TOP_K), lambda i: (i, 0)),
                  pl.BlockSpec((1, D_MODEL), lambda i: (0, 0)),
                  pl.BlockSpec((1, D_MODEL), lambda i: (0, 0))],
        out_specs=[pl.BlockSpec((tm, D_MODEL), lambda i: (i, 0)),
                   pl.BlockSpec((tm, D_MODEL), lambda i: (i, 0))],
        out_shape=[jax.ShapeDtypeStruct((t, D_MODEL), F32),
                   jax.ShapeDtypeStruct((t, D_MODEL), BF16)],
        compiler_params=_cparams(("arbitrary",)),
        name="combine_ln",
    )(h, ys, ys, ys, ys, gates, g, b)


def _split_cols(a, sizes):
    out, off = [], 0
    for s in sizes:
        out.append(a[..., off:off + s])
        off += s
    return out


def _per_group(vec_heads):
    v = vec_heads.reshape(SSD_GROUPS, SSD_HPG)
    return jnp.pad(v, ((0, 0), (0, LANES - SSD_HPG))).reshape(1, DT_W)


def _prep_layer(p, l):
    wq, wf, wv, wg, wz, wxbc, wdt, wga, wgb = _split_cols(p["w_in"][l], IN_SPLITS)
    wdt = jnp.pad(wdt.reshape(D_MODEL, SSD_GROUPS, SSD_HPG), ((0, 0), (0, 0), (0, LANES - SSD_HPG)))
    w_in = jnp.concatenate([wq, wf, wv, wg, wz, wxbc, wga, wgb, wdt.reshape(D_MODEL, DT_W)], axis=1).astype(BF16)
    lb = p["lb_all"][l].reshape(HG_HEADS, 1, HG_DK)
    rw = jnp.pad(p["router_w"][l], ((0, 0), (0, LANES - N_EXPERTS)))
    rwh = rw.astype(BF16)
    rwl = (rw - rwh.astype(F32)).astype(BF16)
    return dict(
        w_in=w_in,
        la=jnp.log(lb), l1m=jnp.log1p(-lb), oml=1.0 - lb,
        hg_nw=p["hg_norm_w"][l].reshape(HG_HEADS, 1, HG_DV),
        hgp=p["hg_proj"][l].astype(BF16),
        conv_w=p["conv_w"][l], conv_b=p["conv_b"][l].reshape(1, CONV_DIM),
        dtb=_per_group(p["dt_bias"][l]),
        a=_per_group(-jnp.exp(p["a_log"][l].astype(F32))),
        dsk=jnp.repeat(p["d_skip"][l], SSD_HEADDIM).reshape(1, SSD_INNER),
        ssd_nw=p["ssd_norm_w"][l].reshape(1, SSD_INNER),
        ssp=p["ssd_proj"][l].astype(BF16),
        wo=p["w_out"][l].astype(BF16),
        ln1_g=p["ln1_g"][l].reshape(1, D_MODEL), ln1_b=p["ln1_b"][l].reshape(1, D_MODEL),
        rwh=rwh, rwl=rwl,
        rb=jnp.pad(p["router_b"][l], (0, LANES - N_EXPERTS)).reshape(1, LANES),
        ln2_g=p["ln2_g"][l].reshape(1, D_MODEL), ln2_b=p["ln2_b"][l].reshape(1, D_MODEL),
    )


def _prep_experts(p):
    return dict(
        wgu=p["w_gu"], bgu=p["b_gu"].reshape(DEPTH, N_EXPERTS, 1, 2 * D_FF),
        wd=p["w_down"], bd=p["b_down"].reshape(DEPTH, N_EXPERTS, 1, D_MODEL),
    )


def _moe(h, h3, logits, lp, ep, li, tm):
    t = h.shape[0]
    s = t * TOP_K
    top_v, top_e = lax.top_k(logits[:, :N_EXPERTS], TOP_K)
    gates = jax.nn.softmax(top_v, axis=-1)
    e_slot = top_e.reshape(s).astype(jnp.int32)
    slot_bits = max(s - 1, 1).bit_length()
    order = jnp.sort((e_slot << slot_bits) | jnp.arange(s, dtype=jnp.int32)) & ((1 << slot_bits) - 1)
    counts = jnp.sum((e_slot[:, None] == jnp.arange(N_EXPERTS, dtype=jnp.int32)[None, :]).astype(jnp.int32), axis=0)
    padded = (counts + MOE_BM - 1) // MOE_BM * MOE_BM
    pend = jnp.cumsum(padded)
    cend = pend - padded + counts
    n_blocks = (s + N_EXPERTS * (MOE_BM - 1) + MOE_BM - 1) // MOE_BM
    nrows = n_blocks * MOE_BM
    rows = jnp.arange(nrows, dtype=jnp.int32)[:, None]
    before = pend[None, :] <= rows
    pad_before = jnp.sum(jnp.where(before, padded - counts, 0), axis=1)
    is_pad = jnp.any((cend[None, :] <= rows) & (rows < pend[None, :]), axis=1) | (rows[:, 0] >= pend[-1])
    slot = order[jnp.clip(rows[:, 0] - pad_before, 0, s - 1)]
    tok = slot // TOP_K
    row_tok = jnp.where(is_pad, 0, tok)
    row_dst = jnp.where(is_pad, TOP_K * t + rows[:, 0] % (2 * MOE_BM), (slot - tok * TOP_K) * t + tok)
    blk_rows = jnp.arange(n_blocks, dtype=jnp.int32)[:, None] * MOE_BM
    block_e = jnp.minimum(jnp.sum((pend[None, :] <= blk_rows).astype(jnp.int32), axis=1), N_EXPERTS - 1)
    n_used = (pend[-1:] // MOE_BM).astype(jnp.int32)
    first_prev = TOP_K * t + MOE_BM + jnp.arange(MOE_BM, dtype=jnp.int32)
    row_dst_prev = jnp.concatenate([first_prev, row_dst[:-MOE_BM]])
    ys = _moe_experts(block_e, n_used, row_tok.reshape(n_blocks, 1, MOE_BM), row_dst.reshape(n_blocks, 1, MOE_BM),
                      row_dst_prev.reshape(n_blocks, 1, MOE_BM), h3, ep["wgu"], ep["bgu"], ep["wd"], ep["bd"], li)
    return _combine(h, ys, gates, lp["ln2_g"], lp["ln2_b"], tm)


def _mixers(xw, li, lp, grp, new):
    bsz, l, valid, base = grp["bsz"], grp["l"], grp["valid"], grp["base"]
    t = bsz * l
    prev8 = jnp.pad(grp["s_conv"][li], ((0, 0), (SUBLANES - (CONV_K - 1), 0), (0, 0)))
    xc = _conv(xw, prev8, lp["conv_w"], lp["conv_b"], bsz, l, base)
    tail = base + jnp.arange(bsz, dtype=jnp.int32)[:, None] * l + jnp.arange(valid - (CONV_K - 1), valid)[None, :]
    conv_new = xw[tail.reshape(-1)][:, OFF_XBC:OFF_XBC + CONV_DIM].reshape(bsz, CONV_K - 1, CONV_DIM)
    o, hg = _hgrn(xw, grp["s_hg"], new["hg"], li, lp["la"], lp["l1m"], lp["oml"], lp["hg_nw"], bsz, l, valid,
                  base // ROW_TILE)
    y, ssm = _ssd(xc.reshape(t // ROW_TILE, ROW_TILE, CONV_DIM), xw, grp["s_ssm"], new["ssm"], li, lp["dtb"],
                  lp["a"], lp["dsk"], lp["ssd_nw"], bsz, l, valid, base // ROW_TILE)
    return o, y, dict(hg=hg, ssm=ssm, conv=new["conv"] + [conv_new])


def _forward(x, groups, layers, ep):
    t = x.shape[0]
    xb = x.astype(BF16)
    new = [dict(hg=jnp.zeros(g["s_hg"].shape, F32), ssm=jnp.zeros(g["s_ssm"].shape, F32), conv=[]) for g in groups]
    for li, lp in enumerate(layers):
        xw = _matmul(xb, lp["w_in"], 512 if t % 512 == 0 else TOKEN_TILE, N_W // 4)
        mix = [_mixers(xw, li, lp, grp, n) for grp, n in zip(groups, new)]
        new = [m[2] for m in mix]
        h, h3, logits = _postmix(mix[0][0], mix[1][0], mix[0][1], mix[1][1], xw, x, lp["hgp"], lp["ssp"], lp["wo"],
                             lp["ln1_g"], lp["ln1_b"], lp["rwh"], lp["rwl"], lp["rb"])
        x, xb = _moe(h, h3, logits, lp, ep, li, TOKEN_TILE)
    return x, [dict(hg=n["hg"], ssm=n["ssm"], conv=jnp.stack(n["conv"])) for n in new]


def kernel(x_prompt, x_sample, state_hgrn, state_ssm, state_conv, hg_lower_bounds, w_in, hg_norm_w, hg_proj,
           conv_w, conv_b, dt_bias, a_log, d_skip, ssd_norm_w, ssd_proj, w_out, ln1_g, ln1_b, router_w,
           router_b, w_gu, b_gu, w_down, b_down, ln2_g, ln2_b):
    lb_all = jnp.cumsum(jax.nn.softmax(hg_lower_bounds.astype(F32), axis=0), axis=0)
    lb_all = lb_all - lb_all[0]
    p = dict(lb_all=lb_all, w_in=w_in, hg_norm_w=hg_norm_w, hg_proj=hg_proj, conv_w=conv_w, conv_b=conv_b,
             dt_bias=dt_bias, a_log=a_log, d_skip=d_skip, ssd_norm_w=ssd_norm_w, ssd_proj=ssd_proj, w_out=w_out,
             ln1_g=ln1_g, ln1_b=ln1_b, router_w=router_w, router_b=router_b, w_gu=w_gu, b_gu=b_gu,
             w_down=w_down, b_down=b_down, ln2_g=ln2_g, ln2_b=ln2_b)
    layers = [_prep_layer(p, l) for l in range(DEPTH)]
    ep = _prep_experts(p)

    return _run(x_prompt, x_sample, state_hgrn, state_ssm, state_conv, layers, ep)


def _run(x_prompt, x_sample, state_hgrn, state_ssm, state_conv, layers, ep):
    bp, lprompt, _ = x_prompt.shape
    bs, ls, _ = x_sample.shape
    tp = bp * lprompt

    def pairs(s_ssm):
        return s_ssm.reshape(s_ssm.shape[:2] + (SSD_PAIRS, LANES, SSD_STATE))

    groups = [
        dict(bsz=bp, l=lprompt, valid=lprompt, base=0,
             s_hg=jnp.zeros((DEPTH, bp) + state_hgrn.shape[2:], F32),
             s_ssm=pairs(jnp.zeros((DEPTH, bp) + state_ssm.shape[2:], F32)),
             s_conv=jnp.zeros((DEPTH, bp) + state_conv.shape[2:], F32)),
        dict(bsz=bs, l=SUBLANES, valid=ls, base=tp, s_hg=state_hgrn, s_ssm=pairs(state_ssm), s_conv=state_conv),
    ]
    xs = jnp.pad(x_sample, ((0, 0), (0, SUBLANES - ls), (0, 0)))
    x = jnp.concatenate([x_prompt.reshape(tp, D_MODEL), xs.reshape(bs * SUBLANES, D_MODEL)], axis=0)
    x, new = _forward(x, groups, layers, ep)
    y_p = x[:tp].reshape(bp, lprompt, D_MODEL)
    y_s = x[tp:].reshape(bs, SUBLANES, D_MODEL)[:, :ls]
    (n_p, n_s) = new
    return (y_p, y_s, n_p["hg"], n_p["ssm"].reshape((DEPTH, bp) + state_ssm.shape[2:]), n_p["conv"],
            n_s["hg"], n_s["ssm"].reshape(state_ssm.shape), n_s["conv"])
```

```python
import functools

import jax
import jax.numpy as jnp
from jax import lax
from jax.experimental import pallas as pl
from jax.experimental.pallas import tpu as pltpu

F32 = jnp.float32
BF16 = jnp.bfloat16

D_MODEL = 1024
DEPTH = 2
HG_HEADS = 8
HG_DK = 128
HG_DV = 128
HG_DIM = HG_HEADS * HG_DK
SSD_INNER = 2 * D_MODEL
SSD_HEADDIM = 64
SSD_HEADS = SSD_INNER // SSD_HEADDIM
SSD_GROUPS = 4
SSD_HPG = SSD_HEADS // SSD_GROUPS
SSD_STATE = 128
SSD_PAIRS = SSD_HEADS // 2
PAIRS_PER_GROUP = SSD_PAIRS // SSD_GROUPS
GROUP_W = SSD_INNER // SSD_GROUPS
CONV_K = 4
CONV_DIM = SSD_INNER + 2 * SSD_GROUPS * SSD_STATE
N_EXPERTS = 32
TOP_K = 4
D_FF = D_MODEL
SWIGLU_LIMIT = 7.0
SWIGLU_ALPHA = 1.702
ALPHA = (2.0 * DEPTH) ** 0.25
LN_EPS = 1e-5
RMS_EPS = 1e-6

LANES = 128
SUBLANES = 8
ROW_SUB = D_MODEL // LANES

DT_W = SSD_GROUPS * LANES
OFF_Q = 0
OFF_F = HG_DIM
OFF_V = 2 * HG_DIM
OFF_G = 3 * HG_DIM
OFF_Z = 4 * HG_DIM
OFF_XBC = OFF_Z + SSD_INNER
OFF_GA = OFF_XBC + CONV_DIM
OFF_GB = OFF_GA + D_MODEL
OFF_DT = OFF_GB + D_MODEL
N_W = OFF_DT + DT_W
IN_SPLITS = (HG_DIM, HG_DIM, HG_DIM, HG_DIM, SSD_INNER, CONV_DIM, SSD_HEADS, D_MODEL, D_MODEL)

ROW_TILE = 128
CONV_ROWS = 256
TOKEN_TILE = 256
MOE_BM = 256
MOE_SCATTER_SPLIT = 128
VMEM_LIMIT = 48 * 1024 * 1024
MOE_VMEM_LIMIT = 56 * 1024 * 1024
LOG2E = 1.4426950408889634
HALO = 16
HGRN_HEADS_PER_STEP = 8
HGRN_SUB = 8
SSD_GROUPS_PER_STEP = 4


def _cparams(sem):
    return pltpu.CompilerParams(dimension_semantics=sem, vmem_limit_bytes=VMEM_LIMIT)


def _sigmoid(x):
    return 0.5 * jnp.tanh(0.5 * x) + 0.5


def _log1p(x):
    return jnp.log(1.0 + x)


def _dot(a, b):
    return jnp.dot(a, b, preferred_element_type=F32)


def _dot_nt(a, b):
    return lax.dot_general(a, b, (((1,), (1,)), ((), ())), preferred_element_type=F32)


def _dot_tn(a, b):
    return lax.dot_general(a, b, (((0,), (0,)), ((), ())), preferred_element_type=F32)


def _load_rows(ref3):
    return pltpu.einshape("tsl->t(sl)", ref3[...])


def _store_rows(ref3, x):
    ref3[...] = pltpu.einshape("t(sl)->tsl", x, s=ROW_SUB)


def _mm_kernel(x_ref, w_ref, o_ref):
    o_ref[...] = _dot(x_ref[...], w_ref[...])


def _matmul(x, w, tm, tn):
    m, k = x.shape
    n = w.shape[1]
    return pl.pallas_call(
        _mm_kernel,
        grid=(n // tn, m // tm),
        in_specs=[pl.BlockSpec((tm, k), lambda j, i: (i, 0)),
                  pl.BlockSpec((k, tn), lambda j, i: (0, j))],
        out_specs=pl.BlockSpec((tm, tn), lambda j, i: (i, j)),
        out_shape=jax.ShapeDtypeStruct((m, n), F32),
        compiler_params=_cparams(("arbitrary", "arbitrary")),
        name="in_proj",
    )(x, w)


def _conv_kernel(x_ref, pb_ref, p0_ref, w_ref, b_ref, o_ref):
    i = pl.program_id(1)
    x = x_ref[...]
    prev = jnp.where(i == 0, p0_ref[0], pb_ref[...])
    full = jnp.concatenate([prev, x], axis=0)
    w = w_ref[...]
    acc = b_ref[...] + w[CONV_K - 1:CONV_K] * x
    for s in range(1, CONV_K):
        xs = pltpu.roll(full, s, 0)[SUBLANES:]
        acc = acc + w[CONV_K - 1 - s:CONV_K - s] * xs
    o_ref[...] = acc * _sigmoid(acc)


def _conv_short_kernel(x_ref, p_ref, w_ref, b_ref, o_ref):
    x = x_ref[...]
    p = p_ref[...]
    rc = lax.broadcasted_iota(jnp.int32, x.shape, 0) & (SUBLANES - 1)
    w = w_ref[...]
    acc = b_ref[...] + w[CONV_K - 1:CONV_K] * x
    for s in range(1, CONV_K):
        xs = jnp.where(rc >= s, pltpu.roll(x, s, 0), pltpu.roll(p, (s - SUBLANES) % ROW_TILE, 0))
        acc = acc + w[CONV_K - 1 - s:CONV_K - s] * xs
    o_ref[...] = acc * _sigmoid(acc)


def _conv(xw, prev8, conv_w, conv_b, bsz, l, base_rows):
    cb = OFF_XBC // CONV_DIM
    wspec = [pl.BlockSpec((CONV_K, CONV_DIM), lambda *_: (0, 0)), pl.BlockSpec((1, CONV_DIM), lambda *_: (0, 0))]
    out_shape = jax.ShapeDtypeStruct((bsz * l, CONV_DIM), F32)
    if l == SUBLANES:
        r = ROW_TILE
        return pl.pallas_call(
            _conv_short_kernel,
            grid=(bsz * l // r,),
            in_specs=[pl.BlockSpec((r, CONV_DIM), lambda i: (base_rows // r + i, cb)),
                      pl.BlockSpec((r, CONV_DIM), lambda i: (i, 0))] + wspec,
            out_specs=pl.BlockSpec((r, CONV_DIM), lambda i: (i, 0)),
            out_shape=out_shape,
            compiler_params=_cparams(("arbitrary",)),
            name="conv_silu_short",
        )(xw, prev8.reshape(bsz * SUBLANES, CONV_DIM), conv_w, conv_b)
    r = CONV_ROWS
    tps = l // r
    rb = r // SUBLANES
    return pl.pallas_call(
        _conv_kernel,
        grid=(bsz, tps),
        in_specs=[pl.BlockSpec((r, CONV_DIM), lambda bi, i: (base_rows // r + bi * tps + i, cb)),
                  pl.BlockSpec((SUBLANES, CONV_DIM),
                               lambda bi, i: (jnp.maximum(base_rows // SUBLANES + (bi * tps + i) * rb - 1, 0), cb)),
                  pl.BlockSpec((1, SUBLANES, CONV_DIM), lambda bi, i: (bi, 0, 0))] + wspec,
        out_specs=pl.BlockSpec((r, CONV_DIM), lambda bi, i: (bi * tps + i, 0)),
        out_shape=out_shape,
        compiler_params=_cparams(("arbitrary", "arbitrary")),
        name="conv_silu",
    )(xw, xw, prev8, conv_w, conv_b)


def _seg_cumsum(x, rc, seg):
    s = 1
    while s < seg:
        x = x + jnp.where(rc >= s, pltpu.roll(x, s, 0), 0.0)
        s *= 2
    return x


def _pad_rows(x, rows):
    if x.shape[0] == rows:
        return x
    return jnp.concatenate([x, jnp.zeros((rows - x.shape[0], x.shape[1]), x.dtype)], axis=0)


def _split3(x):
    hi = x.astype(BF16)
    r1 = x - hi.astype(F32)
    mid = r1.astype(BF16)
    lo = (r1 - mid.astype(F32)).astype(BF16)
    return hi, mid, lo


def _hgrn_kernel(*refs, hps, nseq, nc, **kw):
    refs = refs[:9] + refs[10:]
    s0_ref, sout_ref, st_ref = refs[8], refs[10], refs[11]
    c = pl.program_id(2)

    @pl.when(c == 0)
    def _():
        for hh in range(hps):
            for i in range(nseq):
                st_ref[hh, i] = s0_ref[0, i, hh].T

    for hh in range(hps):
        _hgrn_head(hh, *refs, nseq=nseq, **kw)

    @pl.when(c == nc - 1)
    def _():
        for hh in range(hps):
            for i in range(nseq):
                sout_ref[0, i, hh] = st_ref[hh, i].T


def _hgrn_head(hh, q_ref, f_ref, v_ref, g_ref, la_ref, l1m_ref, oml_ref, nw_ref, s0_ref,
               o_ref, sout_ref, st_ref, k_scr, b_scr, *, ch, sub, nseq, valid):
    r = ROW_TILE
    cols = slice(hh * LANES, (hh + 1) * LANES)
    q = q_ref[:, cols]
    fr = f_ref[:, cols]
    v = v_ref[:, cols]
    q = q * _sigmoid(q)
    la = la_ref[hh]
    lsig = jnp.minimum(fr, 0.0) - _log1p(jnp.exp(-jnp.abs(fr)))
    cc = l1m_ref[hh] + lsig
    logf = jnp.maximum(la, cc) + _log1p(jnp.exp(-jnp.abs(la - cc)))
    k = oml_ref[hh] * _sigmoid(-fr)

    row = lax.broadcasted_iota(jnp.int32, (r, LANES), 0)
    lane = lax.broadcasted_iota(jnp.int32, (r, LANES), 1)
    if valid < ch:
        pad = (row & (ch - 1)) >= valid
        logf = jnp.where(pad, 0.0, logf)
        k = jnp.where(pad, 0.0, k)

    tri = lane <= row
    if ch < r:
        shift = ch.bit_length() - 1
        tri = tri & ((row >> shift) == (lane >> shift))
    tri = jnp.where(tri, 1.0, 0.0).astype(BF16)
    hi, mid, lo = _split3(logf)
    b2 = (_dot(tri, hi) + _dot(tri, mid) + _dot(tri, lo)) * LOG2E

    k_scr[hh, 0:HALO] = jnp.zeros((HALO, LANES), F32)
    b_scr[hh, 0:HALO] = jnp.zeros((HALO, LANES), F32)
    k_scr[hh, HALO:HALO + r] = k
    b_scr[hh, HALO:HALO + r] = b2
    rs = row & (sub - 1)
    scl = jnp.where(lane == rs, jnp.sum(q * k, axis=-1, keepdims=True), 0.0)
    for d in range(1, sub):
        kd = k_scr[hh, pl.ds(HALO - d, r), :]
        bd = b_scr[hh, pl.ds(HALO - d, r), :]
        sd = jnp.sum(q * kd * jnp.exp2(b2 - bd), axis=-1, keepdims=True)
        scl = jnp.where(lane == rs - d, sd, scl)

    spc = ch // sub
    pieces = []
    for m in range(r // sub):
        lo_, hi_ = m * sub, (m + 1) * sub
        piece = scl[lo_:hi_]
        if m:
            piece = pltpu.roll(piece, lo_, 1)
        j = m % spc
        if j:
            cs = (m - j) * sub
            ref_b = b2[lo_ - 1:lo_]
            qj = q[lo_:hi_] * jnp.exp2(b2[lo_:hi_] - ref_b)
            kk = k[cs:lo_] * jnp.exp2(ref_b - b2[cs:lo_])
            parts = ([jnp.zeros((cs, LANES), F32)] if cs else []) + [kk, jnp.zeros((r - lo_, LANES), F32)]
            piece = piece + _dot_nt(qj.astype(BF16), jnp.concatenate(parts, axis=0).astype(BF16))
        pieces.append(piece)
    sc = jnp.concatenate(pieces, axis=0)
    o = _dot(sc.astype(BF16), v.astype(BF16))

    inter = []
    for i in range(nseq):
        lo_, hi_ = i * ch, (i + 1) * ch
        bi = b2[lo_:hi_]
        bl = bi[ch - 1:ch]
        st = st_ref[hh, i]
        inter.append(_dot_nt((q[lo_:hi_] * jnp.exp2(bi)).astype(BF16), st.astype(BF16)))
        kh = _pad_rows(k[lo_:hi_] * jnp.exp2(bl - bi), r)
        vi = _pad_rows(v[lo_:hi_], r)
        st_ref[hh, i] = st * jnp.exp2(bl) + _dot_tn(vi.astype(BF16), kh.astype(BF16))
    o = o + (inter[0] if nseq == 1 else jnp.concatenate(inter, axis=0))

    ms = jnp.mean(o * o, axis=-1, keepdims=True)
    o_ref[:, cols] = o * lax.rsqrt(ms + RMS_EPS) * nw_ref[hh] * _sigmoid(g_ref[:, cols])


def _seq_tiling(bsz, l):
    r = ROW_TILE
    ch = min(l, r)
    nseq = r // ch
    nc = max(l // r, 1)
    nb = bsz * l // (r * nc)
    return ch, nseq, nc, nb


def _carry_alias(new_all, n_in):
    return [pl.BlockSpec(memory_space=pl.ANY)], [new_all], {n_in: 1}


def _hgrn(xw, s0_all, new_all, li, la, l1m, oml, nw, bsz, l, valid, base_tile):
    r = ROW_TILE
    ch, nseq, nc, nb = _seq_tiling(bsz, l)
    sub = min(ch, HGRN_SUB)
    hps = HGRN_HEADS_PER_STEP
    w = hps * LANES
    extra_specs, extra_args, aliases = _carry_alias(new_all, 9)
    kern = functools.partial(_hgrn_kernel, hps=hps, ch=ch, sub=sub, nseq=nseq, valid=valid, nc=nc)

    def xspec(off):
        return pl.BlockSpec((r, w), lambda bb, h, c: (base_tile + bb * nc + c, off // w + h))

    def pspec():
        return pl.BlockSpec((hps, 1, LANES), lambda bb, h, c: (h, 0, 0))

    return pl.pallas_call(
        kern,
        grid=(nb, HG_HEADS // hps, nc),
        in_specs=[xspec(OFF_Q), xspec(OFF_F), xspec(OFF_V), xspec(OFF_G),
                  pspec(), pspec(), pspec(), pspec(),
                  pl.BlockSpec((1, nseq, hps, HG_DK, HG_DV), lambda bb, h, c: (li, bb, h, 0, 0))] + extra_specs,
        out_specs=[pl.BlockSpec((r, w), lambda bb, h, c: (bb * nc + c, h)),
                   pl.BlockSpec((1, nseq, hps, HG_DK, HG_DV), lambda bb, h, c: (li, bb, h, 0, 0))],
        out_shape=[jax.ShapeDtypeStruct((bsz * l, HG_DIM), F32),
                   jax.ShapeDtypeStruct(s0_all.shape, F32)],
        scratch_shapes=[pltpu.VMEM((hps, nseq, HG_DV, HG_DK), F32),
                        pltpu.VMEM((hps, HALO + r, LANES), F32), pltpu.VMEM((hps, HALO + r, LANES), F32)],
        input_output_aliases=aliases,
        compiler_params=_cparams(("arbitrary", "arbitrary", "arbitrary")),
        name="hgrn2",
    )(xw, xw, xw, xw, la, l1m, oml, nw, s0_all, *extra_args)


def _ssd_kernel(*refs, **kw):
    _ssd_body(*(refs[:10] + refs[11:]), **kw)


def _ssd_body(xs_ref, bm_ref, cm_ref, z_ref, dt_ref, dtb_ref, a_ref, dsk_ref, nw_ref, h0_ref,
              y_ref, hout_ref, h_ref, al_scr, dl_scr, *, gps, nc, **kw):
    c = pl.program_id(2)

    @pl.when(c == 0)
    def _():
        h_ref[...] = h0_ref[0]

    for gg in range(gps):
        _ssd_group(gg, xs_ref, bm_ref, cm_ref, z_ref, dt_ref, dtb_ref, a_ref, dsk_ref, nw_ref, y_ref, h_ref,
                   al_scr, dl_scr, **kw)

    @pl.when(c == nc - 1)
    def _():
        hout_ref[0] = h_ref[...]


def _ssd_group(gg, xs_ref, bm_ref, cm_ref, z_ref, dt_ref, dtb_ref, a_ref, dsk_ref, nw_ref, y_ref, h_ref,
               al_scr, dl_scr, *, ch, nseq, valid):
    r = ROW_TILE
    scol = slice(gg * LANES, (gg + 1) * LANES)
    goff = gg * GROUP_W
    poff = gg * PAIRS_PER_GROUP
    shift = ch.bit_length() - 1
    rowl = lax.broadcasted_iota(jnp.int32, (r, LANES), 0)
    lane = lax.broadcasted_iota(jnp.int32, (r, LANES), 1)
    rc = rowl & (ch - 1)
    dtr = dt_ref[:, scol] + dtb_ref[:, scol]
    dt = jnp.maximum(dtr, 0.0) + _log1p(jnp.exp(-jnp.abs(dtr)))
    if valid < ch:
        dt = jnp.where(rc < valid, dt, 0.0)
    mask = (lane <= rowl) & ((rowl >> shift) == (lane >> shift))
    tri = jnp.where(mask, 1.0, 0.0).astype(BF16)
    hi, mid, lo = _split3(dt * a_ref[:, scol])
    acum = _dot(tri, hi) + _dot(tri, mid) + _dot(tri, lo)
    acum_t = acum.T
    dt_t = dt.T

    even_lane = lane < SSD_HEADDIM
    even_row = rowl < SSD_HEADDIM
    last_row = rc == ch - 1

    bmb = bm_ref[0, :, scol].astype(BF16)
    cmb = cm_ref[0, :, scol].astype(BF16)
    cb = _dot_nt(cmb, bmb)
    xs = xs_ref[0, :, goff:goff + GROUP_W]
    z = z_ref[:, goff:goff + GROUP_W]
    for pp in range(PAIRS_PER_GROUP):
        xp = xs[:, pp * LANES:(pp + 1) * LANES]
        acc = dsk_ref[:, goff + pp * LANES:goff + (pp + 1) * LANES] * xp
        for e in range(2):
            hd = 2 * pp + e
            w = cb * jnp.exp(jnp.where(mask, acum[:, hd:hd + 1] - acum_t[hd:hd + 1, :], -jnp.inf)) * dt_t[hd:hd + 1, :]
            xm = jnp.where(even_lane if e == 0 else jnp.logical_not(even_lane), xp, 0.0)
            acc = acc + _dot(w.astype(BF16), xm.astype(BF16))
        a_lane = jnp.where(even_lane, acum[:, 2 * pp:2 * pp + 1], acum[:, 2 * pp + 1:2 * pp + 2])
        dt_lane = jnp.where(even_lane, dt[:, 2 * pp:2 * pp + 1], dt[:, 2 * pp + 1:2 * pp + 2])
        ea = jnp.exp(a_lane)

        def seq_body(i, acc, pp=pp, xp=xp, a_lane=a_lane, dt_lane=dt_lane, ea=ea):
            in_seq = (rowl >> shift) == i
            al_lane = jnp.sum(jnp.where(in_seq & last_row, a_lane, 0.0), axis=0, keepdims=True)
            hp = h_ref[i, poff + pp]
            yi = _dot_nt(cmb, hp.astype(BF16)) * ea
            acc = acc + jnp.where(in_seq, yi, 0.0)
            xw = jnp.where(in_seq, xp * (jnp.exp(al_lane - a_lane) * dt_lane), 0.0)
            dec = jnp.where(even_row, jnp.exp(al_lane[:, 0:1]), jnp.exp(al_lane[:, LANES - 1:LANES]))
            h_ref[i, poff + pp] = dec * hp + _dot_tn(xw.astype(BF16), bmb)
            return acc

        if nseq == 1:
            acc = seq_body(0, acc)
        else:
            al_scr[pp] = a_lane
            dl_scr[pp] = dt_lane
        y_ref[:, goff + pp * LANES:goff + (pp + 1) * LANES] = acc

    if nseq > 1:
        def short_body(i, carry):
            rows = pl.ds(pl.multiple_of(i * ch, ch), ch)
            cm_i = _pad_rows(cm_ref[0, rows, scol], 2 * SUBLANES).astype(BF16)
            bm_i = _pad_rows(bm_ref[0, rows, scol], 2 * SUBLANES).astype(BF16)
            for pp in range(PAIRS_PER_GROUP):
                cols = slice(goff + pp * LANES, goff + (pp + 1) * LANES)
                a_i = al_scr[pp, rows, :]
                al = a_i[ch - 1:ch]
                hp = h_ref[i, poff + pp]
                yi = _dot_nt(cm_i, hp.astype(BF16))[:ch] * jnp.exp(a_i)
                y_ref[rows, cols] = y_ref[rows, cols] + yi
                xw = xs_ref[0, rows, cols] * (jnp.exp(al - a_i) * dl_scr[pp, rows, :])
                dec = jnp.where(even_row, jnp.exp(al[:, 0:1]), jnp.exp(al[:, LANES - 1:LANES]))
                h_ref[i, poff + pp] = dec * hp + _dot_tn(_pad_rows(xw, 2 * SUBLANES).astype(BF16), bm_i)
            return carry
        lax.fori_loop(0, nseq, short_body, 0)

    y = y_ref[:, goff:goff + GROUP_W] * (z * _sigmoid(z))
    ms = jnp.mean(y * y, axis=-1, keepdims=True)
    y_ref[:, goff:goff + GROUP_W] = y * lax.rsqrt(ms + RMS_EPS) * nw_ref[:, goff:goff + GROUP_W]


def _ssd(xc3, xw, h0_all, new_all, li, dtb, a, dsk, nw, bsz, l, valid, base_tile):
    r = ROW_TILE
    ch, nseq, nc, nb = _seq_tiling(bsz, l)
    gps = SSD_GROUPS_PER_STEP if nseq == 1 else 1
    gw, sw, pairs = gps * GROUP_W, gps * SSD_STATE, gps * PAIRS_PER_GROUP
    extra_specs, extra_args, aliases = _carry_alias(new_all, 10)
    kern = functools.partial(_ssd_kernel, gps=gps, ch=ch, nseq=nseq, valid=valid, nc=nc)
    hspec = pl.BlockSpec((1, nseq, pairs, LANES, SSD_STATE), lambda bb, g, c: (li, bb, g, 0, 0))
    b_off = SSD_INNER // sw
    c_off = b_off + SSD_GROUPS // gps

    def vec(w):
        return pl.BlockSpec((1, w), lambda bb, g, c: (0, g))

    return pl.pallas_call(
        kern,
        grid=(nb, SSD_GROUPS // gps, nc),
        in_specs=[pl.BlockSpec((1, r, gw), lambda bb, g, c: (bb * nc + c, 0, g)),
                  pl.BlockSpec((1, r, sw), lambda bb, g, c: (bb * nc + c, 0, b_off + g)),
                  pl.BlockSpec((1, r, sw), lambda bb, g, c: (bb * nc + c, 0, c_off + g)),
                  pl.BlockSpec((r, gw), lambda bb, g, c: (base_tile + bb * nc + c, OFF_Z // gw + g)),
                  pl.BlockSpec((r, sw), lambda bb, g, c: (base_tile + bb * nc + c, OFF_DT // sw + g)),
                  vec(sw), vec(sw), vec(gw), vec(gw), hspec] + extra_specs,
        out_specs=[pl.BlockSpec((r, gw), lambda bb, g, c: (bb * nc + c, g)), hspec],
        out_shape=[jax.ShapeDtypeStruct((bsz * l, SSD_INNER), F32),
                   jax.ShapeDtypeStruct(h0_all.shape, F32)],
        scratch_shapes=[pltpu.VMEM((nseq, pairs, LANES, SSD_STATE), F32),
                        pltpu.VMEM((PAIRS_PER_GROUP, r, LANES), F32), pltpu.VMEM((PAIRS_PER_GROUP, r, LANES), F32)],
        input_output_aliases=aliases,
        compiler_params=_cparams(("arbitrary", "arbitrary", "arbitrary")),
        name="ssd",
    )(xc3, xc3, xc3, xw, xw, dtb, a, dsk, nw, h0_all, *extra_args)


def _layer_norm(x, g, b):
    mu = jnp.mean(x, axis=-1, keepdims=True)
    xc = x - mu
    var = jnp.mean(xc * xc, axis=-1, keepdims=True)
    return xc * lax.rsqrt(var + LN_EPS) * g + b


def _postmix_kernel(op_ref, os_ref, yp_ref, ys_ref, ga_ref, gb_ref, x_ref, hgp_ref, ssp_ref, wo_ref, g_ref, b_ref,
                    rwh_ref, rwl_ref, rb_ref, h_ref, h3_ref, lg_ref, *, n_p):
    first = pl.program_id(0) < n_p
    o = jnp.where(first, op_ref[...], os_ref[...])
    y = jnp.where(first, yp_ref[...], ys_ref[...])
    out_a = _dot(o.astype(BF16), hgp_ref[...])
    out_b = _dot(y.astype(BF16), ssp_ref[...])
    merged = _sigmoid(ga_ref[...]) * out_a + _sigmoid(gb_ref[...]) * out_b
    mix = _dot(merged.astype(BF16), wo_ref[...])
    h = _layer_norm(ALPHA * x_ref[...] + mix, g_ref[...], b_ref[...])
    h_ref[...] = h
    _store_rows(h3_ref, h)
    hh = h.astype(BF16)
    hl = (h - hh.astype(F32)).astype(BF16)
    lg_ref[...] = (_dot(hh, rwh_ref[...]) + _dot(hl, rwh_ref[...]) + _dot(hh, rwl_ref[...])) + rb_ref[...]


def _postmix(o_p, o_s, y_p, y_s, xw, x, hgp, ssp, wo, g, b, rwh, rwl, rb):
    t = x.shape[0]
    tm = TOKEN_TILE
    n_p = o_p.shape[0] // tm

    def full(a):
        return pl.BlockSpec(a.shape, lambda i: (0, 0))

    def first(w):
        return pl.BlockSpec((tm, w), lambda i: (jnp.minimum(i, n_p - 1), 0))

    def second(w):
        return pl.BlockSpec((tm, w), lambda i: (jnp.maximum(i - n_p, 0), 0))

    return pl.pallas_call(
        functools.partial(_postmix_kernel, n_p=n_p),
        grid=(t // tm,),
        in_specs=[first(HG_DIM), second(HG_DIM), first(SSD_INNER), second(SSD_INNER),
                  pl.BlockSpec((tm, D_MODEL), lambda i: (i, OFF_GA // D_MODEL)),
                  pl.BlockSpec((tm, D_MODEL), lambda i: (i, OFF_GB // D_MODEL)),
                  pl.BlockSpec((tm, D_MODEL), lambda i: (i, 0)),
                  full(hgp), full(ssp), full(wo), full(g), full(b), full(rwh), full(rwl), full(rb)],
        out_specs=[pl.BlockSpec((tm, D_MODEL), lambda i: (i, 0)),
                   pl.BlockSpec((tm, ROW_SUB, LANES), lambda i: (i, 0, 0)),
                   pl.BlockSpec((tm, LANES), lambda i: (i, 0))],
        out_shape=[jax.ShapeDtypeStruct((t, D_MODEL), F32),
                   jax.ShapeDtypeStruct((t, ROW_SUB, LANES), F32),
                   jax.ShapeDtypeStruct((t, LANES), F32)],
        compiler_params=_cparams(("arbitrary",)),
        name="postmix",
    )(o_p, o_s, y_p, y_s, xw, xw, x, hgp, ssp, wo, g, b, rwh, rwl, rb)


def _moe_kernel(be_ref, nu_ref, tokc_ref, tokn_ref, dstp_ref, dstc_ref, h_hbm, wgu_ref, bgu_ref, wd_ref, bd_ref,
                ys_hbm, xbuf, obuf, xb_scr, act_scr, gate_scr, wgu_b, wd_b, gsem, ssem, *, nblk, trash_row):
    j = pl.program_id(0)
    nu = nu_ref[0]
    slot = j & 1
    nslot = 1 - slot

    def gather_start(tok_ref, s, i):
        pltpu.make_async_copy(h_hbm.at[pl.ds(tok_ref[0, 0, i], 1)], xbuf.at[s, pl.ds(i, 1)], gsem.at[s]).start()

    def scatter_start(dst_ref, s, i):
        pltpu.make_async_copy(obuf.at[s, pl.ds(i, 1)], ys_hbm.at[pl.ds(dst_ref[0, 0, i], 1)], ssem.at[s]).start()

    def gather_wait(s):
        pltpu.make_async_copy(h_hbm.at[pl.ds(0, MOE_BM)], xbuf.at[s], gsem.at[s]).wait()

    def scatter_wait(s):
        pltpu.make_async_copy(obuf.at[s], ys_hbm.at[pl.ds(0, MOE_BM)], ssem.at[s]).wait()

    @pl.when(j == 0)
    def _():
        obuf[...] = jnp.zeros(obuf.shape, F32)
        pltpu.make_async_copy(obuf.at[0], ys_hbm.at[pl.ds(trash_row, MOE_BM)], ssem.at[0]).start()

        def body(i, carry):
            gather_start(tokc_ref, 0, i)
            return carry
        lax.fori_loop(0, MOE_BM, body, 0, unroll=8)

    first_of_expert = (j == 0) | (be_ref[j] != be_ref[jnp.maximum(j - 1, 0)])

    @pl.when((j < nu) & first_of_expert)
    def _():
        rows = D_MODEL // 8
        for c in range(8):
            wgu_b[c * rows:(c + 1) * rows] = wgu_ref[0, 0, c * rows:(c + 1) * rows].astype(BF16)
            wd_b[c * rows:(c + 1) * rows] = wd_ref[0, 0, c * rows:(c + 1) * rows].astype(BF16)

    @pl.when(j < nu)
    def _():
        gather_wait(slot)
        scatter_wait(slot)
        xb_scr[...] = _load_rows(xbuf.at[slot]).astype(BF16)
        for i in range(MOE_BM):
            gather_start(tokn_ref, nslot, i)
        gate = _dot(xb_scr[...], wgu_b[:, :D_FF]) + bgu_ref[0, 0, :, :D_FF]
        gate_scr[...] = jnp.minimum(gate, SWIGLU_LIMIT)

    run = (j < nu).astype(jnp.int32)

    def up_proj(_, carry):
        for i in range(MOE_SCATTER_SPLIT):
            scatter_start(dstp_ref, nslot, i)
        up = _dot(xb_scr[...], wgu_b[:, D_FF:]) + bgu_ref[0, 0, :, D_FF:]
        up = jnp.clip(up, -SWIGLU_LIMIT, SWIGLU_LIMIT)
        gate = gate_scr[...]
        act_scr[...] = ((up + 1.0) * gate * _sigmoid(SWIGLU_ALPHA * gate)).astype(BF16)
        return carry
    lax.fori_loop(0, run, up_proj, 0)

    def down(_, carry):
        for i in range(MOE_SCATTER_SPLIT, MOE_BM):
            scatter_start(dstp_ref, nslot, i)
        _store_rows(obuf.at[slot], _dot(act_scr[...], wd_b[...]) + bd_ref[0, 0])
        return carry
    lax.fori_loop(0, run, down, 0)

    @pl.when(j == nu - 1)
    def _():
        def body(i, carry):
            scatter_start(dstc_ref, slot, i)
            return carry
        lax.fori_loop(0, MOE_BM, body, 0, unroll=8)

    @pl.when(j == nblk - 1)
    def _():
        scatter_wait((nu - 1) & 1)
        scatter_wait(nu & 1)
        gather_wait(nu & 1)


def _moe_experts(block_e, n_used, row_tok, row_dst, row_dst_prev, h, wgu, bgu, wd, bd, li):
    t = h.shape[0]
    nblk = row_tok.shape[0]
    trash_row = TOP_K * t
    kern = functools.partial(_moe_kernel, nblk=nblk, trash_row=trash_row)

    def smem(imap):
        return pl.BlockSpec((1, 1, MOE_BM), imap, memory_space=pltpu.SMEM)

    grid_spec = pltpu.PrefetchScalarGridSpec(
        num_scalar_prefetch=2,
        grid=(nblk,),
        in_specs=[smem(lambda j, be, nu: (j, 0, 0)),
                  smem(lambda j, be, nu: (jnp.minimum(j + 1, nblk - 1), 0, 0)),
                  smem(lambda j, be, nu: (j, 0, 0)),
                  smem(lambda j, be, nu: (j, 0, 0)),
                  pl.BlockSpec(memory_space=pl.ANY),
                  pl.BlockSpec((1, 1, D_MODEL, 2 * D_FF), lambda j, be, nu: (li, be[j], 0, 0)),
                  pl.BlockSpec((1, 1, 1, 2 * D_FF), lambda j, be, nu: (li, be[j], 0, 0)),
                  pl.BlockSpec((1, 1, D_FF, D_MODEL), lambda j, be, nu: (li, be[j], 0, 0)),
                  pl.BlockSpec((1, 1, 1, D_MODEL), lambda j, be, nu: (li, be[j], 0, 0))],
        out_specs=pl.BlockSpec(memory_space=pl.ANY),
        scratch_shapes=[pltpu.VMEM((2, MOE_BM, ROW_SUB, LANES), F32), pltpu.VMEM((2, MOE_BM, ROW_SUB, LANES), F32),
                        pltpu.VMEM((MOE_BM, D_MODEL), BF16), pltpu.VMEM((MOE_BM, D_FF), BF16),
                        pltpu.VMEM((MOE_BM, D_FF), F32),
                        pltpu.VMEM((D_MODEL, 2 * D_FF), BF16), pltpu.VMEM((D_FF, D_MODEL), BF16),
                        pltpu.SemaphoreType.DMA((2,)), pltpu.SemaphoreType.DMA((2,))],
    )
    return pl.pallas_call(
        kern,
        grid_spec=grid_spec,
        out_shape=jax.ShapeDtypeStruct((trash_row + 2 * MOE_BM, ROW_SUB, LANES), F32),
        compiler_params=pltpu.CompilerParams(dimension_semantics=("arbitrary",), vmem_limit_bytes=MOE_VMEM_LIMIT),
        name="moe_experts",
    )(block_e, n_used, row_tok, row_tok, row_dst_prev, row_dst, h, wgu, bgu, wd, bd)


def _combine_kernel(h_ref, y0_ref, y1_ref, y2_ref, y3_ref, gt_ref, g_ref, b_ref, x_ref, xb_ref):
    gt = gt_ref[...]
    y = gt[:, 0:1] * _load_rows(y0_ref)
    for kk, y_ref in enumerate((y1_ref, y2_ref, y3_ref), start=1):
        y = y + gt[:, kk:kk + 1] * _load_rows(y_ref)
    x = _layer_norm(ALPHA * h_ref[...] + y, g_ref[...], b_ref[...])
    x_ref[...] = x
    xb_ref[...] = x.astype(BF16)


def _combine(h, ys, gates, g, b, tm):
    t = h.shape[0]
    nt = t // tm

    def yspec(kk):
        return pl.BlockSpec((tm, ROW_SUB, LANES), lambda i: (kk * nt + i, 0, 0))

    return pl.pallas_call(
        _combine_kernel,
        grid=(nt,),
        in_specs=[pl.BlockSpec((tm, D_MODEL), lambda i: (i, 0)),
                  yspec(0), yspec(1), yspec(2), yspec(3),
                  pl.BlockSpec((tm, TOP_K), lambda i: (i, 0)),
                  pl.BlockSpec((1, D_MODEL), lambda i: (0, 0)),
                  pl.BlockSpec((1, D_MODEL), lambda i: (0, 0))],
        out_specs=[pl.BlockSpec((tm, D_MODEL), lambda i: (i, 0)),
                   pl.BlockSpec((tm, D_MODEL), lambda i: (i, 0))],
        out_shape=[jax.ShapeDtypeStruct((t, D_MODEL), F32),
                   jax.ShapeDtypeStruct((t, D_MODEL), BF16)],
        compiler_params=_cparams(("arbitrary",)),
        name="combine_ln",
    )(h, ys, ys, ys, ys, gates, g, b)


def _split_cols(a, sizes):
    out, off = [], 0
    for s in sizes:
        out.append(a[..., off:off + s])
        off += s
    return out


def _per_group(vec_heads):
    v = vec_heads.reshape(SSD_GROUPS, SSD_HPG)
    return jnp.pad(v, ((0, 0), (0, LANES - SSD_HPG))).reshape(1, DT_W)


def _prep_layer(p, l):
    wq, wf, wv, wg, wz, wxbc, wdt, wga, wgb = _split_cols(p["w_in"][l], IN_SPLITS)
    wdt = jnp.pad(wdt.reshape(D_MODEL, SSD_GROUPS, SSD_HPG), ((0, 0), (0, 0), (0, LANES - SSD_HPG)))
    w_in = jnp.concatenate([wq, wf, wv, wg, wz, wxbc, wga, wgb, wdt.reshape(D_MODEL, DT_W)], axis=1).astype(BF16)
    lb = p["lb_all"][l].reshape(HG_HEADS, 1, HG_DK)
    rw = jnp.pad(p["router_w"][l], ((0, 0), (0, LANES - N_EXPERTS)))
    rwh = rw.astype(BF16)
    rwl = (rw - rwh.astype(F32)).astype(BF16)
    return dict(
        w_in=w_in,
        la=jnp.log(lb), l1m=jnp.log1p(-lb), oml=1.0 - lb,
        hg_nw=p["hg_norm_w"][l].reshape(HG_HEADS, 1, HG_DV),
        hgp=p["hg_proj"][l].astype(BF16),
        conv_w=p["conv_w"][l], conv_b=p["conv_b"][l].reshape(1, CONV_DIM),
        dtb=_per_group(p["dt_bias"][l]),
        a=_per_group(-jnp.exp(p["a_log"][l].astype(F32))),
        dsk=jnp.repeat(p["d_skip"][l], SSD_HEADDIM).reshape(1, SSD_INNER),
        ssd_nw=p["ssd_norm_w"][l].reshape(1, SSD_INNER),
        ssp=p["ssd_proj"][l].astype(BF16),
        wo=p["w_out"][l].astype(BF16),
        ln1_g=p["ln1_g"][l].reshape(1, D_MODEL), ln1_b=p["ln1_b"][l].reshape(1, D_MODEL),
        rwh=rwh, rwl=rwl,
        rb=jnp.pad(p["router_b"][l], (0, LANES - N_EXPERTS)).reshape(1, LANES),
        ln2_g=p["ln2_g"][l].reshape(1, D_MODEL), ln2_b=p["ln2_b"][l].reshape(1, D_MODEL),
    )


def _prep_experts(p):
    return dict(
        wgu=p["w_gu"], bgu=p["b_gu"].reshape(DEPTH, N_EXPERTS, 1, 2 * D_FF),
        wd=p["w_down"], bd=p["b_down"].reshape(DEPTH, N_EXPERTS, 1, D_MODEL),
    )


def _moe(h, h3, logits, lp, ep, li, tm):
    t = h.shape[0]
    s = t * TOP_K
    top_v, top_e = lax.top_k(logits[:, :N_EXPERTS], TOP_K)
    gates = jax.nn.softmax(top_v, axis=-1)
    e_slot = top_e.reshape(s).astype(jnp.int32)
    slot_bits = max(s - 1, 1).bit_length()
    order = jnp.sort((e_slot << slot_bits) | jnp.arange(s, dtype=jnp.int32)) & ((1 << slot_bits) - 1)
    counts = jnp.sum((e_slot[:, None] == jnp.arange(N_EXPERTS, dtype=jnp.int32)[None, :]).astype(jnp.int32), axis=0)
    padded = (counts + MOE_BM - 1) // MOE_BM * MOE_BM
    pend = jnp.cumsum(padded)
    cend = pend - padded + counts
    n_blocks = (s + N_EXPERTS * (MOE_BM - 1) + MOE_BM - 1) // MOE_BM
    nrows = n_blocks * MOE_BM
    rows = jnp.arange(nrows, dtype=jnp.int32)[:, None]
    before = pend[None, :] <= rows
    pad_before = jnp.sum(jnp.where(before, padded - counts, 0), axis=1)
    is_pad = jnp.any((cend[None, :] <= rows) & (rows < pend[None, :]), axis=1) | (rows[:, 0] >= pend[-1])
    slot = order[jnp.clip(rows[:, 0] - pad_before, 0, s - 1)]
    tok = slot // TOP_K
    row_tok = jnp.where(is_pad, 0, tok)
    row_dst = jnp.where(is_pad, TOP_K * t + rows[:, 0] % (2 * MOE_BM), (slot - tok * TOP_K) * t + tok)
    blk_rows = jnp.arange(n_blocks, dtype=jnp.int32)[:, None] * MOE_BM
    block_e = jnp.minimum(jnp.sum((pend[None, :] <= blk_rows).astype(jnp.int32), axis=1), N_EXPERTS - 1)
    n_used = (pend[-1:] // MOE_BM).astype(jnp.int32)
    first_prev = TOP_K * t + MOE_BM + jnp.arange(MOE_BM, dtype=jnp.int32)
    row_dst_prev = jnp.concatenate([first_prev, row_dst[:-MOE_BM]])
    ys = _moe_experts(block_e, n_used, row_tok.reshape(n_blocks, 1, MOE_BM), row_dst.reshape(n_blocks, 1, MOE_BM),
                      row_dst_prev.reshape(n_blocks, 1, MOE_BM), h3, ep["wgu"], ep["bgu"], ep["wd"], ep["bd"], li)
    return _combine(h, ys, gates, lp["ln2_g"], lp["ln2_b"], tm)


def _mixers(xw, li, lp, grp, new):
    bsz, l, valid, base = grp["bsz"], grp["l"], grp["valid"], grp["base"]
    t = bsz * l
    prev8 = jnp.pad(grp["s_conv"][li], ((0, 0), (SUBLANES - (CONV_K - 1), 0), (0, 0)))
    xc = _conv(xw, prev8, lp["conv_w"], lp["conv_b"], bsz, l, base)
    tail = base + jnp.arange(bsz, dtype=jnp.int32)[:, None] * l + jnp.arange(valid - (CONV_K - 1), valid)[None, :]
    conv_new = xw[tail.reshape(-1)][:, OFF_XBC:OFF_XBC + CONV_DIM].reshape(bsz, CONV_K - 1, CONV_DIM)
    o, hg = _hgrn(xw, grp["s_hg"], new["hg"], li, lp["la"], lp["l1m"], lp["oml"], lp["hg_nw"], bsz, l, valid,
                  base // ROW_TILE)
    y, ssm = _ssd(xc.reshape(t // ROW_TILE, ROW_TILE, CONV_DIM), xw, grp["s_ssm"], new["ssm"], li, lp["dtb"],
                  lp["a"], lp["dsk"], lp["ssd_nw"], bsz, l, valid, base // ROW_TILE)
    return o, y, dict(hg=hg, ssm=ssm, conv=new["conv"] + [conv_new])


def _forward(x, groups, layers, ep):
    t = x.shape[0]
    xb = x.astype(BF16)
    new = [dict(hg=jnp.zeros(g["s_hg"].shape, F32), ssm=jnp.zeros(g["s_ssm"].shape, F32), conv=[]) for g in groups]
    for li, lp in enumerate(layers):
        xw = _matmul(xb, lp["w_in"], 512 if t % 512 == 0 else TOKEN_TILE, N_W // 4)
        mix = [_mixers(xw, li, lp, grp, n) for grp, n in zip(groups, new)]
        new = [m[2] for m in mix]
        h, h3, logits = _postmix(mix[0][0], mix[1][0], mix[0][1], mix[1][1], xw, x, lp["hgp"], lp["ssp"], lp["wo"],
                             lp["ln1_g"], lp["ln1_b"], lp["rwh"], lp["rwl"], lp["rb"])
        x, xb = _moe(h, h3, logits, lp, ep, li, TOKEN_TILE)
    return x, [dict(hg=n["hg"], ssm=n["ssm"], conv=jnp.stack(n["conv"])) for n in new]


def kernel(x_prompt, x_sample, state_hgrn, state_ssm, state_conv, hg_lower_bounds, w_in, hg_norm_w, hg_proj,
           conv_w, conv_b, dt_bias, a_log, d_skip, ssd_norm_w, ssd_proj, w_out, ln1_g, ln1_b, router_w,
           router_b, w_gu, b_gu, w_down, b_down, ln2_g, ln2_b):
    lb_all = jnp.cumsum(jax.nn.softmax(hg_lower_bounds.astype(F32), axis=0), axis=0)
    lb_all = lb_all - lb_all[0]
    p = dict(lb_all=lb_all, w_in=w_in, hg_norm_w=hg_norm_w, hg_proj=hg_proj, conv_w=conv_w, conv_b=conv_b,
             dt_bias=dt_bias, a_log=a_log, d_skip=d_skip, ssd_norm_w=ssd_norm_w, ssd_proj=ssd_proj, w_out=w_out,
             ln1_g=ln1_g, ln1_b=ln1_b, router_w=router_w, router_b=router_b, w_gu=w_gu, b_gu=b_gu,
             w_down=w_down, b_down=b_down, ln2_g=ln2_g, ln2_b=ln2_b)
    layers = [_prep_layer(p, l) for l in range(DEPTH)]
    ep = _prep_experts(p)

    return _run(x_prompt, x_sample, state_hgrn, state_ssm, state_conv, layers, ep)


def _run(x_prompt, x_sample, state_hgrn, state_ssm, state_conv, layers, ep):
    bp, lprompt, _ = x_prompt.shape
    bs, ls, _ = x_sample.shape
    tp = bp * lprompt

    def pairs(s_ssm):
        return s_ssm.reshape(s_ssm.shape[:2] + (SSD_PAIRS, LANES, SSD_STATE))

    groups = [
        dict(bsz=bp, l=lprompt, valid=lprompt, base=0,
             s_hg=jnp.zeros((DEPTH, bp) + state_hgrn.shape[2:], F32),
             s_ssm=pairs(jnp.zeros((DEPTH, bp) + state_ssm.shape[2:], F32)),
             s_conv=jnp.zeros((DEPTH, bp) + state_conv.shape[2:], F32)),
        dict(bsz=bs, l=SUBLANES, valid=ls, base=tp, s_hg=state_hgrn, s_ssm=pairs(state_ssm), s_conv=state_conv),
    ]
    xs = jnp.pad(x_sample, ((0, 0), (0, SUBLANES - ls), (0, 0)))
    x = jnp.concatenate([x_prompt.reshape(tp, D_MODEL), xs.reshape(bs * SUBLANES, D_MODEL)], axis=0)
    x, new = _forward(x, groups, layers, ep)
    y_p = x[:tp].reshape(bp, lprompt, D_MODEL)
    y_s = x[tp:].reshape(bs, SUBLANES, D_MODEL)[:, :ls]
    (n_p, n_s) = new
    return (y_p, y_s, n_p["hg"], n_p["ssm"].reshape((DEPTH, bp) + state_ssm.shape[2:]), n_p["conv"],
            n_s["hg"], n_s["ssm"].reshape(state_ssm.shape), n_s["conv"])
```

```python
import functools

import jax
import jax.numpy as jnp
from jax import lax
from jax.experimental import pallas as pl
from jax.experimental.pallas import tpu as pltpu

F32 = jnp.float32
BF16 = jnp.bfloat16

D_MODEL = 1024
DEPTH = 2
HG_HEADS = 8
HG_DK = 128
HG_DV = 128
HG_DIM = HG_HEADS * HG_DK
SSD_INNER = 2 * D_MODEL
SSD_HEADDIM = 64
SSD_HEADS = SSD_INNER // SSD_HEADDIM
SSD_GROUPS = 4
SSD_HPG = SSD_HEADS // SSD_GROUPS
SSD_STATE = 128
SSD_PAIRS = SSD_HEADS // 2
PAIRS_PER_GROUP = SSD_PAIRS // SSD_GROUPS
GROUP_W = SSD_INNER // SSD_GROUPS
CONV_K = 4
CONV_DIM = SSD_INNER + 2 * SSD_GROUPS * SSD_STATE
N_EXPERTS = 32
TOP_K = 4
D_FF = D_MODEL
SWIGLU_LIMIT = 7.0
SWIGLU_ALPHA = 1.702
ALPHA = (2.0 * DEPTH) ** 0.25
LN_EPS = 1e-5
RMS_EPS = 1e-6

LANES = 128
SUBLANES = 8
ROW_SUB = D_MODEL // LANES

DT_W = SSD_GROUPS * LANES
OFF_Q = 0
OFF_F = HG_DIM
OFF_V = 2 * HG_DIM
OFF_G = 3 * HG_DIM
OFF_Z = 4 * HG_DIM
OFF_XBC = OFF_Z + SSD_INNER
OFF_GA = OFF_XBC + CONV_DIM
OFF_GB = OFF_GA + D_MODEL
OFF_DT = OFF_GB + D_MODEL
N_W = OFF_DT + DT_W
IN_SPLITS = (HG_DIM, HG_DIM, HG_DIM, HG_DIM, SSD_INNER, CONV_DIM, SSD_HEADS, D_MODEL, D_MODEL)

ROW_TILE = 128
CONV_ROWS = 256
TOKEN_TILE = 256
MOE_BM = 256
MOE_SCATTER_SPLIT = 128
VMEM_LIMIT = 48 * 1024 * 1024
MOE_VMEM_LIMIT = 56 * 1024 * 1024
LOG2E = 1.4426950408889634
HALO = 16
HGRN_HEADS_PER_STEP = 8
HGRN_SUB = 8
SSD_GROUPS_PER_STEP = 4


def _cparams(sem):
    return pltpu.CompilerParams(dimension_semantics=sem, vmem_limit_bytes=VMEM_LIMIT)


def _sigmoid(x):
    return 0.5 * jnp.tanh(0.5 * x) + 0.5


def _log1p(x):
    return jnp.log(1.0 + x)


def _dot(a, b):
    return jnp.dot(a, b, preferred_element_type=F32)


def _dot_nt(a, b):
    return lax.dot_general(a, b, (((1,), (1,)), ((), ())), preferred_element_type=F32)


def _dot_tn(a, b):
    return lax.dot_general(a, b, (((0,), (0,)), ((), ())), preferred_element_type=F32)


def _load_rows(ref3):
    return pltpu.einshape("tsl->t(sl)", ref3[...])


def _store_rows(ref3, x):
    ref3[...] = pltpu.einshape("t(sl)->tsl", x, s=ROW_SUB)


def _mm_kernel(x_ref, w_ref, o_ref):
    o_ref[...] = _dot(x_ref[...], w_ref[...])


def _matmul(x, w, tm, tn):
    m, k = x.shape
    n = w.shape[1]
    return pl.pallas_call(
        _mm_kernel,
        grid=(n // tn, m // tm),
        in_specs=[pl.BlockSpec((tm, k), lambda j, i: (i, 0)),
                  pl.BlockSpec((k, tn), lambda j, i: (0, j))],
        out_specs=pl.BlockSpec((tm, tn), lambda j, i: (i, j)),
        out_shape=jax.ShapeDtypeStruct((m, n), F32),
        compiler_params=_cparams(("arbitrary", "arbitrary")),
        name="in_proj",
    )(x, w)


def _conv_kernel(x_ref, pb_ref, p0_ref, w_ref, b_ref, o_ref):
    i = pl.program_id(1)
    x = x_ref[...]
    prev = jnp.where(i == 0, p0_ref[0], pb_ref[...])
    full = jnp.concatenate([prev, x], axis=0)
    w = w_ref[...]
    acc = b_ref[...] + w[CONV_K - 1:CONV_K] * x
    for s in range(1, CONV_K):
        xs = pltpu.roll(full, s, 0)[SUBLANES:]
        acc = acc + w[CONV_K - 1 - s:CONV_K - s] * xs
    o_ref[...] = acc * _sigmoid(acc)


def _conv_short_kernel(x_ref, p_ref, w_ref, b_ref, o_ref):
    x = x_ref[...]
    p = p_ref[...]
    rc = lax.broadcasted_iota(jnp.int32, x.shape, 0) & (SUBLANES - 1)
    w = w_ref[...]
    acc = b_ref[...] + w[CONV_K - 1:CONV_K] * x
    for s in range(1, CONV_K):
        xs = jnp.where(rc >= s, pltpu.roll(x, s, 0), pltpu.roll(p, (s - SUBLANES) % ROW_TILE, 0))
        acc = acc + w[CONV_K - 1 - s:CONV_K - s] * xs
    o_ref[...] = acc * _sigmoid(acc)


def _conv(xw, prev8, conv_w, conv_b, bsz, l, base_rows):
    cb = OFF_XBC // CONV_DIM
    wspec = [pl.BlockSpec((CONV_K, CONV_DIM), lambda *_: (0, 0)), pl.BlockSpec((1, CONV_DIM), lambda *_: (0, 0))]
    out_shape = jax.ShapeDtypeStruct((bsz * l, CONV_DIM), F32)
    if l == SUBLANES:
        r = ROW_TILE
        return pl.pallas_call(
            _conv_short_kernel,
            grid=(bsz * l // r,),
            in_specs=[pl.BlockSpec((r, CONV_DIM), lambda i: (base_rows // r + i, cb)),
                      pl.BlockSpec((r, CONV_DIM), lambda i: (i, 0))] + wspec,
            out_specs=pl.BlockSpec((r, CONV_DIM), lambda i: (i, 0)),
            out_shape=out_shape,
            compiler_params=_cparams(("arbitrary",)),
            name="conv_silu_short",
        )(xw, prev8.reshape(bsz * SUBLANES, CONV_DIM), conv_w, conv_b)
    r = CONV_ROWS
    tps = l // r
    rb = r // SUBLANES
    return pl.pallas_call(
        _conv_kernel,
        grid=(bsz, tps),
        in_specs=[pl.BlockSpec((r, CONV_DIM), lambda bi, i: (base_rows // r + bi * tps + i, cb)),
                  pl.BlockSpec((SUBLANES, CONV_DIM),
                               lambda bi, i: (jnp.maximum(base_rows // SUBLANES + (bi * tps + i) * rb - 1, 0), cb)),
                  pl.BlockSpec((1, SUBLANES, CONV_DIM), lambda bi, i: (bi, 0, 0))] + wspec,
        out_specs=pl.BlockSpec((r, CONV_DIM), lambda bi, i: (bi * tps + i, 0)),
        out_shape=out_shape,
        compiler_params=_cparams(("arbitrary", "arbitrary")),
        name="conv_silu",
    )(xw, xw, prev8, conv_w, conv_b)


def _pad_rows(x, rows):
    if x.shape[0] == rows:
        return x
    return jnp.concatenate([x, jnp.zeros((rows - x.shape[0], x.shape[1]), x.dtype)], axis=0)


def _split3(x):
    hi = x.astype(BF16)
    r1 = x - hi.astype(F32)
    mid = r1.astype(BF16)
    lo = (r1 - mid.astype(F32)).astype(BF16)
    return hi, mid, lo


def _hgrn_kernel(*refs, hps, nseq, nc, **kw):
    refs = refs[:9] + refs[10:]
    s0_ref, sout_ref, st_ref = refs[8], refs[10], refs[11]
    c = pl.program_id(2)

    @pl.when(c == 0)
    def _():
        for hh in range(hps):
            for i in range(nseq):
                st_ref[hh, i] = s0_ref[0, i, hh].T

    for hh in range(hps):
        _hgrn_head(hh, *refs, nseq=nseq, **kw)

    @pl.when(c == nc - 1)
    def _():
        for hh in range(hps):
            for i in range(nseq):
                sout_ref[0, i, hh] = st_ref[hh, i].T


def _hgrn_head(hh, q_ref, f_ref, v_ref, g_ref, la_ref, l1m_ref, oml_ref, nw_ref, s0_ref,
               o_ref, sout_ref, st_ref, k_scr, b_scr, *, ch, sub, nseq, valid):
    r = ROW_TILE
    cols = slice(hh * LANES, (hh + 1) * LANES)
    q = q_ref[:, cols]
    fr = f_ref[:, cols]
    v = v_ref[:, cols]
    q = q * _sigmoid(q)
    la = la_ref[hh]
    lsig = jnp.minimum(fr, 0.0) - _log1p(jnp.exp(-jnp.abs(fr)))
    cc = l1m_ref[hh] + lsig
    logf = jnp.maximum(la, cc) + _log1p(jnp.exp(-jnp.abs(la - cc)))
    k = oml_ref[hh] * _sigmoid(-fr)

    row = lax.broadcasted_iota(jnp.int32, (r, LANES), 0)
    lane = lax.broadcasted_iota(jnp.int32, (r, LANES), 1)
    if valid < ch:
        pad = (row & (ch - 1)) >= valid
        logf = jnp.where(pad, 0.0, logf)
        k = jnp.where(pad, 0.0, k)

    tri = lane <= row
    if ch < r:
        shift = ch.bit_length() - 1
        tri = tri & ((row >> shift) == (lane >> shift))
    tri = jnp.where(tri, 1.0, 0.0).astype(BF16)
    hi, mid, lo = _split3(logf)
    b2 = (_dot(tri, hi) + _dot(tri, mid) + _dot(tri, lo)) * LOG2E

    k_scr[hh, 0:HALO] = jnp.zeros((HALO, LANES), F32)
    b_scr[hh, 0:HALO] = jnp.zeros((HALO, LANES), F32)
    k_scr[hh, HALO:HALO + r] = k
    b_scr[hh, HALO:HALO + r] = b2
    rs = row & (sub - 1)
    scl = jnp.where(lane == rs, jnp.sum(q * k, axis=-1, keepdims=True), 0.0)
    for d in range(1, sub):
        kd = k_scr[hh, pl.ds(HALO - d, r), :]
        bd = b_scr[hh, pl.ds(HALO - d, r), :]
        sd = jnp.sum(q * kd * jnp.exp2(b2 - bd), axis=-1, keepdims=True)
        scl = jnp.where(lane == rs - d, sd, scl)

    spc = ch // sub
    pieces = []
    for m in range(r // sub):
        lo_, hi_ = m * sub, (m + 1) * sub
        piece = scl[lo_:hi_]
        if m:
            piece = pltpu.roll(piece, lo_, 1)
        j = m % spc
        if j:
            cs = (m - j) * sub
            ref_b = b2[lo_ - 1:lo_]
            qj = q[lo_:hi_] * jnp.exp2(b2[lo_:hi_] - ref_b)
            kk = k[cs:lo_] * jnp.exp2(ref_b - b2[cs:lo_])
            parts = ([jnp.zeros((cs, LANES), F32)] if cs else []) + [kk, jnp.zeros((r - lo_, LANES), F32)]
            piece = piece + _dot_nt(qj.astype(BF16), jnp.concatenate(parts, axis=0).astype(BF16))
        pieces.append(piece)
    sc = jnp.concatenate(pieces, axis=0)
    o = _dot(sc.astype(BF16), v.astype(BF16))

    inter = []
    for i in range(nseq):
        lo_, hi_ = i * ch, (i + 1) * ch
        bi = b2[lo_:hi_]
        bl = bi[ch - 1:ch]
        st = st_ref[hh, i]
        inter.append(_dot_nt((q[lo_:hi_] * jnp.exp2(bi)).astype(BF16), st.astype(BF16)))
        kh = _pad_rows(k[lo_:hi_] * jnp.exp2(bl - bi), r)
        vi = _pad_rows(v[lo_:hi_], r)
        st_ref[hh, i] = st * jnp.exp2(bl) + _dot_tn(vi.astype(BF16), kh.astype(BF16))
    o = o + (inter[0] if nseq == 1 else jnp.concatenate(inter, axis=0))

    ms = jnp.mean(o * o, axis=-1, keepdims=True)
    o_ref[:, cols] = o * lax.rsqrt(ms + RMS_EPS) * nw_ref[hh] * _sigmoid(g_ref[:, cols])


def _seq_tiling(bsz, l):
    r = ROW_TILE
    ch = min(l, r)
    nseq = r // ch
    nc = max(l // r, 1)
    nb = bsz * l // (r * nc)
    return ch, nseq, nc, nb


def _carry_alias(new_all, n_in):
    return [pl.BlockSpec(memory_space=pl.ANY)], [new_all], {n_in: 1}


def _hgrn(xw, s0_all, new_all, li, la, l1m, oml, nw, bsz, l, valid, base_tile):
    r = ROW_TILE
    ch, nseq, nc, nb = _seq_tiling(bsz, l)
    sub = min(ch, HGRN_SUB)
    hps = HGRN_HEADS_PER_STEP
    w = hps * LANES
    extra_specs, extra_args, aliases = _carry_alias(new_all, 9)
    kern = functools.partial(_hgrn_kernel, hps=hps, ch=ch, sub=sub, nseq=nseq, valid=valid, nc=nc)

    def xspec(off):
        return pl.BlockSpec((r, w), lambda bb, h, c: (base_tile + bb * nc + c, off // w + h))

    def pspec():
        return pl.BlockSpec((hps, 1, LANES), lambda bb, h, c: (h, 0, 0))

    return pl.pallas_call(
        kern,
        grid=(nb, HG_HEADS // hps, nc),
        in_specs=[xspec(OFF_Q), xspec(OFF_F), xspec(OFF_V), xspec(OFF_G),
                  pspec(), pspec(), pspec(), pspec(),
                  pl.BlockSpec((1, nseq, hps, HG_DK, HG_DV), lambda bb, h, c: (li, bb, h, 0, 0))] + extra_specs,
        out_specs=[pl.BlockSpec((r, w), lambda bb, h, c: (bb * nc + c, h)),
                   pl.BlockSpec((1, nseq, hps, HG_DK, HG_DV), lambda bb, h, c: (li, bb, h, 0, 0))],
        out_shape=[jax.ShapeDtypeStruct((bsz * l, HG_DIM), F32),
                   jax.ShapeDtypeStruct(s0_all.shape, F32)],
        scratch_shapes=[pltpu.VMEM((hps, nseq, HG_DV, HG_DK), F32),
                        pltpu.VMEM((hps, HALO + r, LANES), F32), pltpu.VMEM((hps, HALO + r, LANES), F32)],
        input_output_aliases=aliases,
        compiler_params=_cparams(("arbitrary", "arbitrary", "arbitrary")),
        name="hgrn2",
    )(xw, xw, xw, xw, la, l1m, oml, nw, s0_all, *extra_args)


def _ssd_kernel(*refs, **kw):
    _ssd_body(*(refs[:10] + refs[11:]), **kw)


def _ssd_body(xs_ref, bm_ref, cm_ref, z_ref, dt_ref, dtb_ref, a_ref, dsk_ref, nw_ref, h0_ref,
              y_ref, hout_ref, h_ref, al_scr, dl_scr, *, gps, nc, **kw):
    c = pl.program_id(2)

    @pl.when(c == 0)
    def _():
        h_ref[...] = h0_ref[0]

    for gg in range(gps):
        _ssd_group(gg, xs_ref, bm_ref, cm_ref, z_ref, dt_ref, dtb_ref, a_ref, dsk_ref, nw_ref, y_ref, h_ref,
                   al_scr, dl_scr, **kw)

    @pl.when(c == nc - 1)
    def _():
        hout_ref[0] = h_ref[...]


def _ssd_group(gg, xs_ref, bm_ref, cm_ref, z_ref, dt_ref, dtb_ref, a_ref, dsk_ref, nw_ref, y_ref, h_ref,
               al_scr, dl_scr, *, ch, nseq, valid):
    r = ROW_TILE
    scol = slice(gg * LANES, (gg + 1) * LANES)
    goff = gg * GROUP_W
    poff = gg * PAIRS_PER_GROUP
    shift = ch.bit_length() - 1
    rowl = lax.broadcasted_iota(jnp.int32, (r, LANES), 0)
    lane = lax.broadcasted_iota(jnp.int32, (r, LANES), 1)
    rc = rowl & (ch - 1)
    dtr = dt_ref[:, scol] + dtb_ref[:, scol]
    dt = jnp.maximum(dtr, 0.0) + _log1p(jnp.exp(-jnp.abs(dtr)))
    if valid < ch:
        dt = jnp.where(rc < valid, dt, 0.0)
    mask = (lane <= rowl) & ((rowl >> shift) == (lane >> shift))
    tri = jnp.where(mask, 1.0, 0.0).astype(BF16)
    hi, mid, lo = _split3(dt * a_ref[:, scol])
    acum = _dot(tri, hi) + _dot(tri, mid) + _dot(tri, lo)
    acum_t = acum.T
    dt_t = dt.T

    even_lane = lane < SSD_HEADDIM
    even_row = rowl < SSD_HEADDIM
    last_row = rc == ch - 1

    bmb = bm_ref[0, :, scol].astype(BF16)
    cmb = cm_ref[0, :, scol].astype(BF16)
    cb = _dot_nt(cmb, bmb)
    xs = xs_ref[0, :, goff:goff + GROUP_W]
    z = z_ref[:, goff:goff + GROUP_W]
    for pp in range(PAIRS_PER_GROUP):
        xp = xs[:, pp * LANES:(pp + 1) * LANES]
        acc = dsk_ref[:, goff + pp * LANES:goff + (pp + 1) * LANES] * xp
        for e in range(2):
            hd = 2 * pp + e
            w = cb * jnp.exp(jnp.where(mask, acum[:, hd:hd + 1] - acum_t[hd:hd + 1, :], -jnp.inf)) * dt_t[hd:hd + 1, :]
            xm = jnp.where(even_lane if e == 0 else jnp.logical_not(even_lane), xp, 0.0)
            acc = acc + _dot(w.astype(BF16), xm.astype(BF16))
        a_lane = jnp.where(even_lane, acum[:, 2 * pp:2 * pp + 1], acum[:, 2 * pp + 1:2 * pp + 2])
        dt_lane = jnp.where(even_lane, dt[:, 2 * pp:2 * pp + 1], dt[:, 2 * pp + 1:2 * pp + 2])
        ea = jnp.exp(a_lane)

        def seq_body(i, acc, pp=pp, xp=xp, a_lane=a_lane, dt_lane=dt_lane, ea=ea):
            in_seq = (rowl >> shift) == i
            al_lane = jnp.sum(jnp.where(in_seq & last_row, a_lane, 0.0), axis=0, keepdims=True)
            hp = h_ref[i, poff + pp]
            yi = _dot_nt(cmb, hp.astype(BF16)) * ea
            acc = acc + jnp.where(in_seq, yi, 0.0)
            xw = jnp.where(in_seq, xp * (jnp.exp(al_lane - a_lane) * dt_lane), 0.0)
            dec = jnp.where(even_row, jnp.exp(al_lane[:, 0:1]), jnp.exp(al_lane[:, LANES - 1:LANES]))
            h_ref[i, poff + pp] = dec * hp + _dot_tn(xw.astype(BF16), bmb)
            return acc

        if nseq == 1:
            acc = seq_body(0, acc)
        else:
            al_scr[pp] = a_lane
            dl_scr[pp] = dt_lane
        y_ref[:, goff + pp * LANES:goff + (pp + 1) * LANES] = acc

    if nseq > 1:
        def short_body(i, carry):
            rows = pl.ds(pl.multiple_of(i * ch, ch), ch)
            cm_i = _pad_rows(cm_ref[0, rows, scol], 2 * SUBLANES).astype(BF16)
            bm_i = _pad_rows(bm_ref[0, rows, scol], 2 * SUBLANES).astype(BF16)
            for pp in range(PAIRS_PER_GROUP):
                cols = slice(goff + pp * LANES, goff + (pp + 1) * LANES)
                a_i = al_scr[pp, rows, :]
                al = a_i[ch - 1:ch]
                hp = h_ref[i, poff + pp]
                yi = _dot_nt(cm_i, hp.astype(BF16))[:ch] * jnp.exp(a_i)
                y_ref[rows, cols] = y_ref[rows, cols] + yi
                xw = xs_ref[0, rows, cols] * (jnp.exp(al - a_i) * dl_scr[pp, rows, :])
                dec = jnp.where(even_row, jnp.exp(al[:, 0:1]), jnp.exp(al[:, LANES - 1:LANES]))
                h_ref[i, poff + pp] = dec * hp + _dot_tn(_pad_rows(xw, 2 * SUBLANES).astype(BF16), bm_i)
            return carry
        lax.fori_loop(0, nseq, short_body, 0, unroll=4)

    y = y_ref[:, goff:goff + GROUP_W] * (z * _sigmoid(z))
    ms = jnp.mean(y * y, axis=-1, keepdims=True)
    y_ref[:, goff:goff + GROUP_W] = y * lax.rsqrt(ms + RMS_EPS) * nw_ref[:, goff:goff + GROUP_W]


def _ssd(xc3, xw, h0_all, new_all, li, dtb, a, dsk, nw, bsz, l, valid, base_tile):
    r = ROW_TILE
    ch, nseq, nc, nb = _seq_tiling(bsz, l)
    gps = SSD_GROUPS_PER_STEP if nseq == 1 else 1
    gw, sw, pairs = gps * GROUP_W, gps * SSD_STATE, gps * PAIRS_PER_GROUP
    extra_specs, extra_args, aliases = _carry_alias(new_all, 10)
    kern = functools.partial(_ssd_kernel, gps=gps, ch=ch, nseq=nseq, valid=valid, nc=nc)
    hspec = pl.BlockSpec((1, nseq, pairs, LANES, SSD_STATE), lambda bb, g, c: (li, bb, g, 0, 0))
    b_off = SSD_INNER // sw
    c_off = b_off + SSD_GROUPS // gps

    def vec(w):
        return pl.BlockSpec((1, w), lambda bb, g, c: (0, g))

    return pl.pallas_call(
        kern,
        grid=(nb, SSD_GROUPS // gps, nc),
        in_specs=[pl.BlockSpec((1, r, gw), lambda bb, g, c: (bb * nc + c, 0, g)),
                  pl.BlockSpec((1, r, sw), lambda bb, g, c: (bb * nc + c, 0, b_off + g)),
                  pl.BlockSpec((1, r, sw), lambda bb, g, c: (bb * nc + c, 0, c_off + g)),
                  pl.BlockSpec((r, gw), lambda bb, g, c: (base_tile + bb * nc + c, OFF_Z // gw + g)),
                  pl.BlockSpec((r, sw), lambda bb, g, c: (base_tile + bb * nc + c, OFF_DT // sw + g)),
                  vec(sw), vec(sw), vec(gw), vec(gw), hspec] + extra_specs,
        out_specs=[pl.BlockSpec((r, gw), lambda bb, g, c: (bb * nc + c, g)), hspec],
        out_shape=[jax.ShapeDtypeStruct((bsz * l, SSD_INNER), F32),
                   jax.ShapeDtypeStruct(h0_all.shape, F32)],
        scratch_shapes=[pltpu.VMEM((nseq, pairs, LANES, SSD_STATE), F32),
                        pltpu.VMEM((PAIRS_PER_GROUP, r, LANES), F32), pltpu.VMEM((PAIRS_PER_GROUP, r, LANES), F32)],
        input_output_aliases=aliases,
        compiler_params=_cparams(("arbitrary", "arbitrary", "arbitrary")),
        name="ssd",
    )(xc3, xc3, xc3, xw, xw, dtb, a, dsk, nw, h0_all, *extra_args)


def _layer_norm(x, g, b):
    mu = jnp.mean(x, axis=-1, keepdims=True)
    xc = x - mu
    var = jnp.mean(xc * xc, axis=-1, keepdims=True)
    return xc * lax.rsqrt(var + LN_EPS) * g + b


def _postmix_kernel(op_ref, os_ref, yp_ref, ys_ref, ga_ref, gb_ref, x_ref, hgp_ref, ssp_ref, wo_ref, g_ref, b_ref,
                    rwh_ref, rwl_ref, rb_ref, h_ref, h3_ref, lg_ref, *, n_p):
    first = pl.program_id(0) < n_p
    o = jnp.where(first, op_ref[...], os_ref[...])
    y = jnp.where(first, yp_ref[...], ys_ref[...])
    out_a = _dot(o.astype(BF16), hgp_ref[...])
    out_b = _dot(y.astype(BF16), ssp_ref[...])
    merged = _sigmoid(ga_ref[...]) * out_a + _sigmoid(gb_ref[...]) * out_b
    mix = _dot(merged.astype(BF16), wo_ref[...])
    h = _layer_norm(ALPHA * x_ref[...] + mix, g_ref[...], b_ref[...])
    h_ref[...] = h
    _store_rows(h3_ref, h)
    hh = h.astype(BF16)
    hl = (h - hh.astype(F32)).astype(BF16)
    lg_ref[...] = (_dot(hh, rwh_ref[...]) + _dot(hl, rwh_ref[...]) + _dot(hh, rwl_ref[...])) + rb_ref[...]


def _postmix(o_p, o_s, y_p, y_s, xw, x, hgp, ssp, wo, g, b, rwh, rwl, rb):
    t = x.shape[0]
    tm = TOKEN_TILE
    n_p = o_p.shape[0] // tm

    def full(a):
        return pl.BlockSpec(a.shape, lambda i: (0, 0))

    def first(w):
        return pl.BlockSpec((tm, w), lambda i: (jnp.minimum(i, n_p - 1), 0))

    def second(w):
        return pl.BlockSpec((tm, w), lambda i: (jnp.maximum(i - n_p, 0), 0))

    return pl.pallas_call(
        functools.partial(_postmix_kernel, n_p=n_p),
        grid=(t // tm,),
        in_specs=[first(HG_DIM), second(HG_DIM), first(SSD_INNER), second(SSD_INNER),
                  pl.BlockSpec((tm, D_MODEL), lambda i: (i, OFF_GA // D_MODEL)),
                  pl.BlockSpec((tm, D_MODEL), lambda i: (i, OFF_GB // D_MODEL)),
                  pl.BlockSpec((tm, D_MODEL), lambda i: (i, 0)),
                  full(hgp), full(ssp), full(wo), full(g), full(b), full(rwh), full(rwl), full(rb)],
        out_specs=[pl.BlockSpec((tm, D_MODEL), lambda i: (i, 0)),
                   pl.BlockSpec((tm, ROW_SUB, LANES), lambda i: (i, 0, 0)),
                   pl.BlockSpec((tm, LANES), lambda i: (i, 0))],
        out_shape=[jax.ShapeDtypeStruct((t, D_MODEL), F32),
                   jax.ShapeDtypeStruct((t, ROW_SUB, LANES), F32),
                   jax.ShapeDtypeStruct((t, LANES), F32)],
        compiler_params=_cparams(("arbitrary",)),
        name="postmix",
    )(o_p, o_s, y_p, y_s, xw, xw, x, hgp, ssp, wo, g, b, rwh, rwl, rb)


def _moe_kernel(be_ref, nu_ref, tokc_ref, tokn_ref, dstp_ref, dstc_ref, h_hbm, wgu_ref, bgu_ref, wd_ref, bd_ref,
                ys_hbm, xbuf, obuf, xb_scr, act_scr, gate_scr, wgu_b, wd_b, gsem, ssem, *, nblk, trash_row):
    j = pl.program_id(0)
    nu = nu_ref[0]
    slot = j & 1
    nslot = 1 - slot

    def gather_start(tok_ref, s, i):
        pltpu.make_async_copy(h_hbm.at[pl.ds(tok_ref[0, 0, i], 1)], xbuf.at[s, pl.ds(i, 1)], gsem.at[s]).start()

    def scatter_start(dst_ref, s, i):
        pltpu.make_async_copy(obuf.at[s, pl.ds(i, 1)], ys_hbm.at[pl.ds(dst_ref[0, 0, i], 1)], ssem.at[s]).start()

    def gather_wait(s):
        pltpu.make_async_copy(h_hbm.at[pl.ds(0, MOE_BM)], xbuf.at[s], gsem.at[s]).wait()

    def scatter_wait(s):
        pltpu.make_async_copy(obuf.at[s], ys_hbm.at[pl.ds(0, MOE_BM)], ssem.at[s]).wait()

    @pl.when(j == 0)
    def _():
        obuf[...] = jnp.zeros(obuf.shape, F32)
        pltpu.make_async_copy(obuf.at[0], ys_hbm.at[pl.ds(trash_row, MOE_BM)], ssem.at[0]).start()

        def body(i, carry):
            gather_start(tokc_ref, 0, i)
            return carry
        lax.fori_loop(0, MOE_BM, body, 0, unroll=8)

    first_of_expert = (j == 0) | (be_ref[j] != be_ref[jnp.maximum(j - 1, 0)])

    @pl.when((j < nu) & first_of_expert)
    def _():
        rows = D_MODEL // 8
        for c in range(8):
            wgu_b[c * rows:(c + 1) * rows] = wgu_ref[0, 0, c * rows:(c + 1) * rows].astype(BF16)
            wd_b[c * rows:(c + 1) * rows] = wd_ref[0, 0, c * rows:(c + 1) * rows].astype(BF16)

    @pl.when(j < nu)
    def _():
        gather_wait(slot)
        scatter_wait(slot)
        xb_scr[...] = _load_rows(xbuf.at[slot]).astype(BF16)
        for i in range(MOE_BM):
            gather_start(tokn_ref, nslot, i)
        gate = _dot(xb_scr[...], wgu_b[:, :D_FF]) + bgu_ref[0, 0, :, :D_FF]
        gate_scr[...] = jnp.minimum(gate, SWIGLU_LIMIT)

    run = (j < nu).astype(jnp.int32)

    def up_proj(_, carry):
        for i in range(MOE_SCATTER_SPLIT):
            scatter_start(dstp_ref, nslot, i)
        up = _dot(xb_scr[...], wgu_b[:, D_FF:]) + bgu_ref[0, 0, :, D_FF:]
        up = jnp.clip(up, -SWIGLU_LIMIT, SWIGLU_LIMIT)
        gate = gate_scr[...]
        act_scr[...] = ((up + 1.0) * gate * _sigmoid(SWIGLU_ALPHA * gate)).astype(BF16)
        return carry
    lax.fori_loop(0, run, up_proj, 0)

    def down(_, carry):
        for i in range(MOE_SCATTER_SPLIT, MOE_BM):
            scatter_start(dstp_ref, nslot, i)
        _store_rows(obuf.at[slot], _dot(act_scr[...], wd_b[...]) + bd_ref[0, 0])
        return carry
    lax.fori_loop(0, run, down, 0)

    @pl.when(j == nu - 1)
    def _():
        def body(i, carry):
            scatter_start(dstc_ref, slot, i)
            return carry
        lax.fori_loop(0, MOE_BM, body, 0, unroll=8)

    @pl.when(j == nblk - 1)
    def _():
        scatter_wait((nu - 1) & 1)
        scatter_wait(nu & 1)
        gather_wait(nu & 1)


def _moe_experts(block_e, n_used, row_tok, row_dst, row_dst_prev, h, wgu, bgu, wd, bd, li):
    t = h.shape[0]
    nblk = row_tok.shape[0]
    trash_row = TOP_K * t
    kern = functools.partial(_moe_kernel, nblk=nblk, trash_row=trash_row)

    def smem(imap):
        return pl.BlockSpec((1, 1, MOE_BM), imap, memory_space=pltpu.SMEM)

    grid_spec = pltpu.PrefetchScalarGridSpec(
        num_scalar_prefetch=2,
        grid=(nblk,),
        in_specs=[smem(lambda j, be, nu: (j, 0, 0)),
                  smem(lambda j, be, nu: (jnp.minimum(j + 1, nblk - 1), 0, 0)),
                  smem(lambda j, be, nu: (j, 0, 0)),
                  smem(lambda j, be, nu: (j, 0, 0)),
                  pl.BlockSpec(memory_space=pl.ANY),
                  pl.BlockSpec((1, 1, D_MODEL, 2 * D_FF), lambda j, be, nu: (li, be[j], 0, 0)),
                  pl.BlockSpec((1, 1, 1, 2 * D_FF), lambda j, be, nu: (li, be[j], 0, 0)),
                  pl.BlockSpec((1, 1, D_FF, D_MODEL), lambda j, be, nu: (li, be[j], 0, 0)),
                  pl.BlockSpec((1, 1, 1, D_MODEL), lambda j, be, nu: (li, be[j], 0, 0))],
        out_specs=pl.BlockSpec(memory_space=pl.ANY),
        scratch_shapes=[pltpu.VMEM((2, MOE_BM, ROW_SUB, LANES), F32), pltpu.VMEM((2, MOE_BM, ROW_SUB, LANES), F32),
                        pltpu.VMEM((MOE_BM, D_MODEL), BF16), pltpu.VMEM((MOE_BM, D_FF), BF16),
                        pltpu.VMEM((MOE_BM, D_FF), F32),
                        pltpu.VMEM((D_MODEL, 2 * D_FF), BF16), pltpu.VMEM((D_FF, D_MODEL), BF16),
                        pltpu.SemaphoreType.DMA((2,)), pltpu.SemaphoreType.DMA((2,))],
    )
    return pl.pallas_call(
        kern,
        grid_spec=grid_spec,
        out_shape=jax.ShapeDtypeStruct((trash_row + 2 * MOE_BM, ROW_SUB, LANES), F32),
        compiler_params=pltpu.CompilerParams(dimension_semantics=("arbitrary",), vmem_limit_bytes=MOE_VMEM_LIMIT),
        name="moe_experts",
    )(block_e, n_used, row_tok, row_tok, row_dst_prev, row_dst, h, wgu, bgu, wd, bd)


def _combine_kernel(h_ref, y0_ref, y1_ref, y2_ref, y3_ref, gt_ref, g_ref, b_ref, x_ref, xb_ref):
    gt = gt_ref[...]
    y = gt[:, 0:1] * _load_rows(y0_ref)
    for kk, y_ref in enumerate((y1_ref, y2_ref, y3_ref), start=1):
        y = y + gt[:, kk:kk + 1] * _load_rows(y_ref)
    x = _layer_norm(ALPHA * h_ref[...] + y, g_ref[...], b_ref[...])
    x_ref[...] = x
    xb_ref[...] = x.astype(BF16)


def _combine(h, ys, gates, g, b, tm):
    t = h.shape[0]
    nt = t // tm

    def yspec(kk):
        return pl.BlockSpec((tm, ROW_SUB, LANES), lambda i: (kk * nt + i, 0, 0))

    return pl.pallas_call(
        _combine_kernel,
        grid=(nt,),
        in_specs=[pl.BlockSpec((tm, D_MODEL), lambda i: (i, 0)),
                  yspec(0), yspec(1), yspec(2), yspec(3),
                  pl.BlockSpec((tm, TOP_K), lambda i: (i, 0)),
                  pl.BlockSpec((1, D_MODEL), lambda i: (0, 0)),
                  pl.BlockSpec((1, D_MODEL), lambda i: (0, 0))],
        out_specs=[pl.BlockSpec((tm, D_MODEL), lambda i: (i, 0)),
                   pl.BlockSpec((tm, D_MODEL), lambda i: (i, 0))],
        out_shape=[jax.ShapeDtypeStruct((t, D_MODEL), F32),
                   jax.ShapeDtypeStruct((t, D_MODEL), BF16)],
        compiler_params=_cparams(("arbitrary",)),
        name="combine_ln",
    )(h, ys, ys, ys, ys, gates, g, b)


def _split_cols(a, sizes):
    out, off = [], 0
    for s in sizes:
        out.append(a[..., off:off + s])
        off += s
    return out


def _per_group(vec_heads):
    v = vec_heads.reshape(SSD_GROUPS, SSD_HPG)
    return jnp.pad(v, ((0, 0), (0, LANES - SSD_HPG))).reshape(1, DT_W)


def _prep_layer(p, l):
    wq, wf, wv, wg, wz, wxbc, wdt, wga, wgb = _split_cols(p["w_in"][l], IN_SPLITS)
    wdt = jnp.pad(wdt.reshape(D_MODEL, SSD_GROUPS, SSD_HPG), ((0, 0), (0, 0), (0, LANES - SSD_HPG)))
    w_in = jnp.concatenate([wq, wf, wv, wg, wz, wxbc, wga, wgb, wdt.reshape(D_MODEL, DT_W)], axis=1).astype(BF16)
    lb = p["lb_all"][l].reshape(HG_HEADS, 1, HG_DK)
    rw = jnp.pad(p["router_w"][l], ((0, 0), (0, LANES - N_EXPERTS)))
    rwh = rw.astype(BF16)
    rwl = (rw - rwh.astype(F32)).astype(BF16)
    return dict(
        w_in=w_in,
        la=jnp.log(lb), l1m=jnp.log1p(-lb), oml=1.0 - lb,
        hg_nw=p["hg_norm_w"][l].reshape(HG_HEADS, 1, HG_DV),
        hgp=p["hg_proj"][l].astype(BF16),
        conv_w=p["conv_w"][l], conv_b=p["conv_b"][l].reshape(1, CONV_DIM),
        dtb=_per_group(p["dt_bias"][l]),
        a=_per_group(-jnp.exp(p["a_log"][l].astype(F32))),
        dsk=jnp.repeat(p["d_skip"][l], SSD_HEADDIM).reshape(1, SSD_INNER),
        ssd_nw=p["ssd_norm_w"][l].reshape(1, SSD_INNER),
        ssp=p["ssd_proj"][l].astype(BF16),
        wo=p["w_out"][l].astype(BF16),
        ln1_g=p["ln1_g"][l].reshape(1, D_MODEL), ln1_b=p["ln1_b"][l].reshape(1, D_MODEL),
        rwh=rwh, rwl=rwl,
        rb=jnp.pad(p["router_b"][l], (0, LANES - N_EXPERTS)).reshape(1, LANES),
        ln2_g=p["ln2_g"][l].reshape(1, D_MODEL), ln2_b=p["ln2_b"][l].reshape(1, D_MODEL),
    )


def _prep_experts(p):
    return dict(
        wgu=p["w_gu"], bgu=p["b_gu"].reshape(DEPTH, N_EXPERTS, 1, 2 * D_FF),
        wd=p["w_down"], bd=p["b_down"].reshape(DEPTH, N_EXPERTS, 1, D_MODEL),
    )


def _moe(h, h3, logits, lp, ep, li, tm):
    t = h.shape[0]
    s = t * TOP_K
    top_v, top_e = lax.top_k(logits[:, :N_EXPERTS], TOP_K)
    gates = jax.nn.softmax(top_v, axis=-1)
    e_slot = top_e.reshape(s).astype(jnp.int32)
    slot_bits = max(s - 1, 1).bit_length()
    order = jnp.sort((e_slot << slot_bits) | jnp.arange(s, dtype=jnp.int32)) & ((1 << slot_bits) - 1)
    counts = jnp.sum((e_slot[:, None] == jnp.arange(N_EXPERTS, dtype=jnp.int32)[None, :]).astype(jnp.int32), axis=0)
    padded = (counts + MOE_BM - 1) // MOE_BM * MOE_BM
    pend = jnp.cumsum(padded)
    cend = pend - padded + counts
    n_blocks = (s + N_EXPERTS * (MOE_BM - 1) + MOE_BM - 1) // MOE_BM
    nrows = n_blocks * MOE_BM
    rows = jnp.arange(nrows, dtype=jnp.int32)[:, None]
    before = pend[None, :] <= rows
    pad_before = jnp.sum(jnp.where(before, padded - counts, 0), axis=1)
    is_pad = jnp.any((cend[None, :] <= rows) & (rows < pend[None, :]), axis=1) | (rows[:, 0] >= pend[-1])
    slot = order[jnp.clip(rows[:, 0] - pad_before, 0, s - 1)]
    tok = slot // TOP_K
    row_tok = jnp.where(is_pad, 0, tok)
    row_dst = jnp.where(is_pad, TOP_K * t + rows[:, 0] % (2 * MOE_BM), (slot - tok * TOP_K) * t + tok)
    blk_rows = jnp.arange(n_blocks, dtype=jnp.int32)[:, None] * MOE_BM
    block_e = jnp.minimum(jnp.sum((pend[None, :] <= blk_rows).astype(jnp.int32), axis=1), N_EXPERTS - 1)
    n_used = (pend[-1:] // MOE_BM).astype(jnp.int32)
    first_prev = TOP_K * t + MOE_BM + jnp.arange(MOE_BM, dtype=jnp.int32)
    row_dst_prev = jnp.concatenate([first_prev, row_dst[:-MOE_BM]])
    ys = _moe_experts(block_e, n_used, row_tok.reshape(n_blocks, 1, MOE_BM), row_dst.reshape(n_blocks, 1, MOE_BM),
                      row_dst_prev.reshape(n_blocks, 1, MOE_BM), h3, ep["wgu"], ep["bgu"], ep["wd"], ep["bd"], li)
    return _combine(h, ys, gates, lp["ln2_g"], lp["ln2_b"], tm)


def _mixers(xw, li, lp, grp, new):
    bsz, l, valid, base = grp["bsz"], grp["l"], grp["valid"], grp["base"]
    t = bsz * l
    prev8 = jnp.pad(grp["s_conv"][li], ((0, 0), (SUBLANES - (CONV_K - 1), 0), (0, 0)))
    xc = _conv(xw, prev8, lp["conv_w"], lp["conv_b"], bsz, l, base)
    tail = base + jnp.arange(bsz, dtype=jnp.int32)[:, None] * l + jnp.arange(valid - (CONV_K - 1), valid)[None, :]
    conv_new = xw[tail.reshape(-1)][:, OFF_XBC:OFF_XBC + CONV_DIM].reshape(bsz, CONV_K - 1, CONV_DIM)
    o, hg = _hgrn(xw, grp["s_hg"], new["hg"], li, lp["la"], lp["l1m"], lp["oml"], lp["hg_nw"], bsz, l, valid,
                  base // ROW_TILE)
    y, ssm = _ssd(xc.reshape(t // ROW_TILE, ROW_TILE, CONV_DIM), xw, grp["s_ssm"], new["ssm"], li, lp["dtb"],
                  lp["a"], lp["dsk"], lp["ssd_nw"], bsz, l, valid, base // ROW_TILE)
    return o, y, dict(hg=hg, ssm=ssm, conv=new["conv"] + [conv_new])


def _forward(x, groups, layers, ep):
    t = x.shape[0]
    xb = x.astype(BF16)
    new = [dict(hg=jnp.zeros(g["s_hg"].shape, F32), ssm=jnp.zeros(g["s_ssm"].shape, F32), conv=[]) for g in groups]
    for li, lp in enumerate(layers):
        xw = _matmul(xb, lp["w_in"], 512 if t % 512 == 0 else TOKEN_TILE, N_W // 4)
        mix = [_mixers(xw, li, lp, grp, n) for grp, n in zip(groups, new)]
        new = [m[2] for m in mix]
        h, h3, logits = _postmix(mix[0][0], mix[1][0], mix[0][1], mix[1][1], xw, x, lp["hgp"], lp["ssp"], lp["wo"],
                             lp["ln1_g"], lp["ln1_b"], lp["rwh"], lp["rwl"], lp["rb"])
        x, xb = _moe(h, h3, logits, lp, ep, li, TOKEN_TILE)
    return x, [dict(hg=n["hg"], ssm=n["ssm"], conv=jnp.stack(n["conv"])) for n in new]


def kernel(x_prompt, x_sample, state_hgrn, state_ssm, state_conv, hg_lower_bounds, w_in, hg_norm_w, hg_proj,
           conv_w, conv_b, dt_bias, a_log, d_skip, ssd_norm_w, ssd_proj, w_out, ln1_g, ln1_b, router_w,
           router_b, w_gu, b_gu, w_down, b_down, ln2_g, ln2_b):
    lb_all = jnp.cumsum(jax.nn.softmax(hg_lower_bounds.astype(F32), axis=0), axis=0)
    lb_all = lb_all - lb_all[0]
    p = dict(lb_all=lb_all, w_in=w_in, hg_norm_w=hg_norm_w, hg_proj=hg_proj, conv_w=conv_w, conv_b=conv_b,
             dt_bias=dt_bias, a_log=a_log, d_skip=d_skip, ssd_norm_w=ssd_norm_w, ssd_proj=ssd_proj, w_out=w_out,
             ln1_g=ln1_g, ln1_b=ln1_b, router_w=router_w, router_b=router_b, w_gu=w_gu, b_gu=b_gu,
             w_down=w_down, b_down=b_down, ln2_g=ln2_g, ln2_b=ln2_b)
    layers = [_prep_layer(p, l) for l in range(DEPTH)]
    ep = _prep_experts(p)

    return _run(x_prompt, x_sample, state_hgrn, state_ssm, state_conv, layers, ep)


def _run(x_prompt, x_sample, state_hgrn, state_ssm, state_conv, layers, ep):
    bp, lprompt, _ = x_prompt.shape
    bs, ls, _ = x_sample.shape
    tp = bp * lprompt

    def pairs(s_ssm):
        return s_ssm.reshape(s_ssm.shape[:2] + (SSD_PAIRS, LANES, SSD_STATE))

    groups = [
        dict(bsz=bp, l=lprompt, valid=lprompt, base=0,
             s_hg=jnp.zeros((DEPTH, bp) + state_hgrn.shape[2:], F32),
             s_ssm=pairs(jnp.zeros((DEPTH, bp) + state_ssm.shape[2:], F32)),
             s_conv=jnp.zeros((DEPTH, bp) + state_conv.shape[2:], F32)),
        dict(bsz=bs, l=SUBLANES, valid=ls, base=tp, s_hg=state_hgrn, s_ssm=pairs(state_ssm), s_conv=state_conv),
    ]
    xs = jnp.pad(x_sample, ((0, 0), (0, SUBLANES - ls), (0, 0)))
    x = jnp.concatenate([x_prompt.reshape(tp, D_MODEL), xs.reshape(bs * SUBLANES, D_MODEL)], axis=0)
    x, new = _forward(x, groups, layers, ep)
    y_p = x[:tp].reshape(bp, lprompt, D_MODEL)
    y_s = x[tp:].reshape(bs, SUBLANES, D_MODEL)[:, :ls]
    (n_p, n_s) = new
    return (y_p, y_s, n_p["hg"], n_p["ssm"].reshape((DEPTH, bp) + state_ssm.shape[2:]), n_p["conv"],
            n_s["hg"], n_s["ssm"].reshape(state_ssm.shape), n_s["conv"])
```

```python
import functools

import jax
import jax.numpy as jnp
from jax import lax
from jax.experimental import pallas as pl
from jax.experimental.pallas import tpu as pltpu

F32 = jnp.float32
BF16 = jnp.bfloat16

D_MODEL = 1024
DEPTH = 2
HG_HEADS = 8
HG_DK = 128
HG_DV = 128
HG_DIM = HG_HEADS * HG_DK
SSD_INNER = 2 * D_MODEL
SSD_HEADDIM = 64
SSD_HEADS = SSD_INNER // SSD_HEADDIM
SSD_GROUPS = 4
SSD_HPG = SSD_HEADS // SSD_GROUPS
SSD_STATE = 128
SSD_PAIRS = SSD_HEADS // 2
PAIRS_PER_GROUP = SSD_PAIRS // SSD_GROUPS
GROUP_W = SSD_INNER // SSD_GROUPS
CONV_K = 4
CONV_DIM = SSD_INNER + 2 * SSD_GROUPS * SSD_STATE
N_EXPERTS = 32
TOP_K = 4
D_FF = D_MODEL
SWIGLU_LIMIT = 7.0
SWIGLU_ALPHA = 1.702
ALPHA = (2.0 * DEPTH) ** 0.25
LN_EPS = 1e-5
RMS_EPS = 1e-6

LANES = 128
SUBLANES = 8
ROW_SUB = D_MODEL // LANES

DT_W = SSD_GROUPS * LANES
OFF_Q = 0
OFF_F = HG_DIM
OFF_V = 2 * HG_DIM
OFF_G = 3 * HG_DIM
OFF_Z = 4 * HG_DIM
OFF_XBC = OFF_Z + SSD_INNER
OFF_GA = OFF_XBC + CONV_DIM
OFF_GB = OFF_GA + D_MODEL
OFF_DT = OFF_GB + D_MODEL
N_W = OFF_DT + DT_W
IN_SPLITS = (HG_DIM, HG_DIM, HG_DIM, HG_DIM, SSD_INNER, CONV_DIM, SSD_HEADS, D_MODEL, D_MODEL)

ROW_TILE = 128
CONV_ROWS = 256
TOKEN_TILE = 256
IN_PROJ_ROWS = 1024
MOE_BM = 256
MOE_SCATTER_SPLIT = 128
VMEM_LIMIT = 48 * 1024 * 1024
MOE_VMEM_LIMIT = 56 * 1024 * 1024
LOG2E = 1.4426950408889634
HALO = 16
HGRN_HEADS_PER_STEP = 8
HGRN_SUB = 8
SSD_GROUPS_PER_STEP = 4


def _cparams(sem):
    return pltpu.CompilerParams(dimension_semantics=sem, vmem_limit_bytes=VMEM_LIMIT)


def _sigmoid(x):
    return 0.5 * jnp.tanh(0.5 * x) + 0.5


def _log1p(x):
    return jnp.log(1.0 + x)


def _dot(a, b):
    return jnp.dot(a, b, preferred_element_type=F32)


def _dot_nt(a, b):
    return lax.dot_general(a, b, (((1,), (1,)), ((), ())), preferred_element_type=F32)


def _dot_tn(a, b):
    return lax.dot_general(a, b, (((0,), (0,)), ((), ())), preferred_element_type=F32)


def _load_rows(ref3):
    return pltpu.einshape("tsl->t(sl)", ref3[...])


def _store_rows(ref3, x):
    ref3[...] = pltpu.einshape("t(sl)->tsl", x, s=ROW_SUB)


def _mm_kernel(x_ref, w_ref, o_ref):
    o_ref[...] = _dot(x_ref[...], w_ref[...])


def _matmul(x, w, tm, tn):
    m, k = x.shape
    n = w.shape[1]
    return pl.pallas_call(
        _mm_kernel,
        grid=(n // tn, m // tm),
        in_specs=[pl.BlockSpec((tm, k), lambda j, i: (i, 0)),
                  pl.BlockSpec((k, tn), lambda j, i: (0, j))],
        out_specs=pl.BlockSpec((tm, tn), lambda j, i: (i, j)),
        out_shape=jax.ShapeDtypeStruct((m, n), F32),
        compiler_params=_cparams(("arbitrary", "arbitrary")),
        name="in_proj",
    )(x, w)


def _conv_kernel(x_ref, pb_ref, p0_ref, w_ref, b_ref, o_ref):
    i = pl.program_id(1)
    x = x_ref[...]
    prev = jnp.where(i == 0, p0_ref[0], pb_ref[...])
    full = jnp.concatenate([prev, x], axis=0)
    w = w_ref[...]
    acc = b_ref[...] + w[CONV_K - 1:CONV_K] * x
    for s in range(1, CONV_K):
        xs = pltpu.roll(full, s, 0)[SUBLANES:]
        acc = acc + w[CONV_K - 1 - s:CONV_K - s] * xs
    o_ref[...] = acc * _sigmoid(acc)


def _conv_short_kernel(x_ref, p_ref, w_ref, b_ref, o_ref):
    x = x_ref[...]
    p = p_ref[...]
    rc = lax.broadcasted_iota(jnp.int32, x.shape, 0) & (SUBLANES - 1)
    w = w_ref[...]
    acc = b_ref[...] + w[CONV_K - 1:CONV_K] * x
    for s in range(1, CONV_K):
        xs = jnp.where(rc >= s, pltpu.roll(x, s, 0), pltpu.roll(p, (s - SUBLANES) % ROW_TILE, 0))
        acc = acc + w[CONV_K - 1 - s:CONV_K - s] * xs
    o_ref[...] = acc * _sigmoid(acc)


def _conv(xw, prev8, conv_w, conv_b, bsz, l, base_rows):
    cb = OFF_XBC // CONV_DIM
    wspec = [pl.BlockSpec((CONV_K, CONV_DIM), lambda *_: (0, 0)), pl.BlockSpec((1, CONV_DIM), lambda *_: (0, 0))]
    out_shape = jax.ShapeDtypeStruct((bsz * l, CONV_DIM), F32)
    if l == SUBLANES:
        r = ROW_TILE
        return pl.pallas_call(
            _conv_short_kernel,
            grid=(bsz * l // r,),
            in_specs=[pl.BlockSpec((r, CONV_DIM), lambda i: (base_rows // r + i, cb)),
                      pl.BlockSpec((r, CONV_DIM), lambda i: (i, 0))] + wspec,
            out_specs=pl.BlockSpec((r, CONV_DIM), lambda i: (i, 0)),
            out_shape=out_shape,
            compiler_params=_cparams(("arbitrary",)),
            name="conv_silu_short",
        )(xw, prev8.reshape(bsz * SUBLANES, CONV_DIM), conv_w, conv_b)
    r = CONV_ROWS
    tps = l // r
    rb = r // SUBLANES
    return pl.pallas_call(
        _conv_kernel,
        grid=(bsz, tps),
        in_specs=[pl.BlockSpec((r, CONV_DIM), lambda bi, i: (base_rows // r + bi * tps + i, cb)),
                  pl.BlockSpec((SUBLANES, CONV_DIM),
                               lambda bi, i: (jnp.maximum(base_rows // SUBLANES + (bi * tps + i) * rb - 1, 0), cb)),
                  pl.BlockSpec((1, SUBLANES, CONV_DIM), lambda bi, i: (bi, 0, 0))] + wspec,
        out_specs=pl.BlockSpec((r, CONV_DIM), lambda bi, i: (bi * tps + i, 0)),
        out_shape=out_shape,
        compiler_params=_cparams(("arbitrary", "arbitrary")),
        name="conv_silu",
    )(xw, xw, prev8, conv_w, conv_b)


def _pad_rows(x, rows):
    if x.shape[0] == rows:
        return x
    return jnp.concatenate([x, jnp.zeros((rows - x.shape[0], x.shape[1]), x.dtype)], axis=0)


def _split3(x):
    hi = x.astype(BF16)
    r1 = x - hi.astype(F32)
    mid = r1.astype(BF16)
    lo = (r1 - mid.astype(F32)).astype(BF16)
    return hi, mid, lo


def _hgrn_kernel(*refs, hps, nseq, nc, **kw):
    refs = refs[:9] + refs[10:]
    s0_ref, sout_ref, st_ref = refs[8], refs[10], refs[11]
    c = pl.program_id(2)

    @pl.when(c == 0)
    def _():
        for hh in range(hps):
            for i in range(nseq):
                st_ref[hh, i] = s0_ref[0, i, hh].T

    for hh in range(hps):
        _hgrn_head(hh, *refs, nseq=nseq, **kw)

    @pl.when(c == nc - 1)
    def _():
        for hh in range(hps):
            for i in range(nseq):
                sout_ref[0, i, hh] = st_ref[hh, i].T


def _hgrn_head(hh, q_ref, f_ref, v_ref, g_ref, la_ref, l1m_ref, oml_ref, nw_ref, s0_ref,
               o_ref, sout_ref, st_ref, k_scr, b_scr, *, ch, sub, nseq, valid):
    r = ROW_TILE
    cols = slice(hh * LANES, (hh + 1) * LANES)
    q = q_ref[:, cols]
    fr = f_ref[:, cols]
    v = v_ref[:, cols]
    q = q * _sigmoid(q)
    la = la_ref[hh]
    lsig = jnp.minimum(fr, 0.0) - _log1p(jnp.exp(-jnp.abs(fr)))
    cc = l1m_ref[hh] + lsig
    logf = jnp.maximum(la, cc) + _log1p(jnp.exp(-jnp.abs(la - cc)))
    k = oml_ref[hh] * _sigmoid(-fr)

    row = lax.broadcasted_iota(jnp.int32, (r, LANES), 0)
    lane = lax.broadcasted_iota(jnp.int32, (r, LANES), 1)
    if valid < ch:
        pad = (row & (ch - 1)) >= valid
        logf = jnp.where(pad, 0.0, logf)
        k = jnp.where(pad, 0.0, k)

    tri = lane <= row
    if ch < r:
        shift = ch.bit_length() - 1
        tri = tri & ((row >> shift) == (lane >> shift))
    tri = jnp.where(tri, 1.0, 0.0).astype(BF16)
    hi, mid, lo = _split3(logf)
    b2 = (_dot(tri, hi) + _dot(tri, mid) + _dot(tri, lo)) * LOG2E

    k_scr[hh, 0:HALO] = jnp.zeros((HALO, LANES), F32)
    b_scr[hh, 0:HALO] = jnp.zeros((HALO, LANES), F32)
    k_scr[hh, HALO:HALO + r] = k
    b_scr[hh, HALO:HALO + r] = b2
    rs = row & (sub - 1)
    scl = jnp.where(lane == rs, jnp.sum(q * k, axis=-1, keepdims=True), 0.0)
    for d in range(1, sub):
        kd = k_scr[hh, pl.ds(HALO - d, r), :]
        bd = b_scr[hh, pl.ds(HALO - d, r), :]
        sd = jnp.sum(q * kd * jnp.exp2(b2 - bd), axis=-1, keepdims=True)
        scl = jnp.where(lane == rs - d, sd, scl)

    spc = ch // sub
    pieces = []
    for m in range(r // sub):
        lo_, hi_ = m * sub, (m + 1) * sub
        piece = scl[lo_:hi_]
        if m:
            piece = pltpu.roll(piece, lo_, 1)
        j = m % spc
        if j:
            cs = (m - j) * sub
            ref_b = b2[lo_ - 1:lo_]
            qj = q[lo_:hi_] * jnp.exp2(b2[lo_:hi_] - ref_b)
            kk = k[cs:lo_] * jnp.exp2(ref_b - b2[cs:lo_])
            parts = ([jnp.zeros((cs, LANES), F32)] if cs else []) + [kk, jnp.zeros((r - lo_, LANES), F32)]
            piece = piece + _dot_nt(qj.astype(BF16), jnp.concatenate(parts, axis=0).astype(BF16))
        pieces.append(piece)
    sc = jnp.concatenate(pieces, axis=0)
    o = _dot(sc.astype(BF16), v.astype(BF16))

    inter = []
    for i in range(nseq):
        lo_, hi_ = i * ch, (i + 1) * ch
        bi = b2[lo_:hi_]
        bl = bi[ch - 1:ch]
        st = st_ref[hh, i]
        inter.append(_dot_nt((q[lo_:hi_] * jnp.exp2(bi)).astype(BF16), st.astype(BF16)))
        kh = _pad_rows(k[lo_:hi_] * jnp.exp2(bl - bi), r)
        vi = _pad_rows(v[lo_:hi_], r)
        st_ref[hh, i] = st * jnp.exp2(bl) + _dot_tn(vi.astype(BF16), kh.astype(BF16))
    o = o + (inter[0] if nseq == 1 else jnp.concatenate(inter, axis=0))

    ms = jnp.mean(o * o, axis=-1, keepdims=True)
    o_ref[:, cols] = o * lax.rsqrt(ms + RMS_EPS) * nw_ref[hh] * _sigmoid(g_ref[:, cols])


def _seq_tiling(bsz, l):
    r = ROW_TILE
    ch = min(l, r)
    nseq = r // ch
    nc = max(l // r, 1)
    nb = bsz * l // (r * nc)
    return ch, nseq, nc, nb


def _carry_alias(new_all, n_in):
    return [pl.BlockSpec(memory_space=pl.ANY)], [new_all], {n_in: 1}


def _hgrn(xw, s0_all, new_all, li, la, l1m, oml, nw, bsz, l, valid, base_tile):
    r = ROW_TILE
    ch, nseq, nc, nb = _seq_tiling(bsz, l)
    sub = min(ch, HGRN_SUB)
    hps = HGRN_HEADS_PER_STEP
    w = hps * LANES
    extra_specs, extra_args, aliases = _carry_alias(new_all, 9)
    kern = functools.partial(_hgrn_kernel, hps=hps, ch=ch, sub=sub, nseq=nseq, valid=valid, nc=nc)

    def xspec(off):
        return pl.BlockSpec((r, w), lambda bb, h, c: (base_tile + bb * nc + c, off // w + h))

    def pspec():
        return pl.BlockSpec((hps, 1, LANES), lambda bb, h, c: (h, 0, 0))

    return pl.pallas_call(
        kern,
        grid=(nb, HG_HEADS // hps, nc),
        in_specs=[xspec(OFF_Q), xspec(OFF_F), xspec(OFF_V), xspec(OFF_G),
                  pspec(), pspec(), pspec(), pspec(),
                  pl.BlockSpec((1, nseq, hps, HG_DK, HG_DV), lambda bb, h, c: (li, bb, h, 0, 0))] + extra_specs,
        out_specs=[pl.BlockSpec((r, w), lambda bb, h, c: (bb * nc + c, h)),
                   pl.BlockSpec((1, nseq, hps, HG_DK, HG_DV), lambda bb, h, c: (li, bb, h, 0, 0))],
        out_shape=[jax.ShapeDtypeStruct((bsz * l, HG_DIM), F32),
                   jax.ShapeDtypeStruct(s0_all.shape, F32)],
        scratch_shapes=[pltpu.VMEM((hps, nseq, HG_DV, HG_DK), F32),
                        pltpu.VMEM((hps, HALO + r, LANES), F32), pltpu.VMEM((hps, HALO + r, LANES), F32)],
        input_output_aliases=aliases,
        compiler_params=_cparams(("arbitrary", "arbitrary", "arbitrary")),
        name="hgrn2",
    )(xw, xw, xw, xw, la, l1m, oml, nw, s0_all, *extra_args)


def _ssd_kernel(*refs, **kw):
    _ssd_body(*(refs[:10] + refs[11:]), **kw)


def _ssd_body(xs_ref, bm_ref, cm_ref, z_ref, dt_ref, dtb_ref, a_ref, dsk_ref, nw_ref, h0_ref,
              y_ref, hout_ref, h_ref, al_scr, dl_scr, *, gps, nc, **kw):
    c = pl.program_id(2)

    @pl.when(c == 0)
    def _():
        h_ref[...] = h0_ref[0]

    for gg in range(gps):
        _ssd_group(gg, xs_ref, bm_ref, cm_ref, z_ref, dt_ref, dtb_ref, a_ref, dsk_ref, nw_ref, y_ref, h_ref,
                   al_scr, dl_scr, **kw)

    @pl.when(c == nc - 1)
    def _():
        hout_ref[0] = h_ref[...]


def _ssd_group(gg, xs_ref, bm_ref, cm_ref, z_ref, dt_ref, dtb_ref, a_ref, dsk_ref, nw_ref, y_ref, h_ref,
               al_scr, dl_scr, *, ch, nseq, valid):
    r = ROW_TILE
    scol = slice(gg * LANES, (gg + 1) * LANES)
    goff = gg * GROUP_W
    poff = gg * PAIRS_PER_GROUP
    shift = ch.bit_length() - 1
    rowl = lax.broadcasted_iota(jnp.int32, (r, LANES), 0)
    lane = lax.broadcasted_iota(jnp.int32, (r, LANES), 1)
    rc = rowl & (ch - 1)
    dtr = dt_ref[:, scol] + dtb_ref[:, scol]
    dt = jnp.maximum(dtr, 0.0) + _log1p(jnp.exp(-jnp.abs(dtr)))
    if valid < ch:
        dt = jnp.where(rc < valid, dt, 0.0)
    mask = (lane <= rowl) & ((rowl >> shift) == (lane >> shift))
    tri = jnp.where(mask, 1.0, 0.0).astype(BF16)
    hi, mid, lo = _split3(dt * a_ref[:, scol])
    acum = _dot(tri, hi) + _dot(tri, mid) + _dot(tri, lo)
    acum_t = acum.T
    dt_t = dt.T

    even_lane = lane < SSD_HEADDIM
    even_row = rowl < SSD_HEADDIM
    last_row = rc == ch - 1

    bmb = bm_ref[0, :, scol].astype(BF16)
    cmb = cm_ref[0, :, scol].astype(BF16)
    cb = _dot_nt(cmb, bmb)
    xs = xs_ref[0, :, goff:goff + GROUP_W]
    z = z_ref[:, goff:goff + GROUP_W]
    for pp in range(PAIRS_PER_GROUP):
        xp = xs[:, pp * LANES:(pp + 1) * LANES]
        acc = dsk_ref[:, goff + pp * LANES:goff + (pp + 1) * LANES] * xp
        for e in range(2):
            hd = 2 * pp + e
            w = cb * jnp.exp(jnp.where(mask, acum[:, hd:hd + 1] - acum_t[hd:hd + 1, :], -jnp.inf)) * dt_t[hd:hd + 1, :]
            xm = jnp.where(even_lane if e == 0 else jnp.logical_not(even_lane), xp, 0.0)
            acc = acc + _dot(w.astype(BF16), xm.astype(BF16))
        a_lane = jnp.where(even_lane, acum[:, 2 * pp:2 * pp + 1], acum[:, 2 * pp + 1:2 * pp + 2])
        dt_lane = jnp.where(even_lane, dt[:, 2 * pp:2 * pp + 1], dt[:, 2 * pp + 1:2 * pp + 2])
        ea = jnp.exp(a_lane)

        def seq_body(i, acc, pp=pp, xp=xp, a_lane=a_lane, dt_lane=dt_lane, ea=ea):
            in_seq = (rowl >> shift) == i
            al_lane = jnp.sum(jnp.where(in_seq & last_row, a_lane, 0.0), axis=0, keepdims=True)
            hp = h_ref[i, poff + pp]
            yi = _dot_nt(cmb, hp.astype(BF16)) * ea
            acc = acc + jnp.where(in_seq, yi, 0.0)
            xw = jnp.where(in_seq, xp * (jnp.exp(al_lane - a_lane) * dt_lane), 0.0)
            dec = jnp.where(even_row, jnp.exp(al_lane[:, 0:1]), jnp.exp(al_lane[:, LANES - 1:LANES]))
            h_ref[i, poff + pp] = dec * hp + _dot_tn(xw.astype(BF16), bmb)
            return acc

        if nseq == 1:
            acc = seq_body(0, acc)
        else:
            al_scr[pp] = a_lane
            dl_scr[pp] = dt_lane
        y_ref[:, goff + pp * LANES:goff + (pp + 1) * LANES] = acc

    if nseq > 1:
        def short_body(i, carry):
            rows = pl.ds(pl.multiple_of(i * ch, ch), ch)
            cm_i = _pad_rows(cm_ref[0, rows, scol], 2 * SUBLANES).astype(BF16)
            bm_i = _pad_rows(bm_ref[0, rows, scol], 2 * SUBLANES).astype(BF16)
            for pp in range(PAIRS_PER_GROUP):
                cols = slice(goff + pp * LANES, goff + (pp + 1) * LANES)
                a_i = al_scr[pp, rows, :]
                al = a_i[ch - 1:ch]
                hp = h_ref[i, poff + pp]
                yi = _dot_nt(cm_i, hp.astype(BF16))[:ch] * jnp.exp(a_i)
                y_ref[rows, cols] = y_ref[rows, cols] + yi
                xw = xs_ref[0, rows, cols] * (jnp.exp(al - a_i) * dl_scr[pp, rows, :])
                dec = jnp.where(even_row, jnp.exp(al[:, 0:1]), jnp.exp(al[:, LANES - 1:LANES]))
                h_ref[i, poff + pp] = dec * hp + _dot_tn(_pad_rows(xw, 2 * SUBLANES).astype(BF16), bm_i)
            return carry
        lax.fori_loop(0, nseq, short_body, 0, unroll=4)

    y = y_ref[:, goff:goff + GROUP_W] * (z * _sigmoid(z))
    ms = jnp.mean(y * y, axis=-1, keepdims=True)
    y_ref[:, goff:goff + GROUP_W] = y * lax.rsqrt(ms + RMS_EPS) * nw_ref[:, goff:goff + GROUP_W]


def _ssd(xc3, xw, h0_all, new_all, li, dtb, a, dsk, nw, bsz, l, valid, base_tile):
    r = ROW_TILE
    ch, nseq, nc, nb = _seq_tiling(bsz, l)
    gps = SSD_GROUPS_PER_STEP if nseq == 1 else 1
    gw, sw, pairs = gps * GROUP_W, gps * SSD_STATE, gps * PAIRS_PER_GROUP
    extra_specs, extra_args, aliases = _carry_alias(new_all, 10)
    kern = functools.partial(_ssd_kernel, gps=gps, ch=ch, nseq=nseq, valid=valid, nc=nc)
    hspec = pl.BlockSpec((1, nseq, pairs, LANES, SSD_STATE), lambda bb, g, c: (li, bb, g, 0, 0))
    b_off = SSD_INNER // sw
    c_off = b_off + SSD_GROUPS // gps

    def vec(w):
        return pl.BlockSpec((1, w), lambda bb, g, c: (0, g))

    return pl.pallas_call(
        kern,
        grid=(nb, SSD_GROUPS // gps, nc),
        in_specs=[pl.BlockSpec((1, r, gw), lambda bb, g, c: (bb * nc + c, 0, g)),
                  pl.BlockSpec((1, r, sw), lambda bb, g, c: (bb * nc + c, 0, b_off + g)),
                  pl.BlockSpec((1, r, sw), lambda bb, g, c: (bb * nc + c, 0, c_off + g)),
                  pl.BlockSpec((r, gw), lambda bb, g, c: (base_tile + bb * nc + c, OFF_Z // gw + g)),
                  pl.BlockSpec((r, sw), lambda bb, g, c: (base_tile + bb * nc + c, OFF_DT // sw + g)),
                  vec(sw), vec(sw), vec(gw), vec(gw), hspec] + extra_specs,
        out_specs=[pl.BlockSpec((r, gw), lambda bb, g, c: (bb * nc + c, g)), hspec],
        out_shape=[jax.ShapeDtypeStruct((bsz * l, SSD_INNER), F32),
                   jax.ShapeDtypeStruct(h0_all.shape, F32)],
        scratch_shapes=[pltpu.VMEM((nseq, pairs, LANES, SSD_STATE), F32),
                        pltpu.VMEM((PAIRS_PER_GROUP, r, LANES), F32), pltpu.VMEM((PAIRS_PER_GROUP, r, LANES), F32)],
        input_output_aliases=aliases,
        compiler_params=_cparams(("arbitrary", "arbitrary", "arbitrary")),
        name="ssd",
    )(xc3, xc3, xc3, xw, xw, dtb, a, dsk, nw, h0_all, *extra_args)


def _layer_norm(x, g, b):
    mu = jnp.mean(x, axis=-1, keepdims=True)
    xc = x - mu
    var = jnp.mean(xc * xc, axis=-1, keepdims=True)
    return xc * lax.rsqrt(var + LN_EPS) * g + b


def _postmix_kernel(op_ref, os_ref, yp_ref, ys_ref, ga_ref, gb_ref, x_ref, hgp_ref, ssp_ref, wo_ref, g_ref, b_ref,
                    rwh_ref, rwl_ref, rb_ref, h_ref, h3_ref, lg_ref, *, n_p):
    first = pl.program_id(0) < n_p
    o = jnp.where(first, op_ref[...], os_ref[...])
    y = jnp.where(first, yp_ref[...], ys_ref[...])
    out_a = _dot(o.astype(BF16), hgp_ref[...])
    out_b = _dot(y.astype(BF16), ssp_ref[...])
    merged = _sigmoid(ga_ref[...]) * out_a + _sigmoid(gb_ref[...]) * out_b
    mix = _dot(merged.astype(BF16), wo_ref[...])
    h = _layer_norm(ALPHA * x_ref[...] + mix, g_ref[...], b_ref[...])
    h_ref[...] = h
    _store_rows(h3_ref, h)
    hh = h.astype(BF16)
    hl = (h - hh.astype(F32)).astype(BF16)
    lg_ref[...] = (_dot(hh, rwh_ref[...]) + _dot(hl, rwh_ref[...]) + _dot(hh, rwl_ref[...])) + rb_ref[...]


def _postmix(o_p, o_s, y_p, y_s, xw, x, hgp, ssp, wo, g, b, rwh, rwl, rb):
    t = x.shape[0]
    tm = TOKEN_TILE
    n_p = o_p.shape[0] // tm

    def full(a):
        return pl.BlockSpec(a.shape, lambda i: (0, 0))

    def first(w):
        return pl.BlockSpec((tm, w), lambda i: (jnp.minimum(i, n_p - 1), 0))

    def second(w):
        return pl.BlockSpec((tm, w), lambda i: (jnp.maximum(i - n_p, 0), 0))

    return pl.pallas_call(
        functools.partial(_postmix_kernel, n_p=n_p),
        grid=(t // tm,),
        in_specs=[first(HG_DIM), second(HG_DIM), first(SSD_INNER), second(SSD_INNER),
                  pl.BlockSpec((tm, D_MODEL), lambda i: (i, OFF_GA // D_MODEL)),
                  pl.BlockSpec((tm, D_MODEL), lambda i: (i, OFF_GB // D_MODEL)),
                  pl.BlockSpec((tm, D_MODEL), lambda i: (i, 0)),
                  full(hgp), full(ssp), full(wo), full(g), full(b), full(rwh), full(rwl), full(rb)],
        out_specs=[pl.BlockSpec((tm, D_MODEL), lambda i: (i, 0)),
                   pl.BlockSpec((tm, ROW_SUB, LANES), lambda i: (i, 0, 0)),
                   pl.BlockSpec((tm, LANES), lambda i: (i, 0))],
        out_shape=[jax.ShapeDtypeStruct((t, D_MODEL), F32),
                   jax.ShapeDtypeStruct((t, ROW_SUB, LANES), F32),
                   jax.ShapeDtypeStruct((t, LANES), F32)],
        compiler_params=_cparams(("arbitrary",)),
        name="postmix",
    )(o_p, o_s, y_p, y_s, xw, xw, x, hgp, ssp, wo, g, b, rwh, rwl, rb)


def _moe_kernel(be_ref, nu_ref, tokc_ref, tokn_ref, dstp_ref, dstc_ref, h_hbm, wgu_ref, bgu_ref, wd_ref, bd_ref,
                ys_hbm, xbuf, obuf, xb_scr, act_scr, gate_scr, wgu_b, wd_b, gsem, ssem, *, nblk, trash_row):
    j = pl.program_id(0)
    nu = nu_ref[0]
    slot = j & 1
    nslot = 1 - slot

    def gather_start(tok_ref, s, i):
        pltpu.make_async_copy(h_hbm.at[pl.ds(tok_ref[0, 0, i], 1)], xbuf.at[s, pl.ds(i, 1)], gsem.at[s]).start()

    def scatter_start(dst_ref, s, i):
        pltpu.make_async_copy(obuf.at[s, pl.ds(i, 1)], ys_hbm.at[pl.ds(dst_ref[0, 0, i], 1)], ssem.at[s]).start()

    def gather_wait(s):
        pltpu.make_async_copy(h_hbm.at[pl.ds(0, MOE_BM)], xbuf.at[s], gsem.at[s]).wait()

    def scatter_wait(s):
        pltpu.make_async_copy(obuf.at[s], ys_hbm.at[pl.ds(0, MOE_BM)], ssem.at[s]).wait()

    @pl.when(j == 0)
    def _():
        obuf[...] = jnp.zeros(obuf.shape, F32)
        pltpu.make_async_copy(obuf.at[0], ys_hbm.at[pl.ds(trash_row, MOE_BM)], ssem.at[0]).start()

        def body(i, carry):
            gather_start(tokc_ref, 0, i)
            return carry
        lax.fori_loop(0, MOE_BM, body, 0, unroll=8)

    first_of_expert = (j == 0) | (be_ref[j] != be_ref[jnp.maximum(j - 1, 0)])

    @pl.when((j < nu) & first_of_expert)
    def _():
        rows = D_MODEL // 8
        for c in range(8):
            wgu_b[c * rows:(c + 1) * rows] = wgu_ref[0, 0, c * rows:(c + 1) * rows].astype(BF16)
            wd_b[c * rows:(c + 1) * rows] = wd_ref[0, 0, c * rows:(c + 1) * rows].astype(BF16)

    @pl.when(j < nu)
    def _():
        gather_wait(slot)
        scatter_wait(slot)
        xb_scr[...] = _load_rows(xbuf.at[slot]).astype(BF16)
        for i in range(MOE_BM):
            gather_start(tokn_ref, nslot, i)
        gate = _dot(xb_scr[...], wgu_b[:, :D_FF]) + bgu_ref[0, 0, :, :D_FF]
        gate_scr[...] = jnp.minimum(gate, SWIGLU_LIMIT)

    run = (j < nu).astype(jnp.int32)

    def up_proj(_, carry):
        for i in range(MOE_SCATTER_SPLIT):
            scatter_start(dstp_ref, nslot, i)
        up = _dot(xb_scr[...], wgu_b[:, D_FF:]) + bgu_ref[0, 0, :, D_FF:]
        up = jnp.clip(up, -SWIGLU_LIMIT, SWIGLU_LIMIT)
        gate = gate_scr[...]
        act_scr[...] = ((up + 1.0) * gate * _sigmoid(SWIGLU_ALPHA * gate)).astype(BF16)
        return carry
    lax.fori_loop(0, run, up_proj, 0)

    def down(_, carry):
        for i in range(MOE_SCATTER_SPLIT, MOE_BM):
            scatter_start(dstp_ref, nslot, i)
        _store_rows(obuf.at[slot], _dot(act_scr[...], wd_b[...]) + bd_ref[0, 0])
        return carry
    lax.fori_loop(0, run, down, 0)

    @pl.when(j == nu - 1)
    def _():
        def body(i, carry):
            scatter_start(dstc_ref, slot, i)
            return carry
        lax.fori_loop(0, MOE_BM, body, 0, unroll=8)

    @pl.when(j == nblk - 1)
    def _():
        scatter_wait((nu - 1) & 1)
        scatter_wait(nu & 1)
        gather_wait(nu & 1)


def _moe_experts(block_e, n_used, row_tok, row_dst, row_dst_prev, h, wgu, bgu, wd, bd, li):
    t = h.shape[0]
    nblk = row_tok.shape[0]
    trash_row = TOP_K * t
    kern = functools.partial(_moe_kernel, nblk=nblk, trash_row=trash_row)

    def smem(imap):
        return pl.BlockSpec((1, 1, MOE_BM), imap, memory_space=pltpu.SMEM)

    grid_spec = pltpu.PrefetchScalarGridSpec(
        num_scalar_prefetch=2,
        grid=(nblk,),
        in_specs=[smem(lambda j, be, nu: (j, 0, 0)),
                  smem(lambda j, be, nu: (jnp.minimum(j + 1, nblk - 1), 0, 0)),
                  smem(lambda j, be, nu: (j, 0, 0)),
                  smem(lambda j, be, nu: (j, 0, 0)),
                  pl.BlockSpec(memory_space=pl.ANY),
                  pl.BlockSpec((1, 1, D_MODEL, 2 * D_FF), lambda j, be, nu: (li, be[j], 0, 0)),
                  pl.BlockSpec((1, 1, 1, 2 * D_FF), lambda j, be, nu: (li, be[j], 0, 0)),
                  pl.BlockSpec((1, 1, D_FF, D_MODEL), lambda j, be, nu: (li, be[j], 0, 0)),
                  pl.BlockSpec((1, 1, 1, D_MODEL), lambda j, be, nu: (li, be[j], 0, 0))],
        out_specs=pl.BlockSpec(memory_space=pl.ANY),
        scratch_shapes=[pltpu.VMEM((2, MOE_BM, ROW_SUB, LANES), F32), pltpu.VMEM((2, MOE_BM, ROW_SUB, LANES), F32),
                        pltpu.VMEM((MOE_BM, D_MODEL), BF16), pltpu.VMEM((MOE_BM, D_FF), BF16),
                        pltpu.VMEM((MOE_BM, D_FF), F32),
                        pltpu.VMEM((D_MODEL, 2 * D_FF), BF16), pltpu.VMEM((D_FF, D_MODEL), BF16),
                        pltpu.SemaphoreType.DMA((2,)), pltpu.SemaphoreType.DMA((2,))],
    )
    return pl.pallas_call(
        kern,
        grid_spec=grid_spec,
        out_shape=jax.ShapeDtypeStruct((trash_row + 2 * MOE_BM, ROW_SUB, LANES), F32),
        compiler_params=pltpu.CompilerParams(dimension_semantics=("arbitrary",), vmem_limit_bytes=MOE_VMEM_LIMIT),
        name="moe_experts",
    )(block_e, n_used, row_tok, row_tok, row_dst_prev, row_dst, h, wgu, bgu, wd, bd)


def _combine_kernel(h_ref, y0_ref, y1_ref, y2_ref, y3_ref, gt_ref, g_ref, b_ref, x_ref, xb_ref):
    gt = gt_ref[...]
    y = gt[:, 0:1] * _load_rows(y0_ref)
    for kk, y_ref in enumerate((y1_ref, y2_ref, y3_ref), start=1):
        y = y + gt[:, kk:kk + 1] * _load_rows(y_ref)
    x = _layer_norm(ALPHA * h_ref[...] + y, g_ref[...], b_ref[...])
    x_ref[...] = x
    xb_ref[...] = x.astype(BF16)


def _combine(h, ys, gates, g, b, tm):
    t = h.shape[0]
    nt = t // tm

    def yspec(kk):
        return pl.BlockSpec((tm, ROW_SUB, LANES), lambda i: (kk * nt + i, 0, 0))

    return pl.pallas_call(
        _combine_kernel,
        grid=(nt,),
        in_specs=[pl.BlockSpec((tm, D_MODEL), lambda i: (i, 0)),
                  yspec(0), yspec(1), yspec(2), yspec(3),
                  pl.BlockSpec((tm, TOP_K), lambda i: (i, 0)),
                  pl.BlockSpec((1, D_MODEL), lambda i: (0, 0)),
                  pl.BlockSpec((1, D_MODEL), lambda i: (0, 0))],
        out_specs=[pl.BlockSpec((tm, D_MODEL), lambda i: (i, 0)),
                   pl.BlockSpec((tm, D_MODEL), lambda i: (i, 0))],
        out_shape=[jax.ShapeDtypeStruct((t, D_MODEL), F32),
                   jax.ShapeDtypeStruct((t, D_MODEL), BF16)],
        compiler_params=_cparams(("arbitrary",)),
        name="combine_ln",
    )(h, ys, ys, ys, ys, gates, g, b)


def _split_cols(a, sizes):
    out, off = [], 0
    for s in sizes:
        out.append(a[..., off:off + s])
        off += s
    return out


def _per_group(vec_heads):
    v = vec_heads.reshape(SSD_GROUPS, SSD_HPG)
    return jnp.pad(v, ((0, 0), (0, LANES - SSD_HPG))).reshape(1, DT_W)


def _prep_layer(p, l):
    wq, wf, wv, wg, wz, wxbc, wdt, wga, wgb = _split_cols(p["w_in"][l], IN_SPLITS)
    wdt = jnp.pad(wdt.reshape(D_MODEL, SSD_GROUPS, SSD_HPG), ((0, 0), (0, 0), (0, LANES - SSD_HPG)))
    w_in = jnp.concatenate([wq, wf, wv, wg, wz, wxbc, wga, wgb, wdt.reshape(D_MODEL, DT_W)], axis=1).astype(BF16)
    lb = p["lb_all"][l].reshape(HG_HEADS, 1, HG_DK)
    rw = jnp.pad(p["router_w"][l], ((0, 0), (0, LANES - N_EXPERTS)))
    rwh = rw.astype(BF16)
    rwl = (rw - rwh.astype(F32)).astype(BF16)
    return dict(
        w_in=w_in,
        la=jnp.log(lb), l1m=jnp.log1p(-lb), oml=1.0 - lb,
        hg_nw=p["hg_norm_w"][l].reshape(HG_HEADS, 1, HG_DV),
        hgp=p["hg_proj"][l].astype(BF16),
        conv_w=p["conv_w"][l], conv_b=p["conv_b"][l].reshape(1, CONV_DIM),
        dtb=_per_group(p["dt_bias"][l]),
        a=_per_group(-jnp.exp(p["a_log"][l].astype(F32))),
        dsk=jnp.repeat(p["d_skip"][l], SSD_HEADDIM).reshape(1, SSD_INNER),
        ssd_nw=p["ssd_norm_w"][l].reshape(1, SSD_INNER),
        ssp=p["ssd_proj"][l].astype(BF16),
        wo=p["w_out"][l].astype(BF16),
        ln1_g=p["ln1_g"][l].reshape(1, D_MODEL), ln1_b=p["ln1_b"][l].reshape(1, D_MODEL),
        rwh=rwh, rwl=rwl,
        rb=jnp.pad(p["router_b"][l], (0, LANES - N_EXPERTS)).reshape(1, LANES),
        ln2_g=p["ln2_g"][l].reshape(1, D_MODEL), ln2_b=p["ln2_b"][l].reshape(1, D_MODEL),
    )


def _prep_experts(p):
    return dict(
        wgu=p["w_gu"], bgu=p["b_gu"].reshape(DEPTH, N_EXPERTS, 1, 2 * D_FF),
        wd=p["w_down"], bd=p["b_down"].reshape(DEPTH, N_EXPERTS, 1, D_MODEL),
    )


def _moe(h, h3, logits, lp, ep, li, tm):
    t = h.shape[0]
    s = t * TOP_K
    top_v, top_e = lax.top_k(logits[:, :N_EXPERTS], TOP_K)
    gates = jax.nn.softmax(top_v, axis=-1)
    e_slot = top_e.reshape(s).astype(jnp.int32)
    slot_bits = max(s - 1, 1).bit_length()
    order = jnp.sort((e_slot << slot_bits) | jnp.arange(s, dtype=jnp.int32)) & ((1 << slot_bits) - 1)
    counts = jnp.sum((e_slot[:, None] == jnp.arange(N_EXPERTS, dtype=jnp.int32)[None, :]).astype(jnp.int32), axis=0)
    padded = (counts + MOE_BM - 1) // MOE_BM * MOE_BM
    pend = jnp.cumsum(padded)
    cend = pend - padded + counts
    n_blocks = (s + N_EXPERTS * (MOE_BM - 1) + MOE_BM - 1) // MOE_BM
    nrows = n_blocks * MOE_BM
    rows = jnp.arange(nrows, dtype=jnp.int32)[:, None]
    before = pend[None, :] <= rows
    pad_before = jnp.sum(jnp.where(before, padded - counts, 0), axis=1)
    is_pad = jnp.any((cend[None, :] <= rows) & (rows < pend[None, :]), axis=1) | (rows[:, 0] >= pend[-1])
    slot = order[jnp.clip(rows[:, 0] - pad_before, 0, s - 1)]
    tok = slot // TOP_K
    row_tok = jnp.where(is_pad, 0, tok)
    row_dst = jnp.where(is_pad, TOP_K * t + rows[:, 0] % (2 * MOE_BM), (slot - tok * TOP_K) * t + tok)
    blk_rows = jnp.arange(n_blocks, dtype=jnp.int32)[:, None] * MOE_BM
    block_e = jnp.minimum(jnp.sum((pend[None, :] <= blk_rows).astype(jnp.int32), axis=1), N_EXPERTS - 1)
    n_used = (pend[-1:] // MOE_BM).astype(jnp.int32)
    first_prev = TOP_K * t + MOE_BM + jnp.arange(MOE_BM, dtype=jnp.int32)
    row_dst_prev = jnp.concatenate([first_prev, row_dst[:-MOE_BM]])
    ys = _moe_experts(block_e, n_used, row_tok.reshape(n_blocks, 1, MOE_BM), row_dst.reshape(n_blocks, 1, MOE_BM),
                      row_dst_prev.reshape(n_blocks, 1, MOE_BM), h3, ep["wgu"], ep["bgu"], ep["wd"], ep["bd"], li)
    return _combine(h, ys, gates, lp["ln2_g"], lp["ln2_b"], tm)


def _mixers(xw, li, lp, grp, new):
    bsz, l, valid, base = grp["bsz"], grp["l"], grp["valid"], grp["base"]
    t = bsz * l
    prev8 = jnp.pad(grp["s_conv"][li], ((0, 0), (SUBLANES - (CONV_K - 1), 0), (0, 0)))
    xc = _conv(xw, prev8, lp["conv_w"], lp["conv_b"], bsz, l, base)
    tail = base + jnp.arange(bsz, dtype=jnp.int32)[:, None] * l + jnp.arange(valid - (CONV_K - 1), valid)[None, :]
    conv_new = xw[tail.reshape(-1)][:, OFF_XBC:OFF_XBC + CONV_DIM].reshape(bsz, CONV_K - 1, CONV_DIM)
    o, hg = _hgrn(xw, grp["s_hg"], new["hg"], li, lp["la"], lp["l1m"], lp["oml"], lp["hg_nw"], bsz, l, valid,
                  base // ROW_TILE)
    y, ssm = _ssd(xc.reshape(t // ROW_TILE, ROW_TILE, CONV_DIM), xw, grp["s_ssm"], new["ssm"], li, lp["dtb"],
                  lp["a"], lp["dsk"], lp["ssd_nw"], bsz, l, valid, base // ROW_TILE)
    return o, y, dict(hg=hg, ssm=ssm, conv=new["conv"] + [conv_new])


def _forward(x, groups, layers, ep):
    t = x.shape[0]
    xb = x.astype(BF16)
    new = [dict(hg=jnp.zeros(g["s_hg"].shape, F32), ssm=jnp.zeros(g["s_ssm"].shape, F32), conv=[]) for g in groups]
    for li, lp in enumerate(layers):
        xw = _matmul(xb, lp["w_in"], IN_PROJ_ROWS if t % IN_PROJ_ROWS == 0 else TOKEN_TILE, N_W // 4)
        mix = [_mixers(xw, li, lp, grp, n) for grp, n in zip(groups, new)]
        new = [m[2] for m in mix]
        h, h3, logits = _postmix(mix[0][0], mix[1][0], mix[0][1], mix[1][1], xw, x, lp["hgp"], lp["ssp"], lp["wo"],
                             lp["ln1_g"], lp["ln1_b"], lp["rwh"], lp["rwl"], lp["rb"])
        x, xb = _moe(h, h3, logits, lp, ep, li, TOKEN_TILE)
    return x, [dict(hg=n["hg"], ssm=n["ssm"], conv=jnp.stack(n["conv"])) for n in new]


def kernel(x_prompt, x_sample, state_hgrn, state_ssm, state_conv, hg_lower_bounds, w_in, hg_norm_w, hg_proj,
           conv_w, conv_b, dt_bias, a_log, d_skip, ssd_norm_w, ssd_proj, w_out, ln1_g, ln1_b, router_w,
           router_b, w_gu, b_gu, w_down, b_down, ln2_g, ln2_b):
    lb_all = jnp.cumsum(jax.nn.softmax(hg_lower_bounds.astype(F32), axis=0), axis=0)
    lb_all = lb_all - lb_all[0]
    p = dict(lb_all=lb_all, w_in=w_in, hg_norm_w=hg_norm_w, hg_proj=hg_proj, conv_w=conv_w, conv_b=conv_b,
             dt_bias=dt_bias, a_log=a_log, d_skip=d_skip, ssd_norm_w=ssd_norm_w, ssd_proj=ssd_proj, w_out=w_out,
             ln1_g=ln1_g, ln1_b=ln1_b, router_w=router_w, router_b=router_b, w_gu=w_gu, b_gu=b_gu,
             w_down=w_down, b_down=b_down, ln2_g=ln2_g, ln2_b=ln2_b)
    layers = [_prep_layer(p, l) for l in range(DEPTH)]
    ep = _prep_experts(p)

    return _run(x_prompt, x_sample, state_hgrn, state_ssm, state_conv, layers, ep)


def _run(x_prompt, x_sample, state_hgrn, state_ssm, state_conv, layers, ep):
    bp, lprompt, _ = x_prompt.shape
    bs, ls, _ = x_sample.shape
    tp = bp * lprompt

    def pairs(s_ssm):
        return s_ssm.reshape(s_ssm.shape[:2] + (SSD_PAIRS, LANES, SSD_STATE))

    groups = [
        dict(bsz=bp, l=lprompt, valid=lprompt, base=0,
             s_hg=jnp.zeros((DEPTH, bp) + state_hgrn.shape[2:], F32),
             s_ssm=pairs(jnp.zeros((DEPTH, bp) + state_ssm.shape[2:], F32)),
             s_conv=jnp.zeros((DEPTH, bp) + state_conv.shape[2:], F32)),
        dict(bsz=bs, l=SUBLANES, valid=ls, base=tp, s_hg=state_hgrn, s_ssm=pairs(state_ssm), s_conv=state_conv),
    ]
    xs = jnp.pad(x_sample, ((0, 0), (0, SUBLANES - ls), (0, 0)))
    x = jnp.concatenate([x_prompt.reshape(tp, D_MODEL), xs.reshape(bs * SUBLANES, D_MODEL)], axis=0)
    x, new = _forward(x, groups, layers, ep)
    y_p = x[:tp].reshape(bp, lprompt, D_MODEL)
    y_s = x[tp:].reshape(bs, SUBLANES, D_MODEL)[:, :ls]
    (n_p, n_s) = new
    return (y_p, y_s, n_p["hg"], n_p["ssm"].reshape((DEPTH, bp) + state_ssm.shape[2:]), n_p["conv"],
            n_s["hg"], n_s["ssm"].reshape(state_ssm.shape), n_s["conv"])
```

```python
import functools

import jax
import jax.numpy as jnp
from jax import lax
from jax.experimental import pallas as pl
from jax.experimental.pallas import tpu as pltpu

F32 = jnp.float32
BF16 = jnp.bfloat16

D_MODEL = 1024
DEPTH = 2
HG_HEADS = 8
HG_DK = 128
HG_DV = 128
HG_DIM = HG_HEADS * HG_DK
SSD_INNER = 2 * D_MODEL
SSD_HEADDIM = 64
SSD_HEADS = SSD_INNER // SSD_HEADDIM
SSD_GROUPS = 4
SSD_HPG = SSD_HEADS // SSD_GROUPS
SSD_STATE = 128
SSD_PAIRS = SSD_HEADS // 2
PAIRS_PER_GROUP = SSD_PAIRS // SSD_GROUPS
GROUP_W = SSD_INNER // SSD_GROUPS
CONV_K = 4
CONV_DIM = SSD_INNER + 2 * SSD_GROUPS * SSD_STATE
N_EXPERTS = 32
TOP_K = 4
D_FF = D_MODEL
SWIGLU_LIMIT = 7.0
SWIGLU_ALPHA = 1.702
ALPHA = (2.0 * DEPTH) ** 0.25
LN_EPS = 1e-5
RMS_EPS = 1e-6

LANES = 128
SUBLANES = 8
ROW_SUB = D_MODEL // LANES

DT_W = SSD_GROUPS * LANES
OFF_Q = 0
OFF_F = HG_DIM
OFF_V = 2 * HG_DIM
OFF_G = 3 * HG_DIM
OFF_Z = 4 * HG_DIM
OFF_XBC = OFF_Z + SSD_INNER
OFF_GA = OFF_XBC + CONV_DIM
OFF_GB = OFF_GA + D_MODEL
OFF_DT = OFF_GB + D_MODEL
N_W = OFF_DT + DT_W
IN_SPLITS = (HG_DIM, HG_DIM, HG_DIM, HG_DIM, SSD_INNER, CONV_DIM, SSD_HEADS, D_MODEL, D_MODEL)

ROW_TILE = 128
CONV_ROWS = 512
TOKEN_TILE = 256
IN_PROJ_ROWS = 1024
MOE_BM = 256
MOE_SCATTER_SPLIT = 128
VMEM_LIMIT = 48 * 1024 * 1024
MOE_VMEM_LIMIT = 56 * 1024 * 1024
LOG2E = 1.4426950408889634
HALO = 16
HGRN_HEADS_PER_STEP = 8
HGRN_SUB = 8
SSD_GROUPS_PER_STEP = 4


def _cparams(sem):
    return pltpu.CompilerParams(dimension_semantics=sem, vmem_limit_bytes=VMEM_LIMIT)


def _sigmoid(x):
    return 0.5 * jnp.tanh(0.5 * x) + 0.5


def _log1p(x):
    return jnp.log(1.0 + x)


def _dot(a, b):
    return jnp.dot(a, b, preferred_element_type=F32)


def _dot_nt(a, b):
    return lax.dot_general(a, b, (((1,), (1,)), ((), ())), preferred_element_type=F32)


def _dot_tn(a, b):
    return lax.dot_general(a, b, (((0,), (0,)), ((), ())), preferred_element_type=F32)


def _load_rows(ref3):
    return pltpu.einshape("tsl->t(sl)", ref3[...])


def _store_rows(ref3, x):
    ref3[...] = pltpu.einshape("t(sl)->tsl", x, s=ROW_SUB)


def _mm_kernel(x_ref, w_ref, o_ref):
    o_ref[...] = _dot(x_ref[...], w_ref[...])


def _matmul(x, w, tm, tn):
    m, k = x.shape
    n = w.shape[1]
    return pl.pallas_call(
        _mm_kernel,
        grid=(n // tn, m // tm),
        in_specs=[pl.BlockSpec((tm, k), lambda j, i: (i, 0)),
                  pl.BlockSpec((k, tn), lambda j, i: (0, j))],
        out_specs=pl.BlockSpec((tm, tn), lambda j, i: (i, j)),
        out_shape=jax.ShapeDtypeStruct((m, n), F32),
        compiler_params=_cparams(("arbitrary", "arbitrary")),
        name="in_proj",
    )(x, w)


def _conv_kernel(x_ref, pb_ref, p0_ref, w_ref, b_ref, o_ref):
    i = pl.program_id(1)
    x = x_ref[...]
    prev = jnp.where(i == 0, p0_ref[0], pb_ref[...])
    full = jnp.concatenate([prev, x], axis=0)
    w = w_ref[...]
    acc = b_ref[...] + w[CONV_K - 1:CONV_K] * x
    for s in range(1, CONV_K):
        xs = pltpu.roll(full, s, 0)[SUBLANES:]
        acc = acc + w[CONV_K - 1 - s:CONV_K - s] * xs
    o_ref[...] = acc * _sigmoid(acc)


def _conv_short_kernel(x_ref, p_ref, w_ref, b_ref, o_ref):
    x = x_ref[...]
    p = p_ref[...]
    rc = lax.broadcasted_iota(jnp.int32, x.shape, 0) & (SUBLANES - 1)
    w = w_ref[...]
    acc = b_ref[...] + w[CONV_K - 1:CONV_K] * x
    for s in range(1, CONV_K):
        xs = jnp.where(rc >= s, pltpu.roll(x, s, 0), pltpu.roll(p, (s - SUBLANES) % ROW_TILE, 0))
        acc = acc + w[CONV_K - 1 - s:CONV_K - s] * xs
    o_ref[...] = acc * _sigmoid(acc)


def _conv(xw, prev8, conv_w, conv_b, bsz, l, base_rows):
    cb = OFF_XBC // CONV_DIM
    wspec = [pl.BlockSpec((CONV_K, CONV_DIM), lambda *_: (0, 0)), pl.BlockSpec((1, CONV_DIM), lambda *_: (0, 0))]
    out_shape = jax.ShapeDtypeStruct((bsz * l, CONV_DIM), F32)
    if l == SUBLANES:
        r = ROW_TILE
        return pl.pallas_call(
            _conv_short_kernel,
            grid=(bsz * l // r,),
            in_specs=[pl.BlockSpec((r, CONV_DIM), lambda i: (base_rows // r + i, cb)),
                      pl.BlockSpec((r, CONV_DIM), lambda i: (i, 0))] + wspec,
            out_specs=pl.BlockSpec((r, CONV_DIM), lambda i: (i, 0)),
            out_shape=out_shape,
            compiler_params=_cparams(("arbitrary",)),
            name="conv_silu_short",
        )(xw, prev8.reshape(bsz * SUBLANES, CONV_DIM), conv_w, conv_b)
    r = CONV_ROWS
    tps = l // r
    rb = r // SUBLANES
    return pl.pallas_call(
        _conv_kernel,
        grid=(bsz, tps),
        in_specs=[pl.BlockSpec((r, CONV_DIM), lambda bi, i: (base_rows // r + bi * tps + i, cb)),
                  pl.BlockSpec((SUBLANES, CONV_DIM),
                               lambda bi, i: (jnp.maximum(base_rows // SUBLANES + (bi * tps + i) * rb - 1, 0), cb)),
                  pl.BlockSpec((1, SUBLANES, CONV_DIM), lambda bi, i: (bi, 0, 0))] + wspec,
        out_specs=pl.BlockSpec((r, CONV_DIM), lambda bi, i: (bi * tps + i, 0)),
        out_shape=out_shape,
        compiler_params=_cparams(("arbitrary", "arbitrary")),
        name="conv_silu",
    )(xw, xw, prev8, conv_w, conv_b)


def _pad_rows(x, rows):
    if x.shape[0] == rows:
        return x
    return jnp.concatenate([x, jnp.zeros((rows - x.shape[0], x.shape[1]), x.dtype)], axis=0)


def _split3(x):
    hi = x.astype(BF16)
    r1 = x - hi.astype(F32)
    mid = r1.astype(BF16)
    lo = (r1 - mid.astype(F32)).astype(BF16)
    return hi, mid, lo


def _hgrn_kernel(*refs, hps, nseq, nc, **kw):
    refs = refs[:9] + refs[10:]
    s0_ref, sout_ref, st_ref = refs[8], refs[10], refs[11]
    c = pl.program_id(2)

    @pl.when(c == 0)
    def _():
        for hh in range(hps):
            for i in range(nseq):
                st_ref[hh, i] = s0_ref[0, i, hh].T

    for hh in range(hps):
        _hgrn_head(hh, *refs, nseq=nseq, **kw)

    @pl.when(c == nc - 1)
    def _():
        for hh in range(hps):
            for i in range(nseq):
                sout_ref[0, i, hh] = st_ref[hh, i].T


def _hgrn_head(hh, q_ref, f_ref, v_ref, g_ref, la_ref, l1m_ref, oml_ref, nw_ref, s0_ref,
               o_ref, sout_ref, st_ref, k_scr, b_scr, *, ch, sub, nseq, valid):
    r = ROW_TILE
    cols = slice(hh * LANES, (hh + 1) * LANES)
    q = q_ref[:, cols]
    fr = f_ref[:, cols]
    v = v_ref[:, cols]
    q = q * _sigmoid(q)
    la = la_ref[hh]
    lsig = jnp.minimum(fr, 0.0) - _log1p(jnp.exp(-jnp.abs(fr)))
    cc = l1m_ref[hh] + lsig
    logf = jnp.maximum(la, cc) + _log1p(jnp.exp(-jnp.abs(la - cc)))
    k = oml_ref[hh] * _sigmoid(-fr)

    row = lax.broadcasted_iota(jnp.int32, (r, LANES), 0)
    lane = lax.broadcasted_iota(jnp.int32, (r, LANES), 1)
    if valid < ch:
        pad = (row & (ch - 1)) >= valid
        logf = jnp.where(pad, 0.0, logf)
        k = jnp.where(pad, 0.0, k)

    tri = lane <= row
    if ch < r:
        shift = ch.bit_length() - 1
        tri = tri & ((row >> shift) == (lane >> shift))
    tri = jnp.where(tri, 1.0, 0.0).astype(BF16)
    hi, mid, lo = _split3(logf)
    b2 = (_dot(tri, hi) + _dot(tri, mid) + _dot(tri, lo)) * LOG2E

    k_scr[hh, 0:HALO] = jnp.zeros((HALO, LANES), F32)
    b_scr[hh, 0:HALO] = jnp.zeros((HALO, LANES), F32)
    k_scr[hh, HALO:HALO + r] = k
    b_scr[hh, HALO:HALO + r] = b2
    rs = row & (sub - 1)
    scl = jnp.where(lane == rs, jnp.sum(q * k, axis=-1, keepdims=True), 0.0)
    for d in range(1, sub):
        kd = k_scr[hh, pl.ds(HALO - d, r), :]
        bd = b_scr[hh, pl.ds(HALO - d, r), :]
        sd = jnp.sum(q * kd * jnp.exp2(b2 - bd), axis=-1, keepdims=True)
        scl = jnp.where(lane == rs - d, sd, scl)

    spc = ch // sub
    pieces = []
    for m in range(r // sub):
        lo_, hi_ = m * sub, (m + 1) * sub
        piece = scl[lo_:hi_]
        if m:
            piece = pltpu.roll(piece, lo_, 1)
        j = m % spc
        if j:
            cs = (m - j) * sub
            ref_b = b2[lo_ - 1:lo_]
            qj = q[lo_:hi_] * jnp.exp2(b2[lo_:hi_] - ref_b)
            kk = k[cs:lo_] * jnp.exp2(ref_b - b2[cs:lo_])
            parts = ([jnp.zeros((cs, LANES), F32)] if cs else []) + [kk, jnp.zeros((r - lo_, LANES), F32)]
            piece = piece + _dot_nt(qj.astype(BF16), jnp.concatenate(parts, axis=0).astype(BF16))
        pieces.append(piece)
    sc = jnp.concatenate(pieces, axis=0)
    o = _dot(sc.astype(BF16), v.astype(BF16))

    inter = []
    for i in range(nseq):
        lo_, hi_ = i * ch, (i + 1) * ch
        bi = b2[lo_:hi_]
        bl = bi[ch - 1:ch]
        st = st_ref[hh, i]
        inter.append(_dot_nt((q[lo_:hi_] * jnp.exp2(bi)).astype(BF16), st.astype(BF16)))
        kh = _pad_rows(k[lo_:hi_] * jnp.exp2(bl - bi), r)
        vi = _pad_rows(v[lo_:hi_], r)
        st_ref[hh, i] = st * jnp.exp2(bl) + _dot_tn(vi.astype(BF16), kh.astype(BF16))
    o = o + (inter[0] if nseq == 1 else jnp.concatenate(inter, axis=0))

    ms = jnp.mean(o * o, axis=-1, keepdims=True)
    o_ref[:, cols] = o * lax.rsqrt(ms + RMS_EPS) * nw_ref[hh] * _sigmoid(g_ref[:, cols])


def _seq_tiling(bsz, l):
    r = ROW_TILE
    ch = min(l, r)
    nseq = r // ch
    nc = max(l // r, 1)
    nb = bsz * l // (r * nc)
    return ch, nseq, nc, nb


def _carry_alias(new_all, n_in):
    return [pl.BlockSpec(memory_space=pl.ANY)], [new_all], {n_in: 1}


def _hgrn(xw, s0_all, new_all, li, la, l1m, oml, nw, bsz, l, valid, base_tile):
    r = ROW_TILE
    ch, nseq, nc, nb = _seq_tiling(bsz, l)
    sub = min(ch, HGRN_SUB)
    hps = HGRN_HEADS_PER_STEP
    w = hps * LANES
    extra_specs, extra_args, aliases = _carry_alias(new_all, 9)
    kern = functools.partial(_hgrn_kernel, hps=hps, ch=ch, sub=sub, nseq=nseq, valid=valid, nc=nc)

    def xspec(off):
        return pl.BlockSpec((r, w), lambda bb, h, c: (base_tile + bb * nc + c, off // w + h))

    def pspec():
        return pl.BlockSpec((hps, 1, LANES), lambda bb, h, c: (h, 0, 0))

    return pl.pallas_call(
        kern,
        grid=(nb, HG_HEADS // hps, nc),
        in_specs=[xspec(OFF_Q), xspec(OFF_F), xspec(OFF_V), xspec(OFF_G),
                  pspec(), pspec(), pspec(), pspec(),
                  pl.BlockSpec((1, nseq, hps, HG_DK, HG_DV), lambda bb, h, c: (li, bb, h, 0, 0))] + extra_specs,
        out_specs=[pl.BlockSpec((r, w), lambda bb, h, c: (bb * nc + c, h)),
                   pl.BlockSpec((1, nseq, hps, HG_DK, HG_DV), lambda bb, h, c: (li, bb, h, 0, 0))],
        out_shape=[jax.ShapeDtypeStruct((bsz * l, HG_DIM), F32),
                   jax.ShapeDtypeStruct(s0_all.shape, F32)],
        scratch_shapes=[pltpu.VMEM((hps, nseq, HG_DV, HG_DK), F32),
                        pltpu.VMEM((hps, HALO + r, LANES), F32), pltpu.VMEM((hps, HALO + r, LANES), F32)],
        input_output_aliases=aliases,
        compiler_params=_cparams(("arbitrary", "arbitrary", "arbitrary")),
        name="hgrn2",
    )(xw, xw, xw, xw, la, l1m, oml, nw, s0_all, *extra_args)


def _ssd_kernel(*refs, **kw):
    _ssd_body(*(refs[:10] + refs[11:]), **kw)


def _ssd_body(xs_ref, bm_ref, cm_ref, z_ref, dt_ref, dtb_ref, a_ref, dsk_ref, nw_ref, h0_ref,
              y_ref, hout_ref, h_ref, al_scr, dl_scr, *, gps, nc, **kw):
    c = pl.program_id(2)

    @pl.when(c == 0)
    def _():
        h_ref[...] = h0_ref[0]

    for gg in range(gps):
        _ssd_group(gg, xs_ref, bm_ref, cm_ref, z_ref, dt_ref, dtb_ref, a_ref, dsk_ref, nw_ref, y_ref, h_ref,
                   al_scr, dl_scr, **kw)

    @pl.when(c == nc - 1)
    def _():
        hout_ref[0] = h_ref[...]


def _ssd_group(gg, xs_ref, bm_ref, cm_ref, z_ref, dt_ref, dtb_ref, a_ref, dsk_ref, nw_ref, y_ref, h_ref,
               al_scr, dl_scr, *, ch, nseq, valid):
    r = ROW_TILE
    scol = slice(gg * LANES, (gg + 1) * LANES)
    goff = gg * GROUP_W
    poff = gg * PAIRS_PER_GROUP
    shift = ch.bit_length() - 1
    rowl = lax.broadcasted_iota(jnp.int32, (r, LANES), 0)
    lane = lax.broadcasted_iota(jnp.int32, (r, LANES), 1)
    rc = rowl & (ch - 1)
    dtr = dt_ref[:, scol] + dtb_ref[:, scol]
    dt = jnp.maximum(dtr, 0.0) + _log1p(jnp.exp(-jnp.abs(dtr)))
    if valid < ch:
        dt = jnp.where(rc < valid, dt, 0.0)
    mask = (lane <= rowl) & ((rowl >> shift) == (lane >> shift))
    tri = jnp.where(mask, 1.0, 0.0).astype(BF16)
    hi, mid, lo = _split3(dt * a_ref[:, scol])
    acum = _dot(tri, hi) + _dot(tri, mid) + _dot(tri, lo)
    acum_t = acum.T
    dt_t = dt.T

    even_lane = lane < SSD_HEADDIM
    even_row = rowl < SSD_HEADDIM
    last_row = rc == ch - 1

    bmb = bm_ref[0, :, scol].astype(BF16)
    cmb = cm_ref[0, :, scol].astype(BF16)
    cb = _dot_nt(cmb, bmb)
    xs = xs_ref[0, :, goff:goff + GROUP_W]
    z = z_ref[:, goff:goff + GROUP_W]
    for pp in range(PAIRS_PER_GROUP):
        xp = xs[:, pp * LANES:(pp + 1) * LANES]
        acc = dsk_ref[:, goff + pp * LANES:goff + (pp + 1) * LANES] * xp
        for e in range(2):
            hd = 2 * pp + e
            w = cb * jnp.exp(jnp.where(mask, acum[:, hd:hd + 1] - acum_t[hd:hd + 1, :], -jnp.inf)) * dt_t[hd:hd + 1, :]
            xm = jnp.where(even_lane if e == 0 else jnp.logical_not(even_lane), xp, 0.0)
            acc = acc + _dot(w.astype(BF16), xm.astype(BF16))
        a_lane = jnp.where(even_lane, acum[:, 2 * pp:2 * pp + 1], acum[:, 2 * pp + 1:2 * pp + 2])
        dt_lane = jnp.where(even_lane, dt[:, 2 * pp:2 * pp + 1], dt[:, 2 * pp + 1:2 * pp + 2])
        ea = jnp.exp(a_lane)

        def seq_body(i, acc, pp=pp, xp=xp, a_lane=a_lane, dt_lane=dt_lane, ea=ea):
            in_seq = (rowl >> shift) == i
            al_lane = jnp.sum(jnp.where(in_seq & last_row, a_lane, 0.0), axis=0, keepdims=True)
            hp = h_ref[i, poff + pp]
            yi = _dot_nt(cmb, hp.astype(BF16)) * ea
            acc = acc + jnp.where(in_seq, yi, 0.0)
            xw = jnp.where(in_seq, xp * (jnp.exp(al_lane - a_lane) * dt_lane), 0.0)
            dec = jnp.where(even_row, jnp.exp(al_lane[:, 0:1]), jnp.exp(al_lane[:, LANES - 1:LANES]))
            h_ref[i, poff + pp] = dec * hp + _dot_tn(xw.astype(BF16), bmb)
            return acc

        if nseq == 1:
            acc = seq_body(0, acc)
        else:
            al_scr[pp] = a_lane
            dl_scr[pp] = dt_lane
        y_ref[:, goff + pp * LANES:goff + (pp + 1) * LANES] = acc

    if nseq > 1:
        def short_body(i, carry):
            rows = pl.ds(pl.multiple_of(i * ch, ch), ch)
            cm_i = _pad_rows(cm_ref[0, rows, scol], 2 * SUBLANES).astype(BF16)
            bm_i = _pad_rows(bm_ref[0, rows, scol], 2 * SUBLANES).astype(BF16)
            for pp in range(PAIRS_PER_GROUP):
                cols = slice(goff + pp * LANES, goff + (pp + 1) * LANES)
                a_i = al_scr[pp, rows, :]
                al = a_i[ch - 1:ch]
                hp = h_ref[i, poff + pp]
                yi = _dot_nt(cm_i, hp.astype(BF16))[:ch] * jnp.exp(a_i)
                y_ref[rows, cols] = y_ref[rows, cols] + yi
                xw = xs_ref[0, rows, cols] * (jnp.exp(al - a_i) * dl_scr[pp, rows, :])
                dec = jnp.where(even_row, jnp.exp(al[:, 0:1]), jnp.exp(al[:, LANES - 1:LANES]))
                h_ref[i, poff + pp] = dec * hp + _dot_tn(_pad_rows(xw, 2 * SUBLANES).astype(BF16), bm_i)
            return carry
        lax.fori_loop(0, nseq, short_body, 0, unroll=4)

    y = y_ref[:, goff:goff + GROUP_W] * (z * _sigmoid(z))
    ms = jnp.mean(y * y, axis=-1, keepdims=True)
    y_ref[:, goff:goff + GROUP_W] = y * lax.rsqrt(ms + RMS_EPS) * nw_ref[:, goff:goff + GROUP_W]


def _ssd(xc3, xw, h0_all, new_all, li, dtb, a, dsk, nw, bsz, l, valid, base_tile):
    r = ROW_TILE
    ch, nseq, nc, nb = _seq_tiling(bsz, l)
    gps = SSD_GROUPS_PER_STEP if nseq == 1 else 1
    gw, sw, pairs = gps * GROUP_W, gps * SSD_STATE, gps * PAIRS_PER_GROUP
    extra_specs, extra_args, aliases = _carry_alias(new_all, 10)
    kern = functools.partial(_ssd_kernel, gps=gps, ch=ch, nseq=nseq, valid=valid, nc=nc)
    hspec = pl.BlockSpec((1, nseq, pairs, LANES, SSD_STATE), lambda bb, g, c: (li, bb, g, 0, 0))
    b_off = SSD_INNER // sw
    c_off = b_off + SSD_GROUPS // gps

    def vec(w):
        return pl.BlockSpec((1, w), lambda bb, g, c: (0, g))

    return pl.pallas_call(
        kern,
        grid=(nb, SSD_GROUPS // gps, nc),
        in_specs=[pl.BlockSpec((1, r, gw), lambda bb, g, c: (bb * nc + c, 0, g)),
                  pl.BlockSpec((1, r, sw), lambda bb, g, c: (bb * nc + c, 0, b_off + g)),
                  pl.BlockSpec((1, r, sw), lambda bb, g, c: (bb * nc + c, 0, c_off + g)),
                  pl.BlockSpec((r, gw), lambda bb, g, c: (base_tile + bb * nc + c, OFF_Z // gw + g)),
                  pl.BlockSpec((r, sw), lambda bb, g, c: (base_tile + bb * nc + c, OFF_DT // sw + g)),
                  vec(sw), vec(sw), vec(gw), vec(gw), hspec] + extra_specs,
        out_specs=[pl.BlockSpec((r, gw), lambda bb, g, c: (bb * nc + c, g)), hspec],
        out_shape=[jax.ShapeDtypeStruct((bsz * l, SSD_INNER), F32),
                   jax.ShapeDtypeStruct(h0_all.shape, F32)],
        scratch_shapes=[pltpu.VMEM((nseq, pairs, LANES, SSD_STATE), F32),
                        pltpu.VMEM((PAIRS_PER_GROUP, r, LANES), F32), pltpu.VMEM((PAIRS_PER_GROUP, r, LANES), F32)],
        input_output_aliases=aliases,
        compiler_params=_cparams(("arbitrary", "arbitrary", "arbitrary")),
        name="ssd",
    )(xc3, xc3, xc3, xw, xw, dtb, a, dsk, nw, h0_all, *extra_args)


def _layer_norm(x, g, b):
    mu = jnp.mean(x, axis=-1, keepdims=True)
    xc = x - mu
    var = jnp.mean(xc * xc, axis=-1, keepdims=True)
    return xc * lax.rsqrt(var + LN_EPS) * g + b


def _postmix_kernel(op_ref, os_ref, yp_ref, ys_ref, ga_ref, gb_ref, x_ref, hgp_ref, ssp_ref, wo_ref, g_ref, b_ref,
                    rwh_ref, rwl_ref, rb_ref, h_ref, h3_ref, lg_ref, *, n_p):
    first = pl.program_id(0) < n_p
    o = jnp.where(first, op_ref[...], os_ref[...])
    y = jnp.where(first, yp_ref[...], ys_ref[...])
    out_a = _dot(o.astype(BF16), hgp_ref[...])
    out_b = _dot(y.astype(BF16), ssp_ref[...])
    merged = _sigmoid(ga_ref[...]) * out_a + _sigmoid(gb_ref[...]) * out_b
    mix = _dot(merged.astype(BF16), wo_ref[...])
    h = _layer_norm(ALPHA * x_ref[...] + mix, g_ref[...], b_ref[...])
    h_ref[...] = h
    _store_rows(h3_ref, h)
    hh = h.astype(BF16)
    hl = (h - hh.astype(F32)).astype(BF16)
    lg_ref[...] = (_dot(hh, rwh_ref[...]) + _dot(hl, rwh_ref[...]) + _dot(hh, rwl_ref[...])) + rb_ref[...]


def _postmix(o_p, o_s, y_p, y_s, xw, x, hgp, ssp, wo, g, b, rwh, rwl, rb):
    t = x.shape[0]
    tm = TOKEN_TILE
    n_p = o_p.shape[0] // tm

    def full(a):
        return pl.BlockSpec(a.shape, lambda i: (0, 0))

    def first(w):
        return pl.BlockSpec((tm, w), lambda i: (jnp.minimum(i, n_p - 1), 0))

    def second(w):
        return pl.BlockSpec((tm, w), lambda i: (jnp.maximum(i - n_p, 0), 0))

    return pl.pallas_call(
        functools.partial(_postmix_kernel, n_p=n_p),
        grid=(t // tm,),
        in_specs=[first(HG_DIM), second(HG_DIM), first(SSD_INNER), second(SSD_INNER),
                  pl.BlockSpec((tm, D_MODEL), lambda i: (i, OFF_GA // D_MODEL)),
                  pl.BlockSpec((tm, D_MODEL), lambda i: (i, OFF_GB // D_MODEL)),
                  pl.BlockSpec((tm, D_MODEL), lambda i: (i, 0)),
                  full(hgp), full(ssp), full(wo), full(g), full(b), full(rwh), full(rwl), full(rb)],
        out_specs=[pl.BlockSpec((tm, D_MODEL), lambda i: (i, 0)),
                   pl.BlockSpec((tm, ROW_SUB, LANES), lambda i: (i, 0, 0)),
                   pl.BlockSpec((tm, LANES), lambda i: (i, 0))],
        out_shape=[jax.ShapeDtypeStruct((t, D_MODEL), F32),
                   jax.ShapeDtypeStruct((t, ROW_SUB, LANES), F32),
                   jax.ShapeDtypeStruct((t, LANES), F32)],
        compiler_params=_cparams(("arbitrary",)),
        name="postmix",
    )(o_p, o_s, y_p, y_s, xw, xw, x, hgp, ssp, wo, g, b, rwh, rwl, rb)


def _moe_kernel(be_ref, nu_ref, tokc_ref, tokn_ref, dstp_ref, dstc_ref, h_hbm, wgu_ref, bgu_ref, wd_ref, bd_ref,
                ys_hbm, xbuf, obuf, xb_scr, act_scr, gate_scr, wgu_b, wd_b, gsem, ssem, *, nblk, trash_row):
    j = pl.program_id(0)
    nu = nu_ref[0]
    slot = j & 1
    nslot = 1 - slot

    def gather_start(tok_ref, s, i):
        pltpu.make_async_copy(h_hbm.at[pl.ds(tok_ref[0, 0, i], 1)], xbuf.at[s, pl.ds(i, 1)], gsem.at[s]).start()

    def scatter_start(dst_ref, s, i):
        pltpu.make_async_copy(obuf.at[s, pl.ds(i, 1)], ys_hbm.at[pl.ds(dst_ref[0, 0, i], 1)], ssem.at[s]).start()

    def gather_wait(s):
        pltpu.make_async_copy(h_hbm.at[pl.ds(0, MOE_BM)], xbuf.at[s], gsem.at[s]).wait()

    def scatter_wait(s):
        pltpu.make_async_copy(obuf.at[s], ys_hbm.at[pl.ds(0, MOE_BM)], ssem.at[s]).wait()

    @pl.when(j == 0)
    def _():
        obuf[...] = jnp.zeros(obuf.shape, F32)
        pltpu.make_async_copy(obuf.at[0], ys_hbm.at[pl.ds(trash_row, MOE_BM)], ssem.at[0]).start()

        def body(i, carry):
            gather_start(tokc_ref, 0, i)
            return carry
        lax.fori_loop(0, MOE_BM, body, 0, unroll=8)

    first_of_expert = (j == 0) | (be_ref[j] != be_ref[jnp.maximum(j - 1, 0)])

    @pl.when((j < nu) & first_of_expert)
    def _():
        rows = D_MODEL // 8
        for c in range(8):
            wgu_b[c * rows:(c + 1) * rows] = wgu_ref[0, 0, c * rows:(c + 1) * rows].astype(BF16)
            wd_b[c * rows:(c + 1) * rows] = wd_ref[0, 0, c * rows:(c + 1) * rows].astype(BF16)

    @pl.when(j < nu)
    def _():
        gather_wait(slot)
        scatter_wait(slot)
        xb_scr[...] = _load_rows(xbuf.at[slot]).astype(BF16)
        for i in range(MOE_BM):
            gather_start(tokn_ref, nslot, i)
        gate = _dot(xb_scr[...], wgu_b[:, :D_FF]) + bgu_ref[0, 0, :, :D_FF]
        gate_scr[...] = jnp.minimum(gate, SWIGLU_LIMIT)

    run = (j < nu).astype(jnp.int32)

    def up_proj(_, carry):
        for i in range(MOE_SCATTER_SPLIT):
            scatter_start(dstp_ref, nslot, i)
        up = _dot(xb_scr[...], wgu_b[:, D_FF:]) + bgu_ref[0, 0, :, D_FF:]
        up = jnp.clip(up, -SWIGLU_LIMIT, SWIGLU_LIMIT)
        gate = gate_scr[...]
        act_scr[...] = ((up + 1.0) * gate * _sigmoid(SWIGLU_ALPHA * gate)).astype(BF16)
        return carry
    lax.fori_loop(0, run, up_proj, 0)

    def down(_, carry):
        for i in range(MOE_SCATTER_SPLIT, MOE_BM):
            scatter_start(dstp_ref, nslot, i)
        _store_rows(obuf.at[slot], _dot(act_scr[...], wd_b[...]) + bd_ref[0, 0])
        return carry
    lax.fori_loop(0, run, down, 0)

    @pl.when(j == nu - 1)
    def _():
        def body(i, carry):
            scatter_start(dstc_ref, slot, i)
            return carry
        lax.fori_loop(0, MOE_BM, body, 0, unroll=8)

    @pl.when(j == nblk - 1)
    def _():
        scatter_wait((nu - 1) & 1)
        scatter_wait(nu & 1)
        gather_wait(nu & 1)


def _moe_experts(block_e, n_used, row_tok, row_dst, row_dst_prev, h, wgu, bgu, wd, bd, li):
    t = h.shape[0]
    nblk = row_tok.shape[0]
    trash_row = TOP_K * t
    kern = functools.partial(_moe_kernel, nblk=nblk, trash_row=trash_row)

    def smem(imap):
        return pl.BlockSpec((1, 1, MOE_BM), imap, memory_space=pltpu.SMEM)

    grid_spec = pltpu.PrefetchScalarGridSpec(
        num_scalar_prefetch=2,
        grid=(nblk,),
        in_specs=[smem(lambda j, be, nu: (j, 0, 0)),
                  smem(lambda j, be, nu: (jnp.minimum(j + 1, nblk - 1), 0, 0)),
                  smem(lambda j, be, nu: (j, 0, 0)),
                  smem(lambda j, be, nu: (j, 0, 0)),
                  pl.BlockSpec(memory_space=pl.ANY),
                  pl.BlockSpec((1, 1, D_MODEL, 2 * D_FF), lambda j, be, nu: (li, be[j], 0, 0)),
                  pl.BlockSpec((1, 1, 1, 2 * D_FF), lambda j, be, nu: (li, be[j], 0, 0)),
                  pl.BlockSpec((1, 1, D_FF, D_MODEL), lambda j, be, nu: (li, be[j], 0, 0)),
                  pl.BlockSpec((1, 1, 1, D_MODEL), lambda j, be, nu: (li, be[j], 0, 0))],
        out_specs=pl.BlockSpec(memory_space=pl.ANY),
        scratch_shapes=[pltpu.VMEM((2, MOE_BM, ROW_SUB, LANES), F32), pltpu.VMEM((2, MOE_BM, ROW_SUB, LANES), F32),
                        pltpu.VMEM((MOE_BM, D_MODEL), BF16), pltpu.VMEM((MOE_BM, D_FF), BF16),
                        pltpu.VMEM((MOE_BM, D_FF), F32),
                        pltpu.VMEM((D_MODEL, 2 * D_FF), BF16), pltpu.VMEM((D_FF, D_MODEL), BF16),
                        pltpu.SemaphoreType.DMA((2,)), pltpu.SemaphoreType.DMA((2,))],
    )
    return pl.pallas_call(
        kern,
        grid_spec=grid_spec,
        out_shape=jax.ShapeDtypeStruct((trash_row + 2 * MOE_BM, ROW_SUB, LANES), F32),
        compiler_params=pltpu.CompilerParams(dimension_semantics=("arbitrary",), vmem_limit_bytes=MOE_VMEM_LIMIT),
        name="moe_experts",
    )(block_e, n_used, row_tok, row_tok, row_dst_prev, row_dst, h, wgu, bgu, wd, bd)


def _combine_kernel(h_ref, y0_ref, y1_ref, y2_ref, y3_ref, gt_ref, g_ref, b_ref, x_ref, xb_ref):
    gt = gt_ref[...]
    y = gt[:, 0:1] * _load_rows(y0_ref)
    for kk, y_ref in enumerate((y1_ref, y2_ref, y3_ref), start=1):
        y = y + gt[:, kk:kk + 1] * _load_rows(y_ref)
    x = _layer_norm(ALPHA * h_ref[...] + y, g_ref[...], b_ref[...])
    x_ref[...] = x
    xb_ref[...] = x.astype(BF16)


def _combine(h, ys, gates, g, b, tm):
    t = h.shape[0]
    nt = t // tm

    def yspec(kk):
        return pl.BlockSpec((tm, ROW_SUB, LANES), lambda i: (kk * nt + i, 0, 0))

    return pl.pallas_call(
        _combine_kernel,
        grid=(nt,),
        in_specs=[pl.BlockSpec((tm, D_MODEL), lambda i: (i, 0)),
                  yspec(0), yspec(1), yspec(2), yspec(3),
                  pl.BlockSpec((tm, TOP_K), lambda i: (i, 0)),
                  pl.BlockSpec((1, D_MODEL), lambda i: (0, 0)),
                  pl.BlockSpec((1, D_MODEL), lambda i: (0, 0))],
        out_specs=[pl.BlockSpec((tm, D_MODEL), lambda i: (i, 0)),
                   pl.BlockSpec((tm, D_MODEL), lambda i: (i, 0))],
        out_shape=[jax.ShapeDtypeStruct((t, D_MODEL), F32),
                   jax.ShapeDtypeStruct((t, D_MODEL), BF16)],
        compiler_params=_cparams(("arbitrary",)),
        name="combine_ln",
    )(h, ys, ys, ys, ys, gates, g, b)


def _split_cols(a, sizes):
    out, off = [], 0
    for s in sizes:
        out.append(a[..., off:off + s])
        off += s
    return out


def _per_group(vec_heads):
    v = vec_heads.reshape(SSD_GROUPS, SSD_HPG)
    return jnp.pad(v, ((0, 0), (0, LANES - SSD_HPG))).reshape(1, DT_W)


def _prep_layer(p, l):
    wq, wf, wv, wg, wz, wxbc, wdt, wga, wgb = _split_cols(p["w_in"][l], IN_SPLITS)
    wdt = jnp.pad(wdt.reshape(D_MODEL, SSD_GROUPS, SSD_HPG), ((0, 0), (0, 0), (0, LANES - SSD_HPG)))
    w_in = jnp.concatenate([wq, wf, wv, wg, wz, wxbc, wga, wgb, wdt.reshape(D_MODEL, DT_W)], axis=1).astype(BF16)
    lb = p["lb_all"][l].reshape(HG_HEADS, 1, HG_DK)
    rw = jnp.pad(p["router_w"][l], ((0, 0), (0, LANES - N_EXPERTS)))
    rwh = rw.astype(BF16)
    rwl = (rw - rwh.astype(F32)).astype(BF16)
    return dict(
        w_in=w_in,
        la=jnp.log(lb), l1m=jnp.log1p(-lb), oml=1.0 - lb,
        hg_nw=p["hg_norm_w"][l].reshape(HG_HEADS, 1, HG_DV),
        hgp=p["hg_proj"][l].astype(BF16),
        conv_w=p["conv_w"][l], conv_b=p["conv_b"][l].reshape(1, CONV_DIM),
        dtb=_per_group(p["dt_bias"][l]),
        a=_per_group(-jnp.exp(p["a_log"][l].astype(F32))),
        dsk=jnp.repeat(p["d_skip"][l], SSD_HEADDIM).reshape(1, SSD_INNER),
        ssd_nw=p["ssd_norm_w"][l].reshape(1, SSD_INNER),
        ssp=p["ssd_proj"][l].astype(BF16),
        wo=p["w_out"][l].astype(BF16),
        ln1_g=p["ln1_g"][l].reshape(1, D_MODEL), ln1_b=p["ln1_b"][l].reshape(1, D_MODEL),
        rwh=rwh, rwl=rwl,
        rb=jnp.pad(p["router_b"][l], (0, LANES - N_EXPERTS)).reshape(1, LANES),
        ln2_g=p["ln2_g"][l].reshape(1, D_MODEL), ln2_b=p["ln2_b"][l].reshape(1, D_MODEL),
    )


def _prep_experts(p):
    return dict(
        wgu=p["w_gu"], bgu=p["b_gu"].reshape(DEPTH, N_EXPERTS, 1, 2 * D_FF),
        wd=p["w_down"], bd=p["b_down"].reshape(DEPTH, N_EXPERTS, 1, D_MODEL),
    )


def _moe(h, h3, logits, lp, ep, li, tm):
    t = h.shape[0]
    s = t * TOP_K
    top_v, top_e = lax.top_k(logits[:, :N_EXPERTS], TOP_K)
    gates = jax.nn.softmax(top_v, axis=-1)
    e_slot = top_e.reshape(s).astype(jnp.int32)
    slot_bits = max(s - 1, 1).bit_length()
    order = jnp.sort((e_slot << slot_bits) | jnp.arange(s, dtype=jnp.int32)) & ((1 << slot_bits) - 1)
    counts = jnp.sum((e_slot[:, None] == jnp.arange(N_EXPERTS, dtype=jnp.int32)[None, :]).astype(jnp.int32), axis=0)
    padded = (counts + MOE_BM - 1) // MOE_BM * MOE_BM
    pend = jnp.cumsum(padded)
    cend = pend - padded + counts
    n_blocks = (s + N_EXPERTS * (MOE_BM - 1) + MOE_BM - 1) // MOE_BM
    nrows = n_blocks * MOE_BM
    rows = jnp.arange(nrows, dtype=jnp.int32)[:, None]
    before = pend[None, :] <= rows
    pad_before = jnp.sum(jnp.where(before, padded - counts, 0), axis=1)
    is_pad = jnp.any((cend[None, :] <= rows) & (rows < pend[None, :]), axis=1) | (rows[:, 0] >= pend[-1])
    slot = order[jnp.clip(rows[:, 0] - pad_before, 0, s - 1)]
    tok = slot // TOP_K
    row_tok = jnp.where(is_pad, 0, tok)
    row_dst = jnp.where(is_pad, TOP_K * t + rows[:, 0] % (2 * MOE_BM), (slot - tok * TOP_K) * t + tok)
    blk_rows = jnp.arange(n_blocks, dtype=jnp.int32)[:, None] * MOE_BM
    block_e = jnp.minimum(jnp.sum((pend[None, :] <= blk_rows).astype(jnp.int32), axis=1), N_EXPERTS - 1)
    n_used = (pend[-1:] // MOE_BM).astype(jnp.int32)
    first_prev = TOP_K * t + MOE_BM + jnp.arange(MOE_BM, dtype=jnp.int32)
    row_dst_prev = jnp.concatenate([first_prev, row_dst[:-MOE_BM]])
    ys = _moe_experts(block_e, n_used, row_tok.reshape(n_blocks, 1, MOE_BM), row_dst.reshape(n_blocks, 1, MOE_BM),
                      row_dst_prev.reshape(n_blocks, 1, MOE_BM), h3, ep["wgu"], ep["bgu"], ep["wd"], ep["bd"], li)
    return _combine(h, ys, gates, lp["ln2_g"], lp["ln2_b"], tm)


def _mixers(xw, li, lp, grp, new):
    bsz, l, valid, base = grp["bsz"], grp["l"], grp["valid"], grp["base"]
    t = bsz * l
    prev8 = jnp.pad(grp["s_conv"][li], ((0, 0), (SUBLANES - (CONV_K - 1), 0), (0, 0)))
    xc = _conv(xw, prev8, lp["conv_w"], lp["conv_b"], bsz, l, base)
    tail = base + jnp.arange(bsz, dtype=jnp.int32)[:, None] * l + jnp.arange(valid - (CONV_K - 1), valid)[None, :]
    conv_new = xw[tail.reshape(-1)][:, OFF_XBC:OFF_XBC + CONV_DIM].reshape(bsz, CONV_K - 1, CONV_DIM)
    o, hg = _hgrn(xw, grp["s_hg"], new["hg"], li, lp["la"], lp["l1m"], lp["oml"], lp["hg_nw"], bsz, l, valid,
                  base // ROW_TILE)
    y, ssm = _ssd(xc.reshape(t // ROW_TILE, ROW_TILE, CONV_DIM), xw, grp["s_ssm"], new["ssm"], li, lp["dtb"],
                  lp["a"], lp["dsk"], lp["ssd_nw"], bsz, l, valid, base // ROW_TILE)
    return o, y, dict(hg=hg, ssm=ssm, conv=new["conv"] + [conv_new])


def _forward(x, groups, layers, ep):
    t = x.shape[0]
    xb = x.astype(BF16)
    new = [dict(hg=jnp.zeros(g["s_hg"].shape, F32), ssm=jnp.zeros(g["s_ssm"].shape, F32), conv=[]) for g in groups]
    for li, lp in enumerate(layers):
        xw = _matmul(xb, lp["w_in"], IN_PROJ_ROWS if t % IN_PROJ_ROWS == 0 else TOKEN_TILE, N_W // 4)
        mix = [_mixers(xw, li, lp, grp, n) for grp, n in zip(groups, new)]
        new = [m[2] for m in mix]
        h, h3, logits = _postmix(mix[0][0], mix[1][0], mix[0][1], mix[1][1], xw, x, lp["hgp"], lp["ssp"], lp["wo"],
                             lp["ln1_g"], lp["ln1_b"], lp["rwh"], lp["rwl"], lp["rb"])
        x, xb = _moe(h, h3, logits, lp, ep, li, TOKEN_TILE)
    return x, [dict(hg=n["hg"], ssm=n["ssm"], conv=jnp.stack(n["conv"])) for n in new]


def kernel(x_prompt, x_sample, state_hgrn, state_ssm, state_conv, hg_lower_bounds, w_in, hg_norm_w, hg_proj,
           conv_w, conv_b, dt_bias, a_log, d_skip, ssd_norm_w, ssd_proj, w_out, ln1_g, ln1_b, router_w,
           router_b, w_gu, b_gu, w_down, b_down, ln2_g, ln2_b):
    lb_all = jnp.cumsum(jax.nn.softmax(hg_lower_bounds.astype(F32), axis=0), axis=0)
    lb_all = lb_all - lb_all[0]
    p = dict(lb_all=lb_all, w_in=w_in, hg_norm_w=hg_norm_w, hg_proj=hg_proj, conv_w=conv_w, conv_b=conv_b,
             dt_bias=dt_bias, a_log=a_log, d_skip=d_skip, ssd_norm_w=ssd_norm_w, ssd_proj=ssd_proj, w_out=w_out,
             ln1_g=ln1_g, ln1_b=ln1_b, router_w=router_w, router_b=router_b, w_gu=w_gu, b_gu=b_gu,
             w_down=w_down, b_down=b_down, ln2_g=ln2_g, ln2_b=ln2_b)
    layers = [_prep_layer(p, l) for l in range(DEPTH)]
    ep = _prep_experts(p)

    return _run(x_prompt, x_sample, state_hgrn, state_ssm, state_conv, layers, ep)


def _run(x_prompt, x_sample, state_hgrn, state_ssm, state_conv, layers, ep):
    bp, lprompt, _ = x_prompt.shape
    bs, ls, _ = x_sample.shape
    tp = bp * lprompt

    def pairs(s_ssm):
        return s_ssm.reshape(s_ssm.shape[:2] + (SSD_PAIRS, LANES, SSD_STATE))

    groups = [
        dict(bsz=bp, l=lprompt, valid=lprompt, base=0,
             s_hg=jnp.zeros((DEPTH, bp) + state_hgrn.shape[2:], F32),
             s_ssm=pairs(jnp.zeros((DEPTH, bp) + state_ssm.shape[2:], F32)),
             s_conv=jnp.zeros((DEPTH, bp) + state_conv.shape[2:], F32)),
        dict(bsz=bs, l=SUBLANES, valid=ls, base=tp, s_hg=state_hgrn, s_ssm=pairs(state_ssm), s_conv=state_conv),
    ]
    xs = jnp.pad(x_sample, ((0, 0), (0, SUBLANES - ls), (0, 0)))
    x = jnp.concatenate([x_prompt.reshape(tp, D_MODEL), xs.reshape(bs * SUBLANES, D_MODEL)], axis=0)
    x, new = _forward(x, groups, layers, ep)
    y_p = x[:tp].reshape(bp, lprompt, D_MODEL)
    y_s = x[tp:].reshape(bs, SUBLANES, D_MODEL)[:, :ls]
    (n_p, n_s) = new
    return (y_p, y_s, n_p["hg"], n_p["ssm"].reshape((DEPTH, bp) + state_ssm.shape[2:]), n_p["conv"],
            n_s["hg"], n_s["ssm"].reshape(state_ssm.shape), n_s["conv"])
```
